```python
import jax, jax.numpy as jnp
from jax import lax
import numpy as np

D_MODEL = 1024
BATCH = 32
SEQ = 2048
DEPTH = 2
DEC_BATCH = 8
DEC_SEQ = 32
PAST_LEN = 2048

CHUNK = 64
N_MIXERS = 2
N_LAYERS_A = (DEPTH + 1) // 2
N_LAYERS_B = DEPTH // 2
EPS = 1e-6
NEG_INF = -1e30

A_HEADS = 16
A_HEAD_DIM = D_MODEL // A_HEADS
A_BAND_CHUNKS = 8
A_WINDOW = A_BAND_CHUNKS * CHUNK
REL_CLIP = 128

B_HEADS = 16
B_NOPE = 64
B_ROPE = 32
B_VDIM = 64
KV_LORA = 256
Q_LORA = 384
B_IN = Q_LORA + KV_LORA + B_ROPE + B_HEADS * B_VDIM
ROPE_THETA = 10000.0
Q_BLOCK = 128

kernel_name = "hybrid_streaming_band_mla_step"


def rms_norm(x, g):
    xf = x.astype(jnp.float32)
    y = xf * lax.rsqrt(jnp.mean(xf * xf, axis=-1, keepdims=True) + EPS)
    return (y * g.astype(jnp.float32)).astype(x.dtype)


def ada_norm(x, c, g, w_ada, b_ada):
    mod = jax.nn.silu(c) @ w_ada + b_ada
    shift, scale, gate = jnp.split(mod[:, None, :], 3, axis=-1)
    return rms_norm(x, g) * (1 + scale) + shift, gate


def rotary(x, pos):
    half = x.shape[-1] // 2
    inv = ROPE_THETA ** (-jnp.arange(half, dtype=jnp.float32) / half)
    ang = pos.astype(jnp.float32)[:, None] * inv[None, :]
    cos = jnp.cos(ang)[:, None, :]
    sin = jnp.sin(ang)[:, None, :]
    xf = x.astype(jnp.float32)
    x1, x2 = xf[..., :half], xf[..., half:]
    return jnp.concatenate([x1 * cos - x2 * sin, x2 * cos + x1 * sin], -1).astype(x.dtype)


def chunk_mask(qpos, kpos, n_prev):
    qc = (qpos // CHUNK)[:, None]
    kc = (kpos // CHUNK)[None, :]
    m = (kc <= qc) & (kpos[None, :] >= 0)
    if n_prev is not None:
        m = m & (kc >= qc - n_prev)
    return m


def rel_bias(table, qpos, kpos):
    rel = jnp.clip(qpos[:, None] - kpos[None, :], -REL_CLIP, REL_CLIP) + REL_CLIP
    return table[:, rel].astype(jnp.float32)


def attend(q, k, v, bias, mask, scale):
    s = jnp.einsum('bqhd,bkhd->bhqk', q, k).astype(jnp.float32) * scale
    if bias is not None:
        s = s + bias[None]
    s = jnp.where(mask[None, None], s, NEG_INF)
    p = jax.nn.softmax(s, axis=-1).astype(v.dtype)
    return jnp.einsum('bhqk,bkhd->bqhd', p, v)


def a_project(h, w_in, g_q, g_k):
    B, L, _ = h.shape
    q, k, v, z = jnp.split(h @ w_in, 4, axis=-1)
    q = rms_norm(q.reshape(B, L, A_HEADS, A_HEAD_DIM), g_q)
    k = rms_norm(k.reshape(B, L, A_HEADS, A_HEAD_DIM), g_k)
    v = v.reshape(B, L, A_HEADS, A_HEAD_DIM)
    return q, k, v, z


def a_prompt(h, w_in, g_q, g_k, table, w_out):
    B, S, _ = h.shape
    q, k, v, z = a_project(h, w_in, g_q, g_k)
    n_chunks = S // CHUNK
    pad = A_WINDOW
    band = A_WINDOW + CHUNK
    kp = jnp.pad(k, ((0, 0), (pad, 0), (0, 0), (0, 0)))
    vp = jnp.pad(v, ((0, 0), (pad, 0), (0, 0), (0, 0)))
    qc = q.reshape(B, n_chunks, CHUNK, A_HEADS, A_HEAD_DIM).transpose(1, 0, 2, 3, 4)
    q_off = jnp.arange(CHUNK, dtype=jnp.int32)
    k_off = jnp.arange(band, dtype=jnp.int32)
    scale = A_HEAD_DIM ** -0.5

    def one_chunk(args):
        ci, qi = args
        start = ci * CHUNK
        kb = lax.dynamic_slice_in_dim(kp, start, band, axis=1)
        vb = lax.dynamic_slice_in_dim(vp, start, band, axis=1)
        qpos = start + q_off
        kpos = start - pad + k_off
        return attend(qi, kb, vb, rel_bias(table, qpos, kpos),
                      chunk_mask(qpos, kpos, A_BAND_CHUNKS), scale)

    o = lax.map(one_chunk, (jnp.arange(n_chunks, dtype=jnp.int32), qc))
    o = o.transpose(1, 0, 2, 3, 4).reshape(B, S, A_HEADS * A_HEAD_DIM)
    y = (o * jax.nn.silu(z)) @ w_out
    rows = min(A_WINDOW, S)
    return y, k[:, S - rows:], v[:, S - rows:]


def a_sample(h, cache_k, cache_v, w_in, g_q, g_k, table, w_out):
    B, T, _ = h.shape
    q, k, v, z = a_project(h, w_in, g_q, g_k)
    n_cache = cache_k.shape[1]
    kk = jnp.concatenate([cache_k, k], axis=1)
    vv = jnp.concatenate([cache_v, v], axis=1)
    qpos = PAST_LEN + jnp.arange(T, dtype=jnp.int32)
    kpos = jnp.concatenate([PAST_LEN - n_cache + jnp.arange(n_cache, dtype=jnp.int32), qpos])
    o = attend(q, kk, vv, rel_bias(table, qpos, kpos),
               chunk_mask(qpos, kpos, A_BAND_CHUNKS), A_HEAD_DIM ** -0.5)
    y = (o.reshape(B, T, A_HEADS * A_HEAD_DIM) * jax.nn.silu(z)) @ w_out
    return y, k, v


def mla_inputs(h, pos, w_in, g_cq, w_uq, g_ckv, g_qn, g_qr, g_kr):
    B, L, _ = h.shape
    c_q, c_kv, k_r, z = jnp.split(h @ w_in, [Q_LORA, Q_LORA + KV_LORA, Q_LORA + KV_LORA + B_ROPE], axis=-1)
    q = (rms_norm(c_q, g_cq) @ w_uq).reshape(B, L, B_HEADS, B_NOPE + B_ROPE)
    q = jnp.concatenate([rms_norm(q[..., :B_NOPE], g_qn),
                         rotary(rms_norm(q[..., B_NOPE:], g_qr), pos)], axis=-1)
    c_kv = rms_norm(c_kv, g_ckv)
    k_r = rotary(rms_norm(k_r, g_kr)[:, :, None, :], pos)[:, :, 0, :]
    return q, c_kv, k_r, z


def mla_keys(c_kv, k_r, w_ukv, g_kn):
    B, L, _ = c_kv.shape
    kv = (c_kv @ w_ukv).reshape(B, L, B_HEADS, B_NOPE + B_VDIM)
    k = jnp.concatenate([rms_norm(kv[..., :B_NOPE], g_kn),
                         jnp.broadcast_to(k_r[:, :, None, :], (B, L, B_HEADS, B_ROPE))], axis=-1)
    return k, kv[..., B_NOPE:]


def b_prompt(h, w_in, g_cq, w_uq, g_ckv, w_ukv, g_qn, g_qr, g_kn, g_kr, w_out):
    B, S, _ = h.shape
    pos = jnp.arange(S, dtype=jnp.int32)
    q, c_kv, k_r, z = mla_inputs(h, pos, w_in, g_cq, w_uq, g_ckv, g_qn, g_qr, g_kr)
    k, v = mla_keys(c_kv, k_r, w_ukv, g_kn)
    n_blocks = S // Q_BLOCK
    qb = q.reshape(B, n_blocks, Q_BLOCK, B_HEADS, B_NOPE + B_ROPE).transpose(1, 0, 2, 3, 4)
    q_off = jnp.arange(Q_BLOCK, dtype=jnp.int32)
    scale = (B_NOPE + B_ROPE) ** -0.5

    def one_block(args):
        bi, qi = args
        qpos = bi * Q_BLOCK + q_off
        return attend(qi, k, v, None, chunk_mask(qpos, pos, None), scale)

    o = lax.map(one_block, (jnp.arange(n_blocks, dtype=jnp.int32), qb))
    o = o.transpose(1, 0, 2, 3, 4).reshape(B, S, B_HEADS * B_VDIM)
    y = (o * jax.nn.silu(z)) @ w_out
    return y, c_kv, k_r


def b_sample(h, cache_ckv, cache_kr, w_in, g_cq, w_uq, g_ckv, w_ukv, g_qn, g_qr, g_kn, g_kr, w_out):
    B, T, _ = h.shape
    qpos = PAST_LEN + jnp.arange(T, dtype=jnp.int32)
    q, c_kv, k_r, z = mla_inputs(h, qpos, w_in, g_cq, w_uq, g_ckv, g_qn, g_qr, g_kr)
    k, v = mla_keys(jnp.concatenate([cache_ckv, c_kv], axis=1),
                    jnp.concatenate([cache_kr, k_r], axis=1), w_ukv, g_kn)
    kpos = jnp.arange(PAST_LEN + T, dtype=jnp.int32)
    o = attend(q, k, v, None, chunk_mask(qpos, kpos, None), (B_NOPE + B_ROPE) ** -0.5)
    y = (o.reshape(B, T, B_HEADS * B_VDIM) * jax.nn.silu(z)) @ w_out
    return y, c_kv, k_r


def setup_inputs(seed: int = 0) -> dict:
    key = jax.random.key(seed)
    ks = jax.random.split(key, 32)
    f32 = jnp.float32

    def nrm(k, shape, s):
        return jax.random.normal(k, shape, f32) * s

    def gain(k, shape):
        return 1.0 + 0.02 * jax.random.normal(k, shape, f32)

    a_cache = min(A_WINDOW, PAST_LEN)
    D = D_MODEL
    return {
        "x_prompt": nrm(ks[0], (BATCH, SEQ, D), 1.0),
        "x_sample": nrm(ks[1], (DEC_BATCH, DEC_SEQ, D), 1.0),
        "cache_a_k": nrm(ks[2], (N_LAYERS_A, DEC_BATCH, a_cache, A_HEADS, A_HEAD_DIM), 1.0),
        "cache_a_v": nrm(ks[3], (N_LAYERS_A, DEC_BATCH, a_cache, A_HEADS, A_HEAD_DIM), 1.0),
        "cache_mla_ckv": nrm(ks[4], (N_LAYERS_B, DEC_BATCH, PAST_LEN, KV_LORA), 1.0),
        "cache_mla_krope": nrm(ks[5], (N_LAYERS_B, DEC_BATCH, PAST_LEN, B_ROPE), 1.0),
        "c_prompt": nrm(ks[6], (BATCH, D), 1.0),
        "c_sample": nrm(ks[7], (DEC_BATCH, D), 1.0),
        "norm_g": gain(ks[8], (DEPTH, D)),
        "ada_w": nrm(ks[9], (DEPTH, D, 3 * D), 0.5 * D ** -0.5),
        "ada_b": nrm(ks[10], (DEPTH, 3 * D), 0.02),
        "a_w_in": nrm(ks[11], (N_LAYERS_A, D, 4 * D), D ** -0.5),
        "a_g_q": gain(ks[12], (N_LAYERS_A, A_HEAD_DIM)),
        "a_g_k": gain(ks[13], (N_LAYERS_A, A_HEAD_DIM)),
        "a_rel_bias": nrm(ks[14], (N_LAYERS_A, A_HEADS, 2 * REL_CLIP + 1), 0.5),
        "a_w_out": nrm(ks[15], (N_LAYERS_A, D, D), D ** -0.5),
        "b_w_in": nrm(ks[16], (N_LAYERS_B, D, B_IN), D ** -0.5),
        "b_g_cq": gain(ks[17], (N_LAYERS_B, Q_LORA)),
        "b_w_uq": nrm(ks[18], (N_LAYERS_B, Q_LORA, B_HEADS * (B_NOPE + B_ROPE)), Q_LORA ** -0.5),
        "b_g_ckv": gain(ks[19], (N_LAYERS_B, KV_LORA)),
        "b_w_ukv": nrm(ks[20], (N_LAYERS_B, KV_LORA, B_HEADS * (B_NOPE + B_VDIM)), KV_LORA ** -0.5),
        "b_g_qn": gain(ks[21], (N_LAYERS_B, B_NOPE)),
        "b_g_qr": gain(ks[22], (N_LAYERS_B, B_ROPE)),
        "b_g_kn": gain(ks[23], (N_LAYERS_B, B_NOPE)),
        "b_g_kr": gain(ks[24], (N_LAYERS_B, B_ROPE)),
        "b_w_out": nrm(ks[25], (N_LAYERS_B, B_HEADS * B_VDIM, D), (B_HEADS * B_VDIM) ** -0.5),
    }


def reference(x_prompt, x_sample, cache_a_k, cache_a_v, cache_mla_ckv, cache_mla_krope,
              c_prompt, c_sample, norm_g, ada_w, ada_b,
              a_w_in, a_g_q, a_g_k, a_rel_bias, a_w_out,
              b_w_in, b_g_cq, b_w_uq, b_g_ckv, b_w_ukv, b_g_qn, b_g_qr, b_g_kn, b_g_kr, b_w_out):
    y_p, y_s = x_prompt, x_sample
    akp, avp, aks, avs = [], [], [], []
    bcp, brp, bcs, brs = [], [], [], []
    for i in range(DEPTH):
        j = i // N_MIXERS
        h_p, gate_p = ada_norm(y_p, c_prompt, norm_g[i], ada_w[i], ada_b[i])
        h_s, gate_s = ada_norm(y_s, c_sample, norm_g[i], ada_w[i], ada_b[i])
        if i % N_MIXERS == 0:
            out_p, k_new_p, v_new_p = a_prompt(h_p, a_w_in[j], a_g_q[j], a_g_k[j], a_rel_bias[j], a_w_out[j])
            out_s, k_new_s, v_new_s = a_sample(h_s, cache_a_k[j], cache_a_v[j], a_w_in[j], a_g_q[j],
                                               a_g_k[j], a_rel_bias[j], a_w_out[j])
            akp.append(k_new_p); avp.append(v_new_p); aks.append(k_new_s); avs.append(v_new_s)
        else:
            out_p, ckv_p, kr_p = b_prompt(h_p, b_w_in[j], b_g_cq[j], b_w_uq[j], b_g_ckv[j], b_w_ukv[j],
                                          b_g_qn[j], b_g_qr[j], b_g_kn[j], b_g_kr[j], b_w_out[j])
            out_s, ckv_s, kr_s = b_sample(h_s, cache_mla_ckv[j], cache_mla_krope[j], b_w_in[j], b_g_cq[j],
                                          b_w_uq[j], b_g_ckv[j], b_w_ukv[j], b_g_qn[j], b_g_qr[j],
                                          b_g_kn[j], b_g_kr[j], b_w_out[j])
            bcp.append(ckv_p); brp.append(kr_p); bcs.append(ckv_s); brs.append(kr_s)
        y_p = y_p + gate_p * out_p
        y_s = y_s + gate_s * out_s
    return (y_p, y_s,
            jnp.stack(akp), jnp.stack(avp), jnp.stack(aks), jnp.stack(avs),
            jnp.stack(bcp), jnp.stack(brp), jnp.stack(bcs), jnp.stack(brs))
```

```python
import functools

import jax
import jax.numpy as jnp
from jax import lax
from jax.experimental import pallas as pl
from jax.experimental.pallas import tpu as pltpu

f32 = jnp.float32
bf16 = jnp.bfloat16

EPS = 1e-6
NEG_INF = -1e30
CHUNK = 64
BAND_CHUNKS = 8
WINDOW = BAND_CHUNKS * CHUNK
REL_CLIP = 128
ROPE_THETA = 10000.0

N_HEADS = 16
A_HEAD_DIM = 64
B_NOPE = 64
B_ROPE = 32
B_VDIM = 64
B_QK = B_NOPE + B_ROPE
KV_LORA = 256
Q_LORA = 384

Q_GROUP = 256
A_KEYS = WINDOW + Q_GROUP
A_BIAS_ROWS = A_KEYS + WINDOW
SAMPLE_PAD = 128
VMEM_LIMIT = 48 * 1024 * 1024


def _silu(x):
    return x * (1.0 / (1.0 + jnp.exp(-x)))


def _nt(a, b):
    return lax.dot_general(a, b, (((1,), (1,)), ((), ())), preferred_element_type=f32)


def _dot(a, b):
    return jnp.dot(a, b, preferred_element_type=f32)


def _rms_rows(blk, gain_col, n):
    ms = jnp.sum(blk * blk, axis=0, keepdims=True) * (1.0 / n)
    return blk * lax.rsqrt(ms + EPS) * gain_col


def _ada_hidden(x_ref, mod_ref, g_ref):
    x = x_ref[...]
    ms = jnp.mean(x * x, axis=-1, keepdims=True)
    xn = x * lax.rsqrt(ms + EPS) * g_ref[...]
    return (xn * (1.0 + mod_ref[1:2, :]) + mod_ref[0:1, :]).astype(bf16)


def _softmax_cols(s):
    m = jnp.max(s, axis=0, keepdims=True)
    p = jnp.exp(s - m)
    return p, jnp.sum(p, axis=0, keepdims=True)


def _params(*sem):
    return pltpu.CompilerParams(dimension_semantics=sem, vmem_limit_bytes=VMEM_LIMIT)


def _const(shape):
    nd = len(shape)
    return pl.BlockSpec(shape, lambda *_: (0,) * nd)


def _mod_body(c_ref, w_ref, b_ref, o_ref):
    o_ref[...] = jnp.dot(_silu(c_ref[...]), w_ref[...], preferred_element_type=f32,
                         precision=lax.Precision.HIGHEST) + b_ref[...]


def _modulation(c_all, ada_w, ada_b):
    depth, d, n3 = ada_w.shape
    rows = c_all.shape[0]
    tn = d
    return pl.pallas_call(
        _mod_body,
        grid=(depth, n3 // tn),
        in_specs=[pl.BlockSpec((rows, d), lambda l, j: (0, 0)),
                  pl.BlockSpec((None, d, tn), lambda l, j: (l, 0, j)),
                  pl.BlockSpec((None, 1, tn), lambda l, j: (l, 0, j))],
        out_specs=pl.BlockSpec((None, rows, tn), lambda l, j: (l, 0, j)),
        out_shape=jax.ShapeDtypeStruct((depth, rows, n3), f32),
        compiler_params=_params("arbitrary", "arbitrary"),
        name="modulation",
    )(c_all, ada_w, ada_b.reshape(depth, 1, n3))


def _a_proj_body(n_skip, x_ref, mod_ref, g_ref, wq_ref, wk_ref, wv_ref, wz_ref, gq_ref, gk_ref,
                 qT_ref, k_ref, vT_ref, zs_ref, k32_ref, v32_ref):
    is_cache_tile = pl.program_id(1) >= n_skip
    hb = _ada_hidden(x_ref, mod_ref, g_ref)
    qT = _nt(wq_ref[...], hb)
    kT = _nt(wk_ref[...], hb)
    vT = _nt(wv_ref[...], hb)
    vT_ref[...] = vT.astype(bf16)
    gq = gq_ref[...]
    gk = gk_ref[...]
    hd = A_HEAD_DIM
    for p in range(N_HEADS // 2):
        rows = slice(2 * hd * p, 2 * hd * (p + 1))
        q_pair, k_pair = [], []
        for h in (2 * p, 2 * p + 1):
            q_pair.append(_rms_rows(qT[hd * h:hd * (h + 1)], gq, hd))
            k_pair.append(_rms_rows(kT[hd * h:hd * (h + 1)], gk, hd))
        qT_ref[rows, :] = jnp.concatenate(q_pair, axis=0).astype(bf16)
        k_nat = jnp.concatenate(k_pair, axis=0).T
        k_ref[:, rows] = k_nat.astype(bf16)

        @pl.when(is_cache_tile)
        def _(k_nat=k_nat, rows=rows):
            k32_ref[:, rows] = k_nat
            v32_ref[:, rows] = vT[rows].T
    zs_ref[...] = _silu(_dot(hb, wz_ref[...])).astype(bf16)


def _a_project(x, mod, g, wq_t, wk_t, wv_t, wz, gq, gk, tm, cache_rows):
    b, l, d = x.shape
    n_skip = (l - cache_rows) // tm
    tile = pl.BlockSpec((None, tm, d), lambda bi, i: (bi, i, 0))
    tile_t = pl.BlockSpec((None, d, tm), lambda bi, i: (bi, 0, i))
    cache = pl.BlockSpec((None, tm, d), lambda bi, i: (bi, jnp.maximum(i - n_skip, 0), 0))
    return pl.pallas_call(
        functools.partial(_a_proj_body, n_skip),
        grid=(b, l // tm),
        in_specs=[tile, pl.BlockSpec((None, 3, d), lambda bi, i: (bi, 0, 0)), _const((1, d)),
                  _const((d, d)), _const((d, d)), _const((d, d)), _const((d, d)),
                  _const((A_HEAD_DIM, 1)), _const((A_HEAD_DIM, 1))],
        out_specs=[tile_t, tile, tile_t, tile, cache, cache],
        out_shape=[jax.ShapeDtypeStruct((b, d, l), bf16), jax.ShapeDtypeStruct((b, l, d), bf16),
                   jax.ShapeDtypeStruct((b, d, l), bf16), jax.ShapeDtypeStruct((b, l, d), bf16),
                   jax.ShapeDtypeStruct((b, cache_rows, d), f32),
                   jax.ShapeDtypeStruct((b, cache_rows, d), f32)],
        compiler_params=_params("arbitrary", "arbitrary"),
        name="a_project",
    )(x, mod, g, wq_t, wk_t, wv_t, wz, gq, gk)


def _a_attn_body(seq, k_ref, qT_ref, vT_ref, e_ref, o_ref):
    hd = A_HEAD_DIM
    zeros = jnp.zeros((hd, Q_GROUP), bf16)

    def group(g, carry):
        q0 = pl.multiple_of(g * Q_GROUP, Q_GROUP)
        w0 = pl.multiple_of(jnp.maximum(g * Q_GROUP - WINDOW, 0), Q_GROUP)
        e0 = pl.multiple_of(WINDOW - (g * Q_GROUP - w0), Q_GROUP)
        k_win = k_ref[pl.ds(w0, A_KEYS), :]
        outs = []
        for hh in range(2):
            qh = qT_ref[hd * hh:hd * (hh + 1), pl.ds(q0, Q_GROUP)]
            w = jnp.concatenate([qh, zeros] if hh == 0 else [zeros, qh], axis=0)
            s = _dot(k_win, w) + e_ref[hh, pl.ds(e0, A_KEYS), :]
            p, l = _softmax_cols(s)
            o_t = _dot(vT_ref[hd * hh:hd * (hh + 1), pl.ds(w0, A_KEYS)], p.astype(bf16))
            outs.append(o_t * (1.0 / l))
        o_ref[pl.ds(q0, Q_GROUP), :] = jnp.concatenate(outs, axis=0).T.astype(bf16)
        return carry

    lax.fori_loop(0, seq // Q_GROUP, group, 0)


def _a_attention(k_nat, q_t, v_t, bias_ext):
    b, l, d = k_nat.shape
    pair = 2 * A_HEAD_DIM
    return pl.pallas_call(
        functools.partial(_a_attn_body, l),
        grid=(d // pair, b),
        in_specs=[pl.BlockSpec((None, l, pair), lambda p, bi: (bi, 0, p)),
                  pl.BlockSpec((None, pair, l), lambda p, bi: (bi, p, 0)),
                  pl.BlockSpec((None, pair, l), lambda p, bi: (bi, p, 0)),
                  pl.BlockSpec((2, A_BIAS_ROWS, Q_GROUP), lambda p, bi: (p, 0, 0))],
        out_specs=pl.BlockSpec((None, l, pair), lambda p, bi: (bi, 0, p)),
        out_shape=jax.ShapeDtypeStruct((b, l, d), bf16),
        compiler_params=_params("arbitrary", "arbitrary"),
        name="a_attention",
    )(k_nat, q_t, v_t, bias_ext)


def _a_sattn_body(kc_ref, kn_ref, vc_ref, vn_ref, qT_ref, e_ref, o_ref):
    hd = A_HEAD_DIM
    zeros = jnp.zeros((hd, SAMPLE_PAD), bf16)
    k_all = jnp.concatenate([kc_ref[...].astype(bf16), kn_ref[...]], axis=0)
    v_t = jnp.concatenate([vc_ref[...], vn_ref[...]], axis=0).T.astype(bf16)
    outs = []
    for hh in range(2):
        qh = qT_ref[hd * hh:hd * (hh + 1), :]
        w = jnp.concatenate([qh, zeros] if hh == 0 else [zeros, qh], axis=0)
        p, l = _softmax_cols(_dot(k_all, w) + e_ref[hh])
        outs.append(_dot(v_t[hd * hh:hd * (hh + 1)], p.astype(bf16)) * (1.0 / l))
    o_ref[...] = jnp.concatenate(outs, axis=0).T.astype(bf16)


def _a_sample_attention(k_cache, k_new, v_cache, v_new, q_t, bias_s):
    b, n_cache, d = k_cache.shape
    pair = 2 * A_HEAD_DIM
    n_keys = n_cache + SAMPLE_PAD
    new = pl.BlockSpec((None, SAMPLE_PAD, pair), lambda bi, p: (bi, 0, p))
    old = pl.BlockSpec((None, n_cache, pair), lambda bi, p: (bi, 0, p))
    return pl.pallas_call(
        _a_sattn_body,
        grid=(b, d // pair),
        in_specs=[old, new, old, new,
                  pl.BlockSpec((None, pair, SAMPLE_PAD), lambda bi, p: (bi, p, 0)),
                  pl.BlockSpec((2, n_keys, SAMPLE_PAD), lambda bi, p: (p, 0, 0))],
        out_specs=new,
        out_shape=jax.ShapeDtypeStruct((b, SAMPLE_PAD, d), bf16),
        compiler_params=_params("arbitrary", "arbitrary"),
        name="a_sample_attention",
    )(k_cache, k_new, v_cache, v_new, q_t, bias_s)


def _out_body(o_ref, zs_ref, w_ref, x_ref, mod_ref, y_ref):
    y = _dot(o_ref[...] * zs_ref[...], w_ref[...])
    y_ref[...] = x_ref[...] + mod_ref[2:3, :] * y


def _out_project(o, zs, w_out, x, mod, tm):
    b, l, d = x.shape
    tile = pl.BlockSpec((None, tm, d), lambda bi, i: (bi, i, 0))
    return pl.pallas_call(
        _out_body,
        grid=(b, l // tm),
        in_specs=[tile, tile, _const(w_out.shape), tile,
                  pl.BlockSpec((None, 3, d), lambda bi, i: (bi, 0, 0))],
        out_specs=tile,
        out_shape=jax.ShapeDtypeStruct((b, l, d), f32),
        compiler_params=_params("arbitrary", "arbitrary"),
        name="out_project",
    )(o, zs, w_out, x, mod)


def _rotate(x, cos, sin):
    half = B_ROPE // 2
    x1, x2 = x[:half], x[half:]
    return jnp.concatenate([x1 * cos - x2 * sin, x2 * cos + x1 * sin], axis=0)


def _b_proj_body(x_ref, mod_ref, g_ref, wcq_ref, wckv_ref, wkr_ref, wz_ref, wuq_ref,
                 gcq_ref, gckv_ref, gkr_ref, gqn_ref, gqr_ref, cos_ref, sin_ref,
                 qT_ref, ckv_ref, kr_ref, zs_ref):
    hb = _ada_hidden(x_ref, mod_ref, g_ref)
    tm = hb.shape[0]
    cos = cos_ref[...]
    sin = sin_ref[...]
    cq = _rms_rows(_nt(wcq_ref[...], hb), gcq_ref[...], Q_LORA).astype(bf16)
    qT = _dot(wuq_ref[...], cq)
    gqn = gqn_ref[...]
    gqr = gqr_ref[...]
    for h in range(N_HEADS):
        base = B_QK * h
        nope = _rms_rows(qT[base:base + B_NOPE], gqn, B_NOPE)
        rope = _rotate(_rms_rows(qT[base + B_NOPE:base + B_QK], gqr, B_ROPE), cos, sin)
        qT_ref[base:base + B_QK, :] = jnp.concatenate([nope, rope], axis=0).astype(bf16)
    ckv = _rms_rows(_nt(wckv_ref[...], hb), gckv_ref[...], KV_LORA)
    ckv_ref[...] = ckv.T
    kr = _rotate(_rms_rows(_nt(wkr_ref[...], hb), gkr_ref[...], B_ROPE), cos, sin)
    kr_pad = jnp.concatenate([kr, jnp.zeros((128 - B_ROPE, tm), f32)], axis=0)
    kr_ref[...] = kr_pad.T[:, :B_ROPE]
    zs_ref[...] = _silu(_dot(hb, wz_ref[...])).astype(bf16)


def _b_project(x, mod, g, w, cos_t, sin_t, tm):
    b, l, d = x.shape
    tile = pl.BlockSpec((None, tm, d), lambda bi, i: (bi, i, 0))
    rot = pl.BlockSpec((B_ROPE // 2, tm), lambda bi, i: (0, i))
    nq = N_HEADS * B_QK
    return pl.pallas_call(
        _b_proj_body,
        grid=(b, l // tm),
        in_specs=[tile, pl.BlockSpec((None, 3, d), lambda bi, i: (bi, 0, 0)), _const((1, d)),
                  _const((Q_LORA, d)), _const((KV_LORA, d)), _const((B_ROPE, d)), _const((d, d)),
                  _const((nq, Q_LORA)),
                  _const((Q_LORA, 1)), _const((KV_LORA, 1)), _const((B_ROPE, 1)),
                  _const((B_NOPE, 1)), _const((B_ROPE, 1)), rot, rot],
        out_specs=[pl.BlockSpec((None, nq, tm), lambda bi, i: (bi, 0, i)),
                   pl.BlockSpec((None, tm, KV_LORA), lambda bi, i: (bi, i, 0)),
                   pl.BlockSpec((None, tm, B_ROPE), lambda bi, i: (bi, i, 0)),
                   tile],
        out_shape=[jax.ShapeDtypeStruct((b, nq, l), bf16),
                   jax.ShapeDtypeStruct((b, l, KV_LORA), f32),
                   jax.ShapeDtypeStruct((b, l, B_ROPE), f32),
                   jax.ShapeDtypeStruct((b, l, d), bf16)],
        compiler_params=_params("arbitrary", "arbitrary"),
        name="b_project",
    )(x, mod, g, w["cq_t"], w["ckv_t"], w["kr_t"], w["z"], w["uq_t"],
      w["g_cq"], w["g_ckv"], w["g_kr"], w["g_qn"], w["g_qr"], cos_t, sin_t)


def _b_kvup_body(ckv_ref, kr_ref, wukv_ref, gkn_ref, k_ref, vT_ref):
    tm = ckv_ref.shape[0]
    kvT = _nt(wukv_ref[...], ckv_ref[...].astype(bf16))
    eye = (lax.broadcasted_iota(jnp.int32, (B_ROPE, B_ROPE), 0)
           == lax.broadcasted_iota(jnp.int32, (B_ROPE, B_ROPE), 1)).astype(bf16)
    krT = _nt(eye, kr_ref[...].astype(bf16))
    tail = jnp.concatenate([krT, jnp.zeros((128 - B_QK, tm), f32)], axis=0)
    gkn = gkn_ref[...]
    per_head = B_NOPE + B_VDIM
    for h in range(N_HEADS):
        base = per_head * h
        kn = _rms_rows(kvT[base:base + B_NOPE], gkn, B_NOPE)
        k_ref[h] = jnp.concatenate([kn, tail], axis=0).T.astype(bf16)
        vT_ref[B_VDIM * h:B_VDIM * (h + 1), :] = kvT[base + B_NOPE:base + per_head].astype(bf16)


def _b_kv_up(ckv, kr, wukv_t, g_kn, tm):
    b, l, _ = ckv.shape
    return pl.pallas_call(
        _b_kvup_body,
        grid=(b, l // tm),
        in_specs=[pl.BlockSpec((None, tm, KV_LORA), lambda bi, i: (bi, i, 0)),
                  pl.BlockSpec((None, tm, B_ROPE), lambda bi, i: (bi, i, 0)),
                  _const(wukv_t.shape), _const((B_NOPE, 1))],
        out_specs=[pl.BlockSpec((None, N_HEADS, tm, 128), lambda bi, i: (bi, 0, i, 0)),
                   pl.BlockSpec((None, N_HEADS * B_VDIM, tm), lambda bi, i: (bi, 0, i))],
        out_shape=[jax.ShapeDtypeStruct((b, N_HEADS, l, 128), bf16),
                   jax.ShapeDtypeStruct((b, N_HEADS * B_VDIM, l), bf16)],
        compiler_params=_params("arbitrary", "arbitrary"),
        name="b_kv_up",
    )(ckv, kr, wukv_t, g_kn)


def _b_attn_body(seq, k_ref, qT_ref, vT_ref, o_ref):
    t = Q_GROUP
    row_chunk = lax.broadcasted_iota(jnp.int32, (t, t), 0) // CHUNK
    col_chunk = lax.broadcasted_iota(jnp.int32, (t, t), 1) // CHUNK
    diag_mask = jnp.where(row_chunk <= col_chunk, 0.0, NEG_INF).astype(f32)
    zpad = jnp.zeros((128 - B_QK, t), bf16)

    def q_tile(qi, carry):
        q0 = pl.multiple_of(qi * t, t)
        ws = [jnp.concatenate([qT_ref[B_QK * hh:B_QK * (hh + 1), pl.ds(q0, t)], zpad], axis=0)
              for hh in range(2)]
        state = []
        for hh in range(2):
            s = _dot(k_ref[hh, pl.ds(q0, t), :], ws[hh]) + diag_mask
            m = jnp.max(s, axis=0, keepdims=True)
            p = jnp.exp(s - m)
            acc = _dot(vT_ref[B_VDIM * hh:B_VDIM * (hh + 1), pl.ds(q0, t)], p.astype(bf16))
            state += [m, jnp.sum(p, axis=0, keepdims=True), acc]

        def kv_tile(j, st):
            k0 = pl.multiple_of(j * t, t)
            new = []
            for hh in range(2):
                m, l, acc = st[3 * hh:3 * hh + 3]
                s = _dot(k_ref[hh, pl.ds(k0, t), :], ws[hh])
                m2 = jnp.maximum(m, jnp.max(s, axis=0, keepdims=True))
                a = jnp.exp(m - m2)
                p = jnp.exp(s - m2)
                pv = _dot(vT_ref[B_VDIM * hh:B_VDIM * (hh + 1), pl.ds(k0, t)], p.astype(bf16))
                new += [m2, a * l + jnp.sum(p, axis=0, keepdims=True), a * acc + pv]
            return tuple(new)

        st = lax.fori_loop(0, qi, kv_tile, tuple(state))
        outs = [st[2] * (1.0 / st[1]), st[5] * (1.0 / st[4])]
        o_ref[pl.ds(q0, t), :] = jnp.concatenate(outs, axis=0).T.astype(bf16)
        return carry

    lax.fori_loop(0, seq // t, q_tile, 0)


def _b_attention(k_nat, q_t, v_t):
    b, _, l, _ = k_nat.shape
    return pl.pallas_call(
        functools.partial(_b_attn_body, l),
        grid=(b, N_HEADS // 2),
        in_specs=[pl.BlockSpec((None, 2, l, 128), lambda bi, p: (bi, p, 0, 0)),
                  pl.BlockSpec((None, 2 * B_QK, l), lambda bi, p: (bi, p, 0)),
                  pl.BlockSpec((None, 2 * B_VDIM, l), lambda bi, p: (bi, p, 0))],
        out_specs=pl.BlockSpec((None, l, 2 * B_VDIM), lambda bi, p: (bi, 0, p)),
        out_shape=jax.ShapeDtypeStruct((b, l, N_HEADS * B_VDIM), bf16),
        compiler_params=_params("arbitrary", "arbitrary"),
        name="b_attention",
    )(k_nat, q_t, v_t)


def _b_sattn_body(n_new, kc_ref, kn_ref, vc_ref, vn_ref, qT_ref, o_ref):
    t = SAMPLE_PAD
    zpad = jnp.zeros((128 - B_QK, t), bf16)
    new_mask = jnp.where(lax.broadcasted_iota(jnp.int32, (t, t), 0) < n_new, 0.0, NEG_INF).astype(f32)
    outs = []
    for hh in range(2):
        w = jnp.concatenate([qT_ref[B_QK * hh:B_QK * (hh + 1), :], zpad], axis=0)
        s_c = _dot(kc_ref[hh], w)
        s_n = _dot(kn_ref[hh], w) + new_mask
        m = jnp.maximum(jnp.max(s_c, axis=0, keepdims=True), jnp.max(s_n, axis=0, keepdims=True))
        p_c = jnp.exp(s_c - m)
        p_n = jnp.exp(s_n - m)
        l = jnp.sum(p_c, axis=0, keepdims=True) + jnp.sum(p_n, axis=0, keepdims=True)
        rows = slice(B_VDIM * hh, B_VDIM * (hh + 1))
        o_t = _dot(vc_ref[rows, :], p_c.astype(bf16)) + _dot(vn_ref[rows, :], p_n.astype(bf16))
        outs.append(o_t * (1.0 / l))
    o_ref[...] = jnp.concatenate(outs, axis=0).T.astype(bf16)


def _b_sample_attention(k_cache, k_new, v_cache, v_new, q_t, n_new):
    b, _, past, _ = k_cache.shape
    t = SAMPLE_PAD
    return pl.pallas_call(
        functools.partial(_b_sattn_body, n_new),
        grid=(b, N_HEADS // 2),
        in_specs=[pl.BlockSpec((None, 2, past, 128), lambda bi, p: (bi, p, 0, 0)),
                  pl.BlockSpec((None, 2, t, 128), lambda bi, p: (bi, p, 0, 0)),
                  pl.BlockSpec((None, 2 * B_VDIM, past), lambda bi, p: (bi, p, 0)),
                  pl.BlockSpec((None, 2 * B_VDIM, t), lambda bi, p: (bi, p, 0)),
                  pl.BlockSpec((None, 2 * B_QK, t), lambda bi, p: (bi, p, 0))],
        out_specs=pl.BlockSpec((None, t, 2 * B_VDIM), lambda bi, p: (bi, 0, p)),
        out_shape=jax.ShapeDtypeStruct((b, t, N_HEADS * B_VDIM), bf16),
        compiler_params=_params("arbitrary", "arbitrary"),
        name="b_sample_attention",
    )(k_cache, k_new, v_cache, v_new, q_t)


def _extended_bias(table):
    r = jnp.arange(A_BIAS_ROWS, dtype=jnp.int32)[:, None]
    i = jnp.arange(Q_GROUP, dtype=jnp.int32)[None, :]
    rel = jnp.clip(WINDOW + i - r, -REL_CLIP, REL_CLIP) + REL_CLIP
    back = BAND_CHUNKS + i // CHUNK - r // CHUNK
    visible = (back >= 0) & (back <= BAND_CHUNKS)
    return jnp.where(visible[None], table[:, rel].astype(f32), NEG_INF)


def _rope_tables(pos):
    half = B_ROPE // 2
    inv = ROPE_THETA ** (-jnp.arange(half, dtype=f32) / half)
    ang = pos.astype(f32)[:, None] * inv[None, :]
    return jnp.cos(ang).T, jnp.sin(ang).T


def _col(g, scale=1.0):
    return (g.astype(f32) * scale)[:, None]


def kernel(x_prompt, x_sample, cache_a_k, cache_a_v, cache_mla_ckv, cache_mla_krope, c_prompt, c_sample, norm_g, ada_w, ada_b, a_w_in, a_g_q, a_g_k, a_rel_bias, a_w_out, b_w_in, b_g_cq, b_w_uq, b_g_ckv, b_w_ukv, b_g_qn, b_g_qr, b_g_kn, b_g_kr, b_w_out):
    bp, seq, d = x_prompt.shape
    bs, dec, _ = x_sample.shape
    past = cache_mla_ckv.shape[2]
    n_cache_a = cache_a_k.shape[2]
    assert d == N_HEADS * A_HEAD_DIM and seq % Q_GROUP == 0 and seq >= A_KEYS
    assert past % CHUNK == 0 and dec <= CHUNK and dec <= SAMPLE_PAD and n_cache_a == WINDOW
    cache_rows = min(WINDOW, seq)
    tm = 256

    mod = _modulation(jnp.concatenate([c_prompt, c_sample], axis=0), ada_w, ada_b)
    mod = mod.reshape(mod.shape[0], bp + bs, 3, d)
    xs_pad = jnp.pad(x_sample, ((0, 0), (0, SAMPLE_PAD - dec), (0, 0)))

    w_in = a_w_in[0]
    wq_t, wk_t, wv_t = (w_in[:, d * n:d * (n + 1)].T.astype(bf16) for n in range(3))
    wz = w_in[:, 3 * d:].astype(bf16)
    w_out = a_w_out[0].astype(bf16)
    gq = _col(a_g_q[0], A_HEAD_DIM ** -0.5)
    gk = _col(a_g_k[0])
    g0 = norm_g[0][None, :]
    bias_ext = _extended_bias(a_rel_bias[0])
    mod_p, mod_s = mod[0, :bp], mod[0, bp:]

    q_t, k_nat, v_t, zs, k32, v32 = _a_project(x_prompt, mod_p, g0, wq_t, wk_t, wv_t, wz, gq, gk, tm, cache_rows)
    o = _a_attention(k_nat, q_t, v_t, bias_ext)
    y_p = _out_project(o, zs, w_out, x_prompt, mod_p, tm)
    new_a_k_p = k32.reshape(1, bp, cache_rows, N_HEADS, A_HEAD_DIM)
    new_a_v_p = v32.reshape(1, bp, cache_rows, N_HEADS, A_HEAD_DIM)

    q_t, k_nat, v_t, zs, k32, v32 = _a_project(xs_pad, mod_s, g0, wq_t, wk_t, wv_t, wz, gq, gk,
                                               SAMPLE_PAD, SAMPLE_PAD)
    key_row = jnp.arange(n_cache_a + SAMPLE_PAD)[None, :, None]
    bias_s = jnp.where(key_row < n_cache_a + dec, bias_ext[:, :n_cache_a + SAMPLE_PAD, :SAMPLE_PAD], NEG_INF)
    o = _a_sample_attention(cache_a_k[0].reshape(bs, n_cache_a, d), k_nat,
                            cache_a_v[0].reshape(bs, n_cache_a, d), v32, q_t, bias_s)
    ys_pad = _out_project(o, zs, w_out, xs_pad, mod_s, SAMPLE_PAD)
    new_a_k_s = k32[:, :dec].reshape(1, bs, dec, N_HEADS, A_HEAD_DIM)
    new_a_v_s = v32[:, :dec].reshape(1, bs, dec, N_HEADS, A_HEAD_DIM)

    w_in = b_w_in[0]
    scale = B_QK ** -0.5
    wb = {
        "cq_t": w_in[:, :Q_LORA].T.astype(bf16),
        "ckv_t": w_in[:, Q_LORA:Q_LORA + KV_LORA].T.astype(bf16),
        "kr_t": w_in[:, Q_LORA + KV_LORA:Q_LORA + KV_LORA + B_ROPE].T.astype(bf16),
        "z": w_in[:, Q_LORA + KV_LORA + B_ROPE:].astype(bf16),
        "uq_t": b_w_uq[0].T.astype(bf16),
        "g_cq": _col(b_g_cq[0]), "g_ckv": _col(b_g_ckv[0]), "g_kr": _col(b_g_kr[0]),
        "g_qn": _col(b_g_qn[0], scale), "g_qr": _col(b_g_qr[0], scale),
    }
    wukv_t = b_w_ukv[0].T.astype(bf16)
    g_kn = _col(b_g_kn[0])
    w_out = b_w_out[0].astype(bf16)
    g1 = norm_g[1][None, :]
    mod_p, mod_s = mod[1, :bp], mod[1, bp:]

    cos_t, sin_t = _rope_tables(jnp.arange(seq, dtype=jnp.int32))
    q_t, ckv_p, kr_p, zs = _b_project(y_p, mod_p, g1, wb, cos_t, sin_t, tm)
    k_nat, v_t = _b_kv_up(ckv_p, kr_p, wukv_t, g_kn, tm)
    o = _b_attention(k_nat, q_t, v_t)
    y_p = _out_project(o, zs, w_out, y_p, mod_p, tm)

    cos_t, sin_t = _rope_tables(past + jnp.arange(SAMPLE_PAD, dtype=jnp.int32))
    q_t, ckv_s, kr_s, zs = _b_project(ys_pad, mod_s, g1, wb, cos_t, sin_t, SAMPLE_PAD)
    k_new, v_new = _b_kv_up(ckv_s, kr_s, wukv_t, g_kn, SAMPLE_PAD)
    k_old, v_old = _b_kv_up(cache_mla_ckv[0], cache_mla_krope[0], wukv_t, g_kn, tm)
    o = _b_sample_attention(k_old, k_new, v_old, v_new, q_t, dec)
    ys_pad = _out_project(o, zs, w_out, ys_pad, mod_s, SAMPLE_PAD)

    return (y_p, ys_pad[:, :dec], new_a_k_p, new_a_v_p, new_a_k_s, new_a_v_s,
            ckv_p[None], kr_p[None], ckv_s[None, :, :dec], kr_s[None, :, :dec])
```

```python
import functools

import numpy as np

import jax
import jax.numpy as jnp
from jax import lax
from jax.experimental import pallas as pl
from jax.experimental.pallas import tpu as pltpu

f32 = jnp.float32
bf16 = jnp.bfloat16

EPS = 1e-6
NEG_INF = -1e30
CHUNK = 64
BAND_CHUNKS = 8
WINDOW = BAND_CHUNKS * CHUNK
REL_CLIP = 128
ROPE_THETA = 10000.0
LOG2E = 1.4426950408889634

N_HEADS = 16
A_HEAD_DIM = 64
B_NOPE = 64
B_ROPE = 32
B_VDIM = 64
B_QK = B_NOPE + B_ROPE
KV_LORA = 256
Q_LORA = 384

Q_GROUP = 256
A_KEYS = WINDOW + Q_GROUP
A_BIAS_ROWS = A_KEYS + WINDOW
A_BIAS_SPAN = A_BIAS_ROWS + Q_GROUP
SAMPLE_PAD = 128
B_ITEMS_PER_STEP = 4
ONES_ROWS = 8
VMEM_LIMIT = 48 * 1024 * 1024


def _silu(x):
    return x * (1.0 / (1.0 + jnp.exp(-x)))


def _nt(a, b):
    return lax.dot_general(a, b, (((1,), (1,)), ((), ())), preferred_element_type=f32)


def _dot(a, b):
    return jnp.dot(a, b, preferred_element_type=f32)


def _rms_rows(blk, gain_col, n):
    ms = jnp.sum(blk * blk, axis=0, keepdims=True) * (1.0 / n)
    return blk * lax.rsqrt(ms + EPS) * gain_col


def _ada_hidden(x_ref, mod_ref, g_ref):
    x = x_ref[...]
    ms = jnp.mean(x * x, axis=-1, keepdims=True)
    xn = x * lax.rsqrt(ms + EPS) * g_ref[...]
    return (xn * (1.0 + mod_ref[1:2, :]) + mod_ref[0:1, :]).astype(bf16)


def _softmax_cols(s):
    m = jnp.max(s, axis=0, keepdims=True)
    p = jnp.exp2(s - m)
    return p, jnp.sum(p, axis=0, keepdims=True)


def _normalise(o_ext, rows):
    return o_ext[:rows] * (1.0 / o_ext[rows:rows + 1])


def _bias_tile(f_row, n_rows, n_cols):
    blocks = []
    for a in range(pl.cdiv(n_rows, Q_GROUP)):
        lo = A_BIAS_ROWS - Q_GROUP * (a + 1)
        x = jnp.broadcast_to(f_row[:, lo:lo + 2 * Q_GROUP], (Q_GROUP, 2 * Q_GROUP))
        y = pltpu.roll(x, Q_GROUP + 1, 1, stride=1, stride_axis=0)[:, :n_cols]
        r = lax.broadcasted_iota(jnp.int32, (Q_GROUP, n_cols), 0) + Q_GROUP * a
        i = lax.broadcasted_iota(jnp.int32, (Q_GROUP, n_cols), 1)
        back = BAND_CHUNKS + i // CHUNK - r // CHUNK
        blocks.append(jnp.where((back >= 0) & (back <= BAND_CHUNKS), y, NEG_INF))
    return jnp.concatenate(blocks, axis=0)[:n_rows]


def _params(*sem):
    return pltpu.CompilerParams(dimension_semantics=sem, vmem_limit_bytes=VMEM_LIMIT)


def _const(shape):
    nd = len(shape)
    return pl.BlockSpec(shape, lambda *_: (0,) * nd)


def _mod_body(c_ref, w_ref, b_ref, o_ref):
    o_ref[...] = jnp.dot(_silu(c_ref[...]), w_ref[...], preferred_element_type=f32,
                         precision=lax.Precision.HIGHEST) + b_ref[...]


def _modulation(c_all, ada_w, ada_b):
    depth, d, n3 = ada_w.shape
    rows = c_all.shape[0]
    tn = d
    return pl.pallas_call(
        _mod_body,
        grid=(depth, n3 // tn),
        in_specs=[pl.BlockSpec((rows, d), lambda l, j: (0, 0)),
                  pl.BlockSpec((None, d, tn), lambda l, j: (l, 0, j)),
                  pl.BlockSpec((None, 1, tn), lambda l, j: (l, 0, j))],
        out_specs=pl.BlockSpec((None, rows, tn), lambda l, j: (l, 0, j)),
        out_shape=jax.ShapeDtypeStruct((depth, rows, n3), f32),
        compiler_params=_params("arbitrary", "arbitrary"),
        name="modulation",
    )(c_all, ada_w, ada_b.reshape(depth, 1, n3))


def _a_proj_body(n_skip, x_ref, mod_ref, g_ref, wq_ref, wk_ref, wv_ref, wz_ref, gq_ref, gk_ref,
                 qT_ref, k_ref, vT_ref, zs_ref, k32_ref, v32_ref):
    is_cache_tile = pl.program_id(1) >= n_skip
    hb = _ada_hidden(x_ref, mod_ref, g_ref)
    qT = _nt(wq_ref[...], hb)
    kT = _nt(wk_ref[...], hb)
    vT = _nt(wv_ref[...], hb)
    vT_ref[...] = vT.astype(bf16)
    gq = gq_ref[...]
    gk = gk_ref[...]
    hd = A_HEAD_DIM
    for p in range(N_HEADS // 2):
        rows = slice(2 * hd * p, 2 * hd * (p + 1))
        q_pair, k_pair = [], []
        for h in (2 * p, 2 * p + 1):
            q_pair.append(_rms_rows(qT[hd * h:hd * (h + 1)], gq, hd))
            k_pair.append(_rms_rows(kT[hd * h:hd * (h + 1)], gk, hd))
        qT_ref[rows, :] = jnp.concatenate(q_pair, axis=0).astype(bf16)
        k_nat = jnp.concatenate(k_pair, axis=0).T
        k_ref[:, rows] = k_nat.astype(bf16)

        @pl.when(is_cache_tile)
        def _(k_nat=k_nat, rows=rows):
            k32_ref[:, rows] = k_nat
            v32_ref[:, rows] = vT[rows].T
    zs_ref[...] = _silu(_dot(hb, wz_ref[...])).astype(bf16)


def _a_project(x, mod, g, wq_t, wk_t, wv_t, wz, gq, gk, tm, cache_rows):
    b, l, d = x.shape
    n_skip = (l - cache_rows) // tm
    tile = pl.BlockSpec((None, tm, d), lambda bi, i: (bi, i, 0))
    tile_t = pl.BlockSpec((None, d, tm), lambda bi, i: (bi, 0, i))
    cache = pl.BlockSpec((None, tm, d), lambda bi, i: (bi, jnp.maximum(i - n_skip, 0), 0))
    return pl.pallas_call(
        functools.partial(_a_proj_body, n_skip),
        grid=(b, l // tm),
        in_specs=[tile, pl.BlockSpec((None, 3, d), lambda bi, i: (bi, 0, 0)), _const((1, d)),
                  _const((d, d)), _const((d, d)), _const((d, d)), _const((d, d)),
                  _const((A_HEAD_DIM, 1)), _const((A_HEAD_DIM, 1))],
        out_specs=[tile_t, tile, tile_t, tile, cache, cache],
        out_shape=[jax.ShapeDtypeStruct((b, d, l), bf16), jax.ShapeDtypeStruct((b, l, d), bf16),
                   jax.ShapeDtypeStruct((b, d, l), bf16), jax.ShapeDtypeStruct((b, l, d), bf16),
                   jax.ShapeDtypeStruct((b, cache_rows, d), f32),
                   jax.ShapeDtypeStruct((b, cache_rows, d), f32)],
        compiler_params=_params("arbitrary", "arbitrary"),
        name="a_project",
    )(x, mod, g, wq_t, wk_t, wv_t, wz, gq, gk)


def _a_attn_body(seq, f_ref, k_ref, qT_ref, vT_ref, o_ref, e_ref, s0, s1, m0, m1, o0, o1):
    hd = A_HEAD_DIM
    n_groups = seq // Q_GROUP
    s_buf, m_buf, o_buf = (s0, s1), (m0, m1), (o0, o1)

    @pl.when(pl.program_id(1) == 0)
    def _():
        for hh in range(2):
            e_ref[hh] = _bias_tile(f_ref[hh], A_BIAS_ROWS, Q_GROUP)

    zeros = jnp.zeros((hd, Q_GROUP), bf16)
    ones = jnp.ones((ONES_ROWS, A_KEYS), bf16)

    def offsets(g):
        q0 = pl.multiple_of(g * Q_GROUP, Q_GROUP)
        w0 = pl.multiple_of(jnp.maximum(g * Q_GROUP - WINDOW, 0), Q_GROUP)
        e0 = pl.multiple_of(WINDOW - (g * Q_GROUP - w0), Q_GROUP)
        return q0, w0, e0

    def scores(g, par):
        q0, w0, e0 = offsets(g)
        k_win = k_ref[pl.ds(w0, A_KEYS), :]
        for hh in range(2):
            qh = qT_ref[hd * hh:hd * (hh + 1), pl.ds(q0, Q_GROUP)]
            w = jnp.concatenate([qh, zeros] if hh == 0 else [zeros, qh], axis=0)
            s = _dot(k_win, w) + e_ref[hh, pl.ds(e0, A_KEYS), :]
            s_buf[par][hh] = s
            m_buf[par][hh] = jnp.max(s, axis=0, keepdims=True)

    def values(g, par):
        _, w0, _ = offsets(g)
        for hh in range(2):
            p = jnp.exp2(s_buf[par][hh] - m_buf[par][hh]).astype(bf16)
            v_ext = jnp.concatenate([vT_ref[hd * hh:hd * (hh + 1), pl.ds(w0, A_KEYS)], ones], axis=0)
            o_buf[par][hh] = _dot(v_ext, p)

    def finish(g, par):
        q0, _, _ = offsets(g)
        outs = [_normalise(o_buf[par][hh], hd) for hh in range(2)]
        o_ref[pl.ds(q0, Q_GROUP), :] = jnp.concatenate(outs, axis=0).T.astype(bf16)

    scores(0, 0)
    scores(1, 1)
    values(0, 0)

    def body(i, carry):
        g = 2 * i + 1
        scores(g + 1, 0)
        values(g, 1)
        finish(g - 1, 0)
        scores(g + 2, 1)
        values(g + 1, 0)
        finish(g, 1)
        return carry

    lax.fori_loop(0, (n_groups - 2) // 2, body, 0)
    values(n_groups - 1, 1)
    finish(n_groups - 2, 0)
    finish(n_groups - 1, 1)


def _a_attention(f_rows, k_nat, q_t, v_t):
    b, l, d = k_nat.shape
    pair = 2 * A_HEAD_DIM
    scratch = ([pltpu.VMEM((2, A_BIAS_ROWS, Q_GROUP), f32)]
               + [pltpu.VMEM((2, A_KEYS, Q_GROUP), f32)] * 2
               + [pltpu.VMEM((2, 1, Q_GROUP), f32)] * 2
               + [pltpu.VMEM((2, A_HEAD_DIM + ONES_ROWS, Q_GROUP), f32)] * 2)
    return pl.pallas_call(
        functools.partial(_a_attn_body, l),
        grid=(d // pair, b),
        in_specs=[pl.BlockSpec((2, 1, A_BIAS_SPAN), lambda p, bi: (p, 0, 0)),
                  pl.BlockSpec((None, l, pair), lambda p, bi: (bi, 0, p)),
                  pl.BlockSpec((None, pair, l), lambda p, bi: (bi, p, 0)),
                  pl.BlockSpec((None, pair, l), lambda p, bi: (bi, p, 0))],
        out_specs=pl.BlockSpec((None, l, pair), lambda p, bi: (bi, 0, p)),
        out_shape=jax.ShapeDtypeStruct((b, l, d), bf16),
        scratch_shapes=scratch,
        compiler_params=_params("arbitrary", "arbitrary"),
        name="a_attention",
    )(f_rows, k_nat, q_t, v_t)


def _a_sattn_body(n_valid, f_ref, kc_ref, kn_ref, vc_ref, vn_ref, qT_ref, o_ref, e_ref):
    hd = A_HEAD_DIM
    n_keys = e_ref.shape[1]

    @pl.when(pl.program_id(1) == 0)
    def _():
        live = lax.broadcasted_iota(jnp.int32, (n_keys, SAMPLE_PAD), 0) < n_valid
        for hh in range(2):
            e_ref[hh] = jnp.where(live, _bias_tile(f_ref[hh], n_keys, SAMPLE_PAD), NEG_INF)

    zeros = jnp.zeros((hd, SAMPLE_PAD), bf16)
    k_all = jnp.concatenate([kc_ref[...].astype(bf16), kn_ref[...]], axis=0)
    v_t = jnp.concatenate([vc_ref[...], vn_ref[...]], axis=0).T.astype(bf16)
    outs = []
    for hh in range(2):
        qh = qT_ref[hd * hh:hd * (hh + 1), :]
        w = jnp.concatenate([qh, zeros] if hh == 0 else [zeros, qh], axis=0)
        p, l = _softmax_cols(_dot(k_all, w) + e_ref[hh])
        outs.append(_dot(v_t[hd * hh:hd * (hh + 1)], p.astype(bf16)) * (1.0 / l))
    o_ref[...] = jnp.concatenate(outs, axis=0).T.astype(bf16)


def _a_sample_attention(f_rows, k_cache, k_new, v_cache, v_new, q_t, n_new):
    b, n_cache, d = k_cache.shape
    pair = 2 * A_HEAD_DIM
    n_keys = n_cache + SAMPLE_PAD
    new = pl.BlockSpec((None, SAMPLE_PAD, pair), lambda p, bi: (bi, 0, p))
    old = pl.BlockSpec((None, n_cache, pair), lambda p, bi: (bi, 0, p))
    return pl.pallas_call(
        functools.partial(_a_sattn_body, n_cache + n_new),
        grid=(d // pair, b),
        in_specs=[pl.BlockSpec((2, 1, A_BIAS_SPAN), lambda p, bi: (p, 0, 0)),
                  old, new, old, new,
                  pl.BlockSpec((None, pair, SAMPLE_PAD), lambda p, bi: (bi, p, 0))],
        out_specs=new,
        out_shape=jax.ShapeDtypeStruct((b, SAMPLE_PAD, d), bf16),
        scratch_shapes=[pltpu.VMEM((2, n_keys, SAMPLE_PAD), f32)],
        compiler_params=_params("arbitrary", "arbitrary"),
        name="a_sample_attention",
    )(f_rows, k_cache, k_new, v_cache, v_new, q_t)


def _out_body(o_ref, zs_ref, w_ref, x_ref, mod_ref, y_ref):
    y = _dot(o_ref[...] * zs_ref[...], w_ref[...])
    y_ref[...] = x_ref[...] + mod_ref[2:3, :] * y


def _out_project(o, zs, w_out, x, mod, tm):
    b, l, d = x.shape
    tile = pl.BlockSpec((None, tm, d), lambda bi, i: (bi, i, 0))
    return pl.pallas_call(
        _out_body,
        grid=(b, l // tm),
        in_specs=[tile, tile, _const(w_out.shape), tile,
                  pl.BlockSpec((None, 3, d), lambda bi, i: (bi, 0, 0))],
        out_specs=tile,
        out_shape=jax.ShapeDtypeStruct((b, l, d), f32),
        compiler_params=_params("arbitrary", "arbitrary"),
        name="out_project",
    )(o, zs, w_out, x, mod)


def _rotate(x, cos, sin):
    half = B_ROPE // 2
    x1, x2 = x[:half], x[half:]
    return jnp.concatenate([x1 * cos - x2 * sin, x2 * cos + x1 * sin], axis=0)


def _b_proj_body(x_ref, mod_ref, g_ref, wcq_ref, wckv_ref, wkr_ref, wz_ref, wuq_ref,
                 gcq_ref, gckv_ref, gkr_ref, gqn_ref, gqr_ref, cos_ref, sin_ref,
                 qT_ref, ckv_ref, kr_ref, zs_ref):
    hb = _ada_hidden(x_ref, mod_ref, g_ref)
    tm = hb.shape[0]
    cos = cos_ref[...]
    sin = sin_ref[...]
    cq = _rms_rows(_nt(wcq_ref[...], hb), gcq_ref[...], Q_LORA).astype(bf16)
    qT = _dot(wuq_ref[...], cq)
    gqn = gqn_ref[...]
    gqr = gqr_ref[...]
    for h in range(N_HEADS):
        base = B_QK * h
        nope = _rms_rows(qT[base:base + B_NOPE], gqn, B_NOPE)
        rope = _rotate(_rms_rows(qT[base + B_NOPE:base + B_QK], gqr, B_ROPE), cos, sin)
        qT_ref[base:base + B_QK, :] = jnp.concatenate([nope, rope], axis=0).astype(bf16)
    ckv = _rms_rows(_nt(wckv_ref[...], hb), gckv_ref[...], KV_LORA)
    ckv_ref[...] = ckv.T
    kr = _rotate(_rms_rows(_nt(wkr_ref[...], hb), gkr_ref[...], B_ROPE), cos, sin)
    kr_pad = jnp.concatenate([kr, jnp.zeros((128 - B_ROPE, tm), f32)], axis=0)
    kr_ref[...] = kr_pad.T[:, :B_ROPE]
    zs_ref[...] = _silu(_dot(hb, wz_ref[...])).astype(bf16)


def _b_project(x, mod, g, w, cos_t, sin_t, tm):
    b, l, d = x.shape
    tile = pl.BlockSpec((None, tm, d), lambda bi, i: (bi, i, 0))
    rot = pl.BlockSpec((B_ROPE // 2, tm), lambda bi, i: (0, i))
    nq = N_HEADS * B_QK
    return pl.pallas_call(
        _b_proj_body,
        grid=(b, l // tm),
        in_specs=[tile, pl.BlockSpec((None, 3, d), lambda bi, i: (bi, 0, 0)), _const((1, d)),
                  _const((Q_LORA, d)), _const((KV_LORA, d)), _const((B_ROPE, d)), _const((d, d)),
                  _const((nq, Q_LORA)),
                  _const((Q_LORA, 1)), _const((KV_LORA, 1)), _const((B_ROPE, 1)),
                  _const((B_NOPE, 1)), _const((B_ROPE, 1)), rot, rot],
        out_specs=[pl.BlockSpec((None, nq, tm), lambda bi, i: (bi, 0, i)),
                   pl.BlockSpec((None, tm, KV_LORA), lambda bi, i: (bi, i, 0)),
                   pl.BlockSpec((None, tm, B_ROPE), lambda bi, i: (bi, i, 0)),
                   tile],
        out_shape=[jax.ShapeDtypeStruct((b, nq, l), bf16),
                   jax.ShapeDtypeStruct((b, l, KV_LORA), f32),
                   jax.ShapeDtypeStruct((b, l, B_ROPE), f32),
                   jax.ShapeDtypeStruct((b, l, d), bf16)],
        compiler_params=_params("arbitrary", "arbitrary"),
        name="b_project",
    )(x, mod, g, w["cq_t"], w["ckv_t"], w["kr_t"], w["z"], w["uq_t"],
      w["g_cq"], w["g_ckv"], w["g_kr"], w["g_qn"], w["g_qr"], cos_t, sin_t)


def _b_kvup_body(ckv_ref, kr_ref, wukv_ref, gkn_ref, k_ref, vT_ref):
    tm = ckv_ref.shape[0]
    kvT = _nt(wukv_ref[...], ckv_ref[...].astype(bf16))
    eye = (lax.broadcasted_iota(jnp.int32, (B_ROPE, B_ROPE), 0)
           == lax.broadcasted_iota(jnp.int32, (B_ROPE, B_ROPE), 1)).astype(bf16)
    krT = _nt(eye, kr_ref[...].astype(bf16))
    n_tile_chunks = Q_GROUP // CHUNK
    key_chunk = (lax.broadcasted_iota(jnp.int32, (128 - B_QK, tm), 1) // CHUNK) % n_tile_chunks
    chunk_rows = (lax.broadcasted_iota(jnp.int32, (128 - B_QK, tm), 0) == key_chunk).astype(f32)
    tail = jnp.concatenate([krT, chunk_rows], axis=0)
    gkn = gkn_ref[...]
    per_head = B_NOPE + B_VDIM
    for h in range(N_HEADS):
        base = per_head * h
        kn = _rms_rows(kvT[base:base + B_NOPE], gkn, B_NOPE)
        k_ref[h] = jnp.concatenate([kn, tail], axis=0).T.astype(bf16)
        vT_ref[B_VDIM * h:B_VDIM * (h + 1), :] = kvT[base + B_NOPE:base + per_head].astype(bf16)


def _b_kv_up(ckv, kr, wukv_t, g_kn, tm):
    b, l, _ = ckv.shape
    return pl.pallas_call(
        _b_kvup_body,
        grid=(b, l // tm),
        in_specs=[pl.BlockSpec((None, tm, KV_LORA), lambda bi, i: (bi, i, 0)),
                  pl.BlockSpec((None, tm, B_ROPE), lambda bi, i: (bi, i, 0)),
                  _const(wukv_t.shape), _const((B_NOPE, 1))],
        out_specs=[pl.BlockSpec((None, N_HEADS, tm, 128), lambda bi, i: (bi, 0, i, 0)),
                   pl.BlockSpec((None, N_HEADS * B_VDIM, tm), lambda bi, i: (bi, 0, i))],
        out_shape=[jax.ShapeDtypeStruct((b, N_HEADS, l, 128), bf16),
                   jax.ShapeDtypeStruct((b, N_HEADS * B_VDIM, l), bf16)],
        compiler_params=_params("arbitrary", "arbitrary"),
        name="b_kv_up",
    )(ckv, kr, wukv_t, g_kn)


def _b_item_table(n_tiles):
    rows = [(t, j, int(j == t)) for t in range(n_tiles) for j in range(t, -1, -1)]
    return np.asarray(rows, np.int32).T.copy()


def _b_items_per_step(n_items):
    return max(u for u in range(1, B_ITEMS_PER_STEP + 1) if n_items % u == 0 and n_items // u >= 3)


def _b_attn_body(seq, per, tab_ref, k_ref, qT_ref, vT_ref, o_ref, s_buf, c_buf, pv_buf, a_buf, m_st, acc_st):
    t_sz = Q_GROUP
    n_tiles = seq // t_sz
    n_steps = n_tiles * (n_tiles + 1) // 2 // per
    ones = jnp.ones((ONES_ROWS, t_sz), bf16)
    key_chunk = lax.broadcasted_iota(jnp.int32, (128 - B_QK, t_sz), 0)
    qry_chunk = lax.broadcasted_iota(jnp.int32, (128 - B_QK, t_sz), 1) // CHUNK
    mask_rows = jnp.where((key_chunk < t_sz // CHUNK) & (key_chunk > qry_chunk), NEG_INF, 0.0).astype(bf16)
    zero_rows = jnp.zeros((128 - B_QK, t_sz), bf16)

    def item(step, u):
        n = step * per + u
        return tab_ref[0, n], tab_ref[1, n], tab_ref[2, n] == 1

    def scores(step, par, u):
        t, j, diag = item(step, u)
        q0 = pl.multiple_of(t * t_sz, t_sz)
        k0 = pl.multiple_of(j * t_sz, t_sz)
        pad = jnp.where(diag, mask_rows, zero_rows)
        for hh in range(2):
            w = jnp.concatenate([qT_ref[B_QK * hh:B_QK * (hh + 1), pl.ds(q0, t_sz)], pad], axis=0)
            s = _dot(k_ref[hh, pl.ds(k0, t_sz), :], w)
            s_buf[par, u, hh] = s
            c_buf[par, u, hh] = jnp.max(s, axis=0, keepdims=True)

    def values(step, par, u):
        t, j, diag = item(step, u)
        k0 = pl.multiple_of(j * t_sz, t_sz)
        for hh in range(2):
            m_old = jnp.where(diag, NEG_INF, m_st[t, hh])
            m_new = jnp.maximum(m_old, c_buf[par, u, hh])
            m_st[t, hh] = m_new
            a_buf[par, u, hh] = jnp.exp2(m_old - m_new)
            p = jnp.exp2(s_buf[par, u, hh] - m_new).astype(bf16)
            v_ext = jnp.concatenate([vT_ref[B_VDIM * hh:B_VDIM * (hh + 1), pl.ds(k0, t_sz)], ones], axis=0)
            pv_buf[par, u, hh] = _dot(v_ext, p)

    def accumulate(step, par, u):
        t, _, diag = item(step, u)
        for hh in range(2):
            pv = pv_buf[par, u, hh]
            acc_st[t, hh] = jnp.where(diag, pv, a_buf[par, u, hh] * acc_st[t, hh] + pv)

    for u in range(per):
        scores(0, 0, u)
    for u in range(per):
        values(0, 0, u)
        scores(1, 1, u)

    def body(step, carry):
        par = lax.rem(step, 2)
        for u in range(per):
            accumulate(step - 1, 1 - par, u)
            values(step, par, u)
            scores(step + 1, 1 - par, u)
        return carry

    lax.fori_loop(1, n_steps - 1, body, 0)
    last = n_steps - 1
    for u in range(per):
        accumulate(last - 1, (last - 1) % 2, u)
        values(last, last % 2, u)
    for u in range(per):
        accumulate(last, last % 2, u)
    for t in range(n_tiles):
        outs = [_normalise(acc_st[t, hh], B_VDIM) for hh in range(2)]
        o_ref[t * t_sz:(t + 1) * t_sz, :] = jnp.concatenate(outs, axis=0).T.astype(bf16)


def _b_attention(k_nat, q_t, v_t):
    b, _, l, _ = k_nat.shape
    t_sz = Q_GROUP
    n_tiles = l // t_sz
    table = _b_item_table(n_tiles)
    per = _b_items_per_step(table.shape[1])
    ext = B_VDIM + ONES_ROWS
    scratch = [pltpu.VMEM((2, per, 2, t_sz, t_sz), f32), pltpu.VMEM((2, per, 2, 1, t_sz), f32),
               pltpu.VMEM((2, per, 2, ext, t_sz), f32), pltpu.VMEM((2, per, 2, 1, t_sz), f32),
               pltpu.VMEM((n_tiles, 2, 1, t_sz), f32), pltpu.VMEM((n_tiles, 2, ext, t_sz), f32)]
    grid_spec = pltpu.PrefetchScalarGridSpec(
        num_scalar_prefetch=1,
        grid=(b, N_HEADS // 2),
        in_specs=[pl.BlockSpec((None, 2, l, 128), lambda bi, p, tab: (bi, p, 0, 0)),
                  pl.BlockSpec((None, 2 * B_QK, l), lambda bi, p, tab: (bi, p, 0)),
                  pl.BlockSpec((None, 2 * B_VDIM, l), lambda bi, p, tab: (bi, p, 0))],
        out_specs=pl.BlockSpec((None, l, 2 * B_VDIM), lambda bi, p, tab: (bi, 0, p)),
        scratch_shapes=scratch)
    return pl.pallas_call(
        functools.partial(_b_attn_body, l, per),
        grid_spec=grid_spec,
        out_shape=jax.ShapeDtypeStruct((b, l, N_HEADS * B_VDIM), bf16),
        compiler_params=_params("arbitrary", "arbitrary"),
        name="b_attention",
    )(jnp.asarray(table), k_nat, q_t, v_t)


def _b_sattn_body(n_new, kc_ref, kn_ref, vc_ref, vn_ref, qT_ref, o_ref):
    t = SAMPLE_PAD
    zpad = jnp.zeros((128 - B_QK, t), bf16)
    new_mask = jnp.where(lax.broadcasted_iota(jnp.int32, (t, t), 0) < n_new, 0.0, NEG_INF).astype(f32)
    outs = []
    for hh in range(2):
        w = jnp.concatenate([qT_ref[B_QK * hh:B_QK * (hh + 1), :], zpad], axis=0)
        s_c = _dot(kc_ref[hh], w)
        s_n = _dot(kn_ref[hh], w) + new_mask
        m = jnp.maximum(jnp.max(s_c, axis=0, keepdims=True), jnp.max(s_n, axis=0, keepdims=True))
        p_c = jnp.exp2(s_c - m)
        p_n = jnp.exp2(s_n - m)
        l = jnp.sum(p_c, axis=0, keepdims=True) + jnp.sum(p_n, axis=0, keepdims=True)
        rows = slice(B_VDIM * hh, B_VDIM * (hh + 1))
        o_t = _dot(vc_ref[rows, :], p_c.astype(bf16)) + _dot(vn_ref[rows, :], p_n.astype(bf16))
        outs.append(o_t * (1.0 / l))
    o_ref[...] = jnp.concatenate(outs, axis=0).T.astype(bf16)


def _b_sample_attention(k_cache, k_new, v_cache, v_new, q_t, n_new):
    b, _, past, _ = k_cache.shape
    t = SAMPLE_PAD
    return pl.pallas_call(
        functools.partial(_b_sattn_body, n_new),
        grid=(b, N_HEADS // 2),
        in_specs=[pl.BlockSpec((None, 2, past, 128), lambda bi, p: (bi, p, 0, 0)),
                  pl.BlockSpec((None, 2, t, 128), lambda bi, p: (bi, p, 0, 0)),
                  pl.BlockSpec((None, 2 * B_VDIM, past), lambda bi, p: (bi, p, 0)),
                  pl.BlockSpec((None, 2 * B_VDIM, t), lambda bi, p: (bi, p, 0)),
                  pl.BlockSpec((None, 2 * B_QK, t), lambda bi, p: (bi, p, 0))],
        out_specs=pl.BlockSpec((None, t, 2 * B_VDIM), lambda bi, p: (bi, 0, p)),
        out_shape=jax.ShapeDtypeStruct((b, t, N_HEADS * B_VDIM), bf16),
        compiler_params=_params("arbitrary", "arbitrary"),
        name="b_sample_attention",
    )(k_cache, k_new, v_cache, v_new, q_t)


def _bias_rows(table):
    left = A_BIAS_ROWS - 1 - WINDOW - REL_CLIP
    right = A_BIAS_SPAN - left - table.shape[-1]
    return (jnp.pad(table.astype(f32), ((0, 0), (left, right)), mode="edge") * LOG2E)[:, None, :]


def _rope_tables(pos):
    half = B_ROPE // 2
    inv = ROPE_THETA ** (-jnp.arange(half, dtype=f32) / half)
    ang = pos.astype(f32)[:, None] * inv[None, :]
    return jnp.cos(ang).T, jnp.sin(ang).T


def _col(g, scale=1.0):
    return (g.astype(f32) * scale)[:, None]


def kernel(x_prompt, x_sample, cache_a_k, cache_a_v, cache_mla_ckv, cache_mla_krope, c_prompt, c_sample, norm_g, ada_w, ada_b, a_w_in, a_g_q, a_g_k, a_rel_bias, a_w_out, b_w_in, b_g_cq, b_w_uq, b_g_ckv, b_w_ukv, b_g_qn, b_g_qr, b_g_kn, b_g_kr, b_w_out):
    bp, seq, d = x_prompt.shape
    bs, dec, _ = x_sample.shape
    past = cache_mla_ckv.shape[2]
    n_cache_a = cache_a_k.shape[2]
    assert d == N_HEADS * A_HEAD_DIM and seq % (2 * Q_GROUP) == 0 and seq >= A_KEYS
    assert past % CHUNK == 0 and dec <= CHUNK and dec <= SAMPLE_PAD and n_cache_a == WINDOW
    cache_rows = min(WINDOW, seq)
    tm = 256

    mod = _modulation(jnp.concatenate([c_prompt, c_sample], axis=0), ada_w, ada_b)
    mod = mod.reshape(mod.shape[0], bp + bs, 3, d)
    xs_pad = jnp.pad(x_sample, ((0, 0), (0, SAMPLE_PAD - dec), (0, 0)))

    w_in = a_w_in[0]
    wq_t, wk_t, wv_t = (w_in[:, d * n:d * (n + 1)].T.astype(bf16) for n in range(3))
    wz = w_in[:, 3 * d:].astype(bf16)
    w_out = a_w_out[0].astype(bf16)
    gq = _col(a_g_q[0], A_HEAD_DIM ** -0.5 * LOG2E)
    gk = _col(a_g_k[0])
    g0 = norm_g[0][None, :]
    f_rows = _bias_rows(a_rel_bias[0])
    mod_p, mod_s = mod[0, :bp], mod[0, bp:]

    q_t, k_nat, v_t, zs, k32, v32 = _a_project(x_prompt, mod_p, g0, wq_t, wk_t, wv_t, wz, gq, gk, tm, cache_rows)
    o = _a_attention(f_rows, k_nat, q_t, v_t)
    y_p = _out_project(o, zs, w_out, x_prompt, mod_p, tm)
    new_a_k_p = k32.reshape(1, bp, cache_rows, N_HEADS, A_HEAD_DIM)
    new_a_v_p = v32.reshape(1, bp, cache_rows, N_HEADS, A_HEAD_DIM)

    q_t, k_nat, v_t, zs, k32, v32 = _a_project(xs_pad, mod_s, g0, wq_t, wk_t, wv_t, wz, gq, gk,
                                               SAMPLE_PAD, SAMPLE_PAD)
    o = _a_sample_attention(f_rows, cache_a_k[0].reshape(bs, n_cache_a, d), k_nat,
                            cache_a_v[0].reshape(bs, n_cache_a, d), v32, q_t, dec)
    ys_pad = _out_project(o, zs, w_out, xs_pad, mod_s, SAMPLE_PAD)
    new_a_k_s = k32[:, :dec].reshape(1, bs, dec, N_HEADS, A_HEAD_DIM)
    new_a_v_s = v32[:, :dec].reshape(1, bs, dec, N_HEADS, A_HEAD_DIM)

    w_in = b_w_in[0]
    scale = B_QK ** -0.5 * LOG2E
    wb = {
        "cq_t": w_in[:, :Q_LORA].T.astype(bf16),
        "ckv_t": w_in[:, Q_LORA:Q_LORA + KV_LORA].T.astype(bf16),
        "kr_t": w_in[:, Q_LORA + KV_LORA:Q_LORA + KV_LORA + B_ROPE].T.astype(bf16),
        "z": w_in[:, Q_LORA + KV_LORA + B_ROPE:].astype(bf16),
        "uq_t": b_w_uq[0].T.astype(bf16),
        "g_cq": _col(b_g_cq[0]), "g_ckv": _col(b_g_ckv[0]), "g_kr": _col(b_g_kr[0]),
        "g_qn": _col(b_g_qn[0], scale), "g_qr": _col(b_g_qr[0], scale),
    }
    wukv_t = b_w_ukv[0].T.astype(bf16)
    g_kn = _col(b_g_kn[0])
    w_out = b_w_out[0].astype(bf16)
    g1 = norm_g[1][None, :]
    mod_p, mod_s = mod[1, :bp], mod[1, bp:]

    cos_t, sin_t = _rope_tables(jnp.arange(seq, dtype=jnp.int32))
    q_t, ckv_p, kr_p, zs = _b_project(y_p, mod_p, g1, wb, cos_t, sin_t, tm)
    k_nat, v_t = _b_kv_up(ckv_p, kr_p, wukv_t, g_kn, tm)
    o = _b_attention(k_nat, q_t, v_t)
    y_p = _out_project(o, zs, w_out, y_p, mod_p, tm)

    cos_t, sin_t = _rope_tables(past + jnp.arange(SAMPLE_PAD, dtype=jnp.int32))
    q_t, ckv_s, kr_s, zs = _b_project(ys_pad, mod_s, g1, wb, cos_t, sin_t, SAMPLE_PAD)
    k_new, v_new = _b_kv_up(ckv_s, kr_s, wukv_t, g_kn, SAMPLE_PAD)
    k_old, v_old = _b_kv_up(cache_mla_ckv[0], cache_mla_krope[0], wukv_t, g_kn, tm)
    o = _b_sample_attention(k_old, k_new, v_old, v_new, q_t, dec)
    ys_pad = _out_project(o, zs, w_out, ys_pad, mod_s, SAMPLE_PAD)

    return (y_p, ys_pad[:, :dec], new_a_k_p, new_a_v_p, new_a_k_s, new_a_v_s,
            ckv_p[None], kr_p[None], ckv_s[None, :, :dec], kr_s[None, :, :dec])
```

```python
import functools

import numpy as np

import jax
import jax.numpy as jnp
from jax import lax
from jax.experimental import pallas as pl
from jax.experimental.pallas import tpu as pltpu

f32 = jnp.float32
bf16 = jnp.bfloat16

EPS = 1e-6
NEG_INF = -1e30
CHUNK = 64
BAND_CHUNKS = 8
WINDOW = BAND_CHUNKS * CHUNK
REL_CLIP = 128
ROPE_THETA = 10000.0
LOG2E = 1.4426950408889634

N_HEADS = 16
A_HEAD_DIM = 64
B_NOPE = 64
B_ROPE = 32
B_VDIM = 64
B_QK = B_NOPE + B_ROPE
KV_LORA = 256
Q_LORA = 384

Q_GROUP = 256
A_KEYS = WINDOW + Q_GROUP
A_BIAS_ROWS = A_KEYS + WINDOW
A_BIAS_SPAN = A_BIAS_ROWS + Q_GROUP
SAMPLE_PAD = 128
ROW_TILE = 512
B_PROJ_TILE = 256
B_ITEMS_PER_STEP = 4
ONES_ROWS = 8
VMEM_LIMIT = 56 * 1024 * 1024


def _silu(x):
    return x * (1.0 / (1.0 + jnp.exp(-x)))


def _nt(a, b):
    return lax.dot_general(a, b, (((1,), (1,)), ((), ())), preferred_element_type=f32)


def _dot(a, b):
    return jnp.dot(a, b, preferred_element_type=f32)


def _rms_rows(blk, gain_col, n):
    ms = jnp.sum(blk * blk, axis=0, keepdims=True) * (1.0 / n)
    return blk * lax.rsqrt(ms + EPS) * gain_col


def _ada_hidden(x_ref, mod_ref, g_ref):
    x = x_ref[...]
    ms = jnp.mean(x * x, axis=-1, keepdims=True)
    xn = x * lax.rsqrt(ms + EPS) * g_ref[...]
    return (xn * (1.0 + mod_ref[1:2, :]) + mod_ref[0:1, :]).astype(bf16)


def _softmax_cols(s):
    m = jnp.max(s, axis=0, keepdims=True)
    p = jnp.exp2(s - m)
    return p, jnp.sum(p, axis=0, keepdims=True)


def _normalise(o_ext, rows):
    return o_ext[:rows] * (1.0 / o_ext[rows:rows + 1])


def _bias_tile(f_row, n_rows, n_cols):
    blocks = []
    for a in range(pl.cdiv(n_rows, Q_GROUP)):
        lo = A_BIAS_ROWS - Q_GROUP * (a + 1)
        x = jnp.broadcast_to(f_row[:, lo:lo + 2 * Q_GROUP], (Q_GROUP, 2 * Q_GROUP))
        y = pltpu.roll(x, Q_GROUP + 1, 1, stride=1, stride_axis=0)[:, :n_cols]
        r = lax.broadcasted_iota(jnp.int32, (Q_GROUP, n_cols), 0) + Q_GROUP * a
        i = lax.broadcasted_iota(jnp.int32, (Q_GROUP, n_cols), 1)
        back = BAND_CHUNKS + i // CHUNK - r // CHUNK
        blocks.append(jnp.where((back >= 0) & (back <= BAND_CHUNKS), y, NEG_INF))
    return jnp.concatenate(blocks, axis=0)[:n_rows]


def _params(*sem):
    return pltpu.CompilerParams(dimension_semantics=sem, vmem_limit_bytes=VMEM_LIMIT)


def _const(shape):
    nd = len(shape)
    return pl.BlockSpec(shape, lambda *_: (0,) * nd)


def _weight(shape):
    nd = len(shape)
    return pl.BlockSpec(shape, lambda *_: (0,) * nd, pipeline_mode=pl.Buffered(1))


def _mod_body(c_ref, w_ref, b_ref, o_ref):
    o_ref[...] = jnp.dot(_silu(c_ref[...]), w_ref[...], preferred_element_type=f32,
                         precision=lax.Precision.HIGHEST) + b_ref[...]


def _modulation(c_all, ada_w, ada_b):
    depth, d, n3 = ada_w.shape
    rows = c_all.shape[0]
    tn = d
    return pl.pallas_call(
        _mod_body,
        grid=(depth, n3 // tn),
        in_specs=[pl.BlockSpec((rows, d), lambda l, j: (0, 0)),
                  pl.BlockSpec((None, d, tn), lambda l, j: (l, 0, j)),
                  pl.BlockSpec((None, 1, tn), lambda l, j: (l, 0, j))],
        out_specs=pl.BlockSpec((None, rows, tn), lambda l, j: (l, 0, j)),
        out_shape=jax.ShapeDtypeStruct((depth, rows, n3), f32),
        compiler_params=_params("arbitrary", "arbitrary"),
        name="modulation",
    )(c_all, ada_w, ada_b.reshape(depth, 1, n3))


def _a_proj_body(n_skip, x_ref, mod_ref, g_ref, wq_ref, wk_ref, wv_ref, wz_ref, gq_ref, gk_ref,
                 qT_ref, k_ref, vT_ref, zs_ref, k32_ref, v32_ref, kf_ref):
    hb = _ada_hidden(x_ref, mod_ref, g_ref)
    z = _dot(hb, wz_ref[...])
    kT = _nt(wk_ref[...], hb)
    zs_ref[...] = _silu(z).astype(bf16)
    qT = _nt(wq_ref[...], hb)
    gq = gq_ref[...]
    gk = gk_ref[...]
    hd = A_HEAD_DIM
    pairs = [slice(2 * hd * p, 2 * hd * (p + 1)) for p in range(N_HEADS // 2)]
    for p, rows in enumerate(pairs):
        k_pair = [_rms_rows(kT[hd * h:hd * (h + 1)], gk, hd) for h in (2 * p, 2 * p + 1)]
        k_nat = jnp.concatenate(k_pair, axis=0).T
        k_ref[:, rows] = k_nat.astype(bf16)
        kf_ref[:, rows] = k_nat
    vT = _nt(wv_ref[...], hb)
    for p, rows in enumerate(pairs):
        q_pair = [_rms_rows(qT[hd * h:hd * (h + 1)], gq, hd) for h in (2 * p, 2 * p + 1)]
        qT_ref[rows, :] = jnp.concatenate(q_pair, axis=0).astype(bf16)
    vT_ref[...] = vT.astype(bf16)

    @pl.when(pl.program_id(1) >= n_skip)
    def _():
        k32_ref[...] = kf_ref[...]
        v32_ref[...] = vT.T


def _a_project(x, mod, g, wq_t, wk_t, wv_t, wz, gq, gk, tm, cache_rows):
    b, l, d = x.shape
    n_skip = (l - cache_rows) // tm
    tile = pl.BlockSpec((None, tm, d), lambda bi, i: (bi, i, 0))
    tile_t = pl.BlockSpec((None, d, tm), lambda bi, i: (bi, 0, i))
    cache = pl.BlockSpec((None, tm, d), lambda bi, i: (bi, jnp.maximum(i - n_skip, 0), 0))
    return pl.pallas_call(
        functools.partial(_a_proj_body, n_skip),
        grid=(b, l // tm),
        in_specs=[tile, pl.BlockSpec((None, 3, d), lambda bi, i: (bi, 0, 0)), _const((1, d)),
                  _weight((d, d)), _weight((d, d)), _weight((d, d)), _weight((d, d)),
                  _const((A_HEAD_DIM, 1)), _const((A_HEAD_DIM, 1))],
        out_specs=[tile_t, tile, tile_t, tile, cache, cache],
        out_shape=[jax.ShapeDtypeStruct((b, d, l), bf16), jax.ShapeDtypeStruct((b, l, d), bf16),
                   jax.ShapeDtypeStruct((b, d, l), bf16), jax.ShapeDtypeStruct((b, l, d), bf16),
                   jax.ShapeDtypeStruct((b, cache_rows, d), f32),
                   jax.ShapeDtypeStruct((b, cache_rows, d), f32)],
        scratch_shapes=[pltpu.VMEM((tm, d), f32)],
        compiler_params=_params("arbitrary", "arbitrary"),
        name="a_project",
    )(x, mod, g, wq_t, wk_t, wv_t, wz, gq, gk)


def _a_attn_body(seq, f_ref, k_ref, qT_ref, vT_ref, o_ref, e_ref, s0, s1, m0, m1, o0, o1):
    hd = A_HEAD_DIM
    n_groups = seq // Q_GROUP
    s_buf, m_buf, o_buf = (s0, s1), (m0, m1), (o0, o1)

    @pl.when(pl.program_id(1) == 0)
    def _():
        for hh in range(2):
            e_ref[hh] = _bias_tile(f_ref[hh], A_BIAS_ROWS, Q_GROUP)

    zeros = jnp.zeros((hd, Q_GROUP), bf16)
    ones = jnp.ones((ONES_ROWS, A_KEYS), bf16)

    def offsets(g):
        q0 = pl.multiple_of(g * Q_GROUP, Q_GROUP)
        w0 = pl.multiple_of(jnp.maximum(g * Q_GROUP - WINDOW, 0), Q_GROUP)
        e0 = pl.multiple_of(WINDOW - (g * Q_GROUP - w0), Q_GROUP)
        return q0, w0, e0

    def scores(g, par):
        q0, w0, e0 = offsets(g)
        k_win = k_ref[pl.ds(w0, A_KEYS), :]
        for hh in range(2):
            qh = qT_ref[hd * hh:hd * (hh + 1), pl.ds(q0, Q_GROUP)]
            w = jnp.concatenate([qh, zeros] if hh == 0 else [zeros, qh], axis=0)
            s = _dot(k_win, w) + e_ref[hh, pl.ds(e0, A_KEYS), :]
            s_buf[par][hh] = s
            m_buf[par][hh] = jnp.max(s, axis=0, keepdims=True)

    def values(g, par):
        _, w0, _ = offsets(g)
        for hh in range(2):
            p = jnp.exp2(s_buf[par][hh] - m_buf[par][hh]).astype(bf16)
            v_ext = jnp.concatenate([vT_ref[hd * hh:hd * (hh + 1), pl.ds(w0, A_KEYS)], ones], axis=0)
            o_buf[par][hh] = _dot(v_ext, p)

    def finish(g, par):
        q0, _, _ = offsets(g)
        outs = [_normalise(o_buf[par][hh], hd) for hh in range(2)]
        o_ref[pl.ds(q0, Q_GROUP), :] = jnp.concatenate(outs, axis=0).T.astype(bf16)

    scores(0, 0)
    scores(1, 1)
    values(0, 0)

    def body(i, carry):
        g = 2 * i + 1
        scores(g + 1, 0)
        values(g, 1)
        finish(g - 1, 0)
        scores(g + 2, 1)
        values(g + 1, 0)
        finish(g, 1)
        return carry

    lax.fori_loop(0, (n_groups - 2) // 2, body, 0)
    values(n_groups - 1, 1)
    finish(n_groups - 2, 0)
    finish(n_groups - 1, 1)


def _a_attention(f_rows, k_nat, q_t, v_t):
    b, l, d = k_nat.shape
    pair = 2 * A_HEAD_DIM
    scratch = ([pltpu.VMEM((2, A_BIAS_ROWS, Q_GROUP), f32)]
               + [pltpu.VMEM((2, A_KEYS, Q_GROUP), f32)] * 2
               + [pltpu.VMEM((2, 1, Q_GROUP), f32)] * 2
               + [pltpu.VMEM((2, A_HEAD_DIM + ONES_ROWS, Q_GROUP), f32)] * 2)
    return pl.pallas_call(
        functools.partial(_a_attn_body, l),
        grid=(d // pair, b),
        in_specs=[pl.BlockSpec((2, 1, A_BIAS_SPAN), lambda p, bi: (p, 0, 0)),
                  pl.BlockSpec((None, l, pair), lambda p, bi: (bi, 0, p)),
                  pl.BlockSpec((None, pair, l), lambda p, bi: (bi, p, 0)),
                  pl.BlockSpec((None, pair, l), lambda p, bi: (bi, p, 0))],
        out_specs=pl.BlockSpec((None, l, pair), lambda p, bi: (bi, 0, p)),
        out_shape=jax.ShapeDtypeStruct((b, l, d), bf16),
        scratch_shapes=scratch,
        compiler_params=_params("arbitrary", "arbitrary"),
        name="a_attention",
    )(f_rows, k_nat, q_t, v_t)


def _a_sattn_body(n_valid, f_ref, kc_ref, kn_ref, vc_ref, vn_ref, qT_ref, o_ref, e_ref):
    hd = A_HEAD_DIM
    n_keys = e_ref.shape[1]

    @pl.when(pl.program_id(1) == 0)
    def _():
        live = lax.broadcasted_iota(jnp.int32, (n_keys, SAMPLE_PAD), 0) < n_valid
        for hh in range(2):
            e_ref[hh] = jnp.where(live, _bias_tile(f_ref[hh], n_keys, SAMPLE_PAD), NEG_INF)

    zeros = jnp.zeros((hd, SAMPLE_PAD), bf16)
    k_all = jnp.concatenate([kc_ref[...].astype(bf16), kn_ref[...]], axis=0)
    v_t = jnp.concatenate([vc_ref[...], vn_ref[...]], axis=0).T.astype(bf16)
    outs = []
    for hh in range(2):
        qh = qT_ref[hd * hh:hd * (hh + 1), :]
        w = jnp.concatenate([qh, zeros] if hh == 0 else [zeros, qh], axis=0)
        p, l = _softmax_cols(_dot(k_all, w) + e_ref[hh])
        outs.append(_dot(v_t[hd * hh:hd * (hh + 1)], p.astype(bf16)) * (1.0 / l))
    o_ref[...] = jnp.concatenate(outs, axis=0).T.astype(bf16)


def _a_sample_attention(f_rows, k_cache, k_new, v_cache, v_new, q_t, n_new):
    b, n_cache, d = k_cache.shape
    pair = 2 * A_HEAD_DIM
    n_keys = n_cache + SAMPLE_PAD
    new = pl.BlockSpec((None, SAMPLE_PAD, pair), lambda p, bi: (bi, 0, p))
    old = pl.BlockSpec((None, n_cache, pair), lambda p, bi: (bi, 0, p))
    return pl.pallas_call(
        functools.partial(_a_sattn_body, n_cache + n_new),
        grid=(d // pair, b),
        in_specs=[pl.BlockSpec((2, 1, A_BIAS_SPAN), lambda p, bi: (p, 0, 0)),
                  old, new, old, new,
                  pl.BlockSpec((None, pair, SAMPLE_PAD), lambda p, bi: (bi, p, 0))],
        out_specs=new,
        out_shape=jax.ShapeDtypeStruct((b, SAMPLE_PAD, d), bf16),
        scratch_shapes=[pltpu.VMEM((2, n_keys, SAMPLE_PAD), f32)],
        compiler_params=_params("arbitrary", "arbitrary"),
        name="a_sample_attention",
    )(f_rows, k_cache, k_new, v_cache, v_new, q_t)


def _out_body(o_ref, zs_ref, w_ref, x_ref, mod_ref, y_ref):
    y = _dot(o_ref[...] * zs_ref[...], w_ref[...])
    y_ref[...] = x_ref[...] + mod_ref[2:3, :] * y


def _out_project(o, zs, w_out, x, mod, tm):
    b, l, d = x.shape
    tile = pl.BlockSpec((None, tm, d), lambda bi, i: (bi, i, 0))
    return pl.pallas_call(
        _out_body,
        grid=(b, l // tm),
        in_specs=[tile, tile, _const(w_out.shape), tile,
                  pl.BlockSpec((None, 3, d), lambda bi, i: (bi, 0, 0))],
        out_specs=tile,
        out_shape=jax.ShapeDtypeStruct((b, l, d), f32),
        compiler_params=_params("arbitrary", "arbitrary"),
        name="out_project",
    )(o, zs, w_out, x, mod)


def _rotate(x, cos, sin):
    half = B_ROPE // 2
    x1, x2 = x[:half], x[half:]
    return jnp.concatenate([x1 * cos - x2 * sin, x2 * cos + x1 * sin], axis=0)


def _b_proj_body(x_ref, mod_ref, g_ref, wcq_ref, wckv_ref, wkr_ref, wz_ref, wuq_ref,
                 gcq_ref, gckv_ref, gkr_ref, gqn_ref, gqr_ref, cos_ref, sin_ref,
                 qT_ref, ckv_ref, kr_ref, zs_ref):
    hb = _ada_hidden(x_ref, mod_ref, g_ref)
    tm = hb.shape[0]
    cos = cos_ref[...]
    sin = sin_ref[...]
    cq_raw = _nt(wcq_ref[...], hb)
    z = _dot(hb, wz_ref[...])
    ckv_raw = _nt(wckv_ref[...], hb)
    kr_raw = _nt(wkr_ref[...], hb)
    cq = _rms_rows(cq_raw, gcq_ref[...], Q_LORA).astype(bf16)
    qT = _dot(wuq_ref[...], cq)
    zs_ref[...] = _silu(z).astype(bf16)
    ckv_ref[...] = _rms_rows(ckv_raw, gckv_ref[...], KV_LORA).T
    kr = _rotate(_rms_rows(kr_raw, gkr_ref[...], B_ROPE), cos, sin)
    kr_pad = jnp.concatenate([kr, jnp.zeros((128 - B_ROPE, tm), f32)], axis=0)
    kr_ref[...] = kr_pad.T[:, :B_ROPE]
    gqn = gqn_ref[...]
    gqr = gqr_ref[...]
    for h in range(N_HEADS):
        base = B_QK * h
        nope = _rms_rows(qT[base:base + B_NOPE], gqn, B_NOPE)
        rope = _rotate(_rms_rows(qT[base + B_NOPE:base + B_QK], gqr, B_ROPE), cos, sin)
        qT_ref[base:base + B_QK, :] = jnp.concatenate([nope, rope], axis=0).astype(bf16)


def _b_project(x, mod, g, w, cos_t, sin_t, tm):
    b, l, d = x.shape
    tile = pl.BlockSpec((None, tm, d), lambda bi, i: (bi, i, 0))
    rot = pl.BlockSpec((B_ROPE // 2, tm), lambda bi, i: (0, i))
    nq = N_HEADS * B_QK
    return pl.pallas_call(
        _b_proj_body,
        grid=(b, l // tm),
        in_specs=[tile, pl.BlockSpec((None, 3, d), lambda bi, i: (bi, 0, 0)), _const((1, d)),
                  _weight((Q_LORA, d)), _weight((KV_LORA, d)), _weight((B_ROPE, d)), _weight((d, d)),
                  _weight((nq, Q_LORA)),
                  _const((Q_LORA, 1)), _const((KV_LORA, 1)), _const((B_ROPE, 1)),
                  _const((B_NOPE, 1)), _const((B_ROPE, 1)), rot, rot],
        out_specs=[pl.BlockSpec((None, nq, tm), lambda bi, i: (bi, 0, i)),
                   pl.BlockSpec((None, tm, KV_LORA), lambda bi, i: (bi, i, 0)),
                   pl.BlockSpec((None, tm, B_ROPE), lambda bi, i: (bi, i, 0)),
                   tile],
        out_shape=[jax.ShapeDtypeStruct((b, nq, l), bf16),
                   jax.ShapeDtypeStruct((b, l, KV_LORA), f32),
                   jax.ShapeDtypeStruct((b, l, B_ROPE), f32),
                   jax.ShapeDtypeStruct((b, l, d), bf16)],
        compiler_params=_params("arbitrary", "arbitrary"),
        name="b_project",
    )(x, mod, g, w["cq_t"], w["ckv_t"], w["kr_t"], w["z"], w["uq_t"],
      w["g_cq"], w["g_ckv"], w["g_kr"], w["g_qn"], w["g_qr"], cos_t, sin_t)


def _b_kvup_body(ckv_ref, kr_ref, wukv_ref, gkn_ref, k_ref, vT_ref):
    tm = ckv_ref.shape[0]
    kvT = _nt(wukv_ref[...], ckv_ref[...].astype(bf16))
    eye = (lax.broadcasted_iota(jnp.int32, (B_ROPE, B_ROPE), 0)
           == lax.broadcasted_iota(jnp.int32, (B_ROPE, B_ROPE), 1)).astype(bf16)
    krT = _nt(eye, kr_ref[...].astype(bf16))
    n_tile_chunks = Q_GROUP // CHUNK
    key_chunk = (lax.broadcasted_iota(jnp.int32, (128 - B_QK, tm), 1) // CHUNK) % n_tile_chunks
    chunk_rows = (lax.broadcasted_iota(jnp.int32, (128 - B_QK, tm), 0) == key_chunk).astype(f32)
    tail = jnp.concatenate([krT, chunk_rows], axis=0)
    gkn = gkn_ref[...]
    per_head = B_NOPE + B_VDIM
    for h in range(N_HEADS):
        base = per_head * h
        kn = _rms_rows(kvT[base:base + B_NOPE], gkn, B_NOPE)
        k_ref[h] = jnp.concatenate([kn, tail], axis=0).T.astype(bf16)
        vT_ref[B_VDIM * h:B_VDIM * (h + 1), :] = kvT[base + B_NOPE:base + per_head].astype(bf16)


def _b_kv_up(ckv, kr, wukv_t, g_kn, tm):
    b, l, _ = ckv.shape
    return pl.pallas_call(
        _b_kvup_body,
        grid=(b, l // tm),
        in_specs=[pl.BlockSpec((None, tm, KV_LORA), lambda bi, i: (bi, i, 0)),
                  pl.BlockSpec((None, tm, B_ROPE), lambda bi, i: (bi, i, 0)),
                  _const(wukv_t.shape), _const((B_NOPE, 1))],
        out_specs=[pl.BlockSpec((None, N_HEADS, tm, 128), lambda bi, i: (bi, 0, i, 0)),
                   pl.BlockSpec((None, N_HEADS * B_VDIM, tm), lambda bi, i: (bi, 0, i))],
        out_shape=[jax.ShapeDtypeStruct((b, N_HEADS, l, 128), bf16),
                   jax.ShapeDtypeStruct((b, N_HEADS * B_VDIM, l), bf16)],
        compiler_params=_params("arbitrary", "arbitrary"),
        name="b_kv_up",
    )(ckv, kr, wukv_t, g_kn)


def _b_item_table(n_tiles):
    rows = [(t, j, int(j == t)) for t in range(n_tiles) for j in range(t, -1, -1)]
    return np.asarray(rows, np.int32).T.copy()


def _b_items_per_step(n_items):
    return max(u for u in range(1, B_ITEMS_PER_STEP + 1)
               if n_items % u == 0 and n_items // u >= 3 and (n_items // u) % 2 == 1)


def _b_attn_body(seq, per, tab_ref, k_ref, qT_ref, vT_ref, o_ref,
                 s0, s1, c0, c1, pv0, pv1, a0, a1, m_st, acc_st):
    t_sz = Q_GROUP
    n_tiles = seq // t_sz
    n_steps = n_tiles * (n_tiles + 1) // 2 // per
    s_buf, c_buf, pv_buf, a_buf = (s0, s1), (c0, c1), (pv0, pv1), (a0, a1)
    ones = jnp.ones((ONES_ROWS, t_sz), bf16)
    key_chunk = lax.broadcasted_iota(jnp.int32, (128 - B_QK, t_sz), 0)
    qry_chunk = lax.broadcasted_iota(jnp.int32, (128 - B_QK, t_sz), 1) // CHUNK
    mask_rows = jnp.where((key_chunk < t_sz // CHUNK) & (key_chunk > qry_chunk), NEG_INF, 0.0).astype(bf16)
    zero_rows = jnp.zeros((128 - B_QK, t_sz), bf16)

    def item(step, u):
        n = step * per + u
        return tab_ref[0, n], tab_ref[1, n], tab_ref[2, n] == 1

    def scores(step, par, u):
        t, j, diag = item(step, u)
        q0 = pl.multiple_of(t * t_sz, t_sz)
        k0 = pl.multiple_of(j * t_sz, t_sz)
        pad = jnp.where(diag, mask_rows, zero_rows)
        for hh in range(2):
            w = jnp.concatenate([qT_ref[B_QK * hh:B_QK * (hh + 1), pl.ds(q0, t_sz)], pad], axis=0)
            s = _dot(k_ref[hh, pl.ds(k0, t_sz), :], w)
            s_buf[par][u, hh] = s
            c_buf[par][u, hh] = jnp.max(s, axis=0, keepdims=True)

    def values(step, par, u):
        t, j, diag = item(step, u)
        k0 = pl.multiple_of(j * t_sz, t_sz)
        for hh in range(2):
            m_old = jnp.where(diag, NEG_INF, m_st[t, hh])
            m_new = jnp.maximum(m_old, c_buf[par][u, hh])
            m_st[t, hh] = m_new
            a_buf[par][u, hh] = jnp.exp2(m_old - m_new)
            p = jnp.exp2(s_buf[par][u, hh] - m_new).astype(bf16)
            v_ext = jnp.concatenate([vT_ref[B_VDIM * hh:B_VDIM * (hh + 1), pl.ds(k0, t_sz)], ones], axis=0)
            pv_buf[par][u, hh] = _dot(v_ext, p)

    def accumulate(step, par, u):
        t, _, diag = item(step, u)
        for hh in range(2):
            pv = pv_buf[par][u, hh]
            acc_st[t, hh] = jnp.where(diag, pv, a_buf[par][u, hh] * acc_st[t, hh] + pv)

    def time_step(tau, par, first=False, drain=0):
        for u in range(per):
            if drain == 0:
                scores(tau + 1, 1 - par, u)
            if drain <= 1:
                values(tau, par, u)
            if not first:
                accumulate(tau - 1, 1 - par, u)

    for u in range(per):
        scores(0, 0, u)
    time_step(0, 0, first=True)

    def body(i, carry):
        tau = 2 * i + 1
        time_step(tau, 1)
        time_step(tau + 1, 0)
        return carry

    lax.fori_loop(0, (n_steps - 3) // 2, body, 0)
    time_step(n_steps - 2, 1)
    time_step(n_steps - 1, 0, drain=1)
    time_step(n_steps, 1, drain=2)
    for t in range(n_tiles):
        outs = [_normalise(acc_st[t, hh], B_VDIM) for hh in range(2)]
        o_ref[t * t_sz:(t + 1) * t_sz, :] = jnp.concatenate(outs, axis=0).T.astype(bf16)


def _b_attention(k_nat, q_t, v_t):
    b, _, l, _ = k_nat.shape
    t_sz = Q_GROUP
    n_tiles = l // t_sz
    table = _b_item_table(n_tiles)
    per = _b_items_per_step(table.shape[1])
    ext = B_VDIM + ONES_ROWS
    scratch = ([pltpu.VMEM((per, 2, t_sz, t_sz), f32)] * 2 + [pltpu.VMEM((per, 2, 1, t_sz), f32)] * 2
               + [pltpu.VMEM((per, 2, ext, t_sz), f32)] * 2 + [pltpu.VMEM((per, 2, 1, t_sz), f32)] * 2
               + [pltpu.VMEM((n_tiles, 2, 1, t_sz), f32), pltpu.VMEM((n_tiles, 2, ext, t_sz), f32)])
    grid_spec = pltpu.PrefetchScalarGridSpec(
        num_scalar_prefetch=1,
        grid=(b, N_HEADS // 2),
        in_specs=[pl.BlockSpec((None, 2, l, 128), lambda bi, p, tab: (bi, p, 0, 0)),
                  pl.BlockSpec((None, 2 * B_QK, l), lambda bi, p, tab: (bi, p, 0)),
                  pl.BlockSpec((None, 2 * B_VDIM, l), lambda bi, p, tab: (bi, p, 0))],
        out_specs=pl.BlockSpec((None, l, 2 * B_VDIM), lambda bi, p, tab: (bi, 0, p)),
        scratch_shapes=scratch)
    return pl.pallas_call(
        functools.partial(_b_attn_body, l, per),
        grid_spec=grid_spec,
        out_shape=jax.ShapeDtypeStruct((b, l, N_HEADS * B_VDIM), bf16),
        compiler_params=_params("arbitrary", "arbitrary"),
        name="b_attention",
    )(jnp.asarray(table), k_nat, q_t, v_t)


def _b_sattn_body(n_new, kc_ref, kn_ref, vc_ref, vn_ref, qT_ref, o_ref):
    t = SAMPLE_PAD
    zpad = jnp.zeros((128 - B_QK, t), bf16)
    new_mask = jnp.where(lax.broadcasted_iota(jnp.int32, (t, t), 0) < n_new, 0.0, NEG_INF).astype(f32)
    outs = []
    for hh in range(2):
        w = jnp.concatenate([qT_ref[B_QK * hh:B_QK * (hh + 1), :], zpad], axis=0)
        s_c = _dot(kc_ref[hh], w)
        s_n = _dot(kn_ref[hh], w) + new_mask
        m = jnp.maximum(jnp.max(s_c, axis=0, keepdims=True), jnp.max(s_n, axis=0, keepdims=True))
        p_c = jnp.exp2(s_c - m)
        p_n = jnp.exp2(s_n - m)
        l = jnp.sum(p_c, axis=0, keepdims=True) + jnp.sum(p_n, axis=0, keepdims=True)
        rows = slice(B_VDIM * hh, B_VDIM * (hh + 1))
        o_t = _dot(vc_ref[rows, :], p_c.astype(bf16)) + _dot(vn_ref[rows, :], p_n.astype(bf16))
        outs.append(o_t * (1.0 / l))
    o_ref[...] = jnp.concatenate(outs, axis=0).T.astype(bf16)


def _b_sample_attention(k_cache, k_new, v_cache, v_new, q_t, n_new):
    b, _, past, _ = k_cache.shape
    t = SAMPLE_PAD
    return pl.pallas_call(
        functools.partial(_b_sattn_body, n_new),
        grid=(b, N_HEADS // 2),
        in_specs=[pl.BlockSpec((None, 2, past, 128), lambda bi, p: (bi, p, 0, 0)),
                  pl.BlockSpec((None, 2, t, 128), lambda bi, p: (bi, p, 0, 0)),
                  pl.BlockSpec((None, 2 * B_VDIM, past), lambda bi, p: (bi, p, 0)),
                  pl.BlockSpec((None, 2 * B_VDIM, t), lambda bi, p: (bi, p, 0)),
                  pl.BlockSpec((None, 2 * B_QK, t), lambda bi, p: (bi, p, 0))],
        out_specs=pl.BlockSpec((None, t, 2 * B_VDIM), lambda bi, p: (bi, 0, p)),
        out_shape=jax.ShapeDtypeStruct((b, t, N_HEADS * B_VDIM), bf16),
        compiler_params=_params("arbitrary", "arbitrary"),
        name="b_sample_attention",
    )(k_cache, k_new, v_cache, v_new, q_t)


def _bias_rows(table):
    left = A_BIAS_ROWS - 1 - WINDOW - REL_CLIP
    right = A_BIAS_SPAN - left - table.shape[-1]
    return (jnp.pad(table.astype(f32), ((0, 0), (left, right)), mode="edge") * LOG2E)[:, None, :]


def _rope_tables(pos):
    half = B_ROPE // 2
    inv = ROPE_THETA ** (-jnp.arange(half, dtype=f32) / half)
    ang = pos.astype(f32)[:, None] * inv[None, :]
    return jnp.cos(ang).T, jnp.sin(ang).T


def _col(g, scale=1.0):
    return (g.astype(f32) * scale)[:, None]


def kernel(x_prompt, x_sample, cache_a_k, cache_a_v, cache_mla_ckv, cache_mla_krope, c_prompt, c_sample, norm_g, ada_w, ada_b, a_w_in, a_g_q, a_g_k, a_rel_bias, a_w_out, b_w_in, b_g_cq, b_w_uq, b_g_ckv, b_w_ukv, b_g_qn, b_g_qr, b_g_kn, b_g_kr, b_w_out):
    bp, seq, d = x_prompt.shape
    bs, dec, _ = x_sample.shape
    past = cache_mla_ckv.shape[2]
    n_cache_a = cache_a_k.shape[2]
    assert d == N_HEADS * A_HEAD_DIM and seq % (2 * Q_GROUP) == 0 and seq >= A_KEYS
    assert past % CHUNK == 0 and dec <= CHUNK and dec <= SAMPLE_PAD and n_cache_a == WINDOW
    cache_rows = min(WINDOW, seq)
    tm = min(ROW_TILE, seq)

    mod = _modulation(jnp.concatenate([c_prompt, c_sample], axis=0), ada_w, ada_b)
    mod = mod.reshape(mod.shape[0], bp + bs, 3, d)
    xs_pad = jnp.pad(x_sample, ((0, 0), (0, SAMPLE_PAD - dec), (0, 0)))

    w_in = a_w_in[0]
    wq_t, wk_t, wv_t = (w_in[:, d * n:d * (n + 1)].T.astype(bf16) for n in range(3))
    wz = w_in[:, 3 * d:].astype(bf16)
    w_out = a_w_out[0].astype(bf16)
    gq = _col(a_g_q[0], A_HEAD_DIM ** -0.5 * LOG2E)
    gk = _col(a_g_k[0])
    g0 = norm_g[0][None, :]
    f_rows = _bias_rows(a_rel_bias[0])
    mod_p, mod_s = mod[0, :bp], mod[0, bp:]

    q_t, k_nat, v_t, zs, k32, v32 = _a_project(x_prompt, mod_p, g0, wq_t, wk_t, wv_t, wz, gq, gk, tm, cache_rows)
    o = _a_attention(f_rows, k_nat, q_t, v_t)
    y_p = _out_project(o, zs, w_out, x_prompt, mod_p, tm)
    new_a_k_p = k32.reshape(1, bp, cache_rows, N_HEADS, A_HEAD_DIM)
    new_a_v_p = v32.reshape(1, bp, cache_rows, N_HEADS, A_HEAD_DIM)

    q_t, k_nat, v_t, zs, k32, v32 = _a_project(xs_pad, mod_s, g0, wq_t, wk_t, wv_t, wz, gq, gk,
                                               SAMPLE_PAD, SAMPLE_PAD)
    o = _a_sample_attention(f_rows, cache_a_k[0].reshape(bs, n_cache_a, d), k_nat,
                            cache_a_v[0].reshape(bs, n_cache_a, d), v32, q_t, dec)
    ys_pad = _out_project(o, zs, w_out, xs_pad, mod_s, SAMPLE_PAD)
    new_a_k_s = k32[:, :dec].reshape(1, bs, dec, N_HEADS, A_HEAD_DIM)
    new_a_v_s = v32[:, :dec].reshape(1, bs, dec, N_HEADS, A_HEAD_DIM)

    w_in = b_w_in[0]
    scale = B_QK ** -0.5 * LOG2E
    wb = {
        "cq_t": w_in[:, :Q_LORA].T.astype(bf16),
        "ckv_t": w_in[:, Q_LORA:Q_LORA + KV_LORA].T.astype(bf16),
        "kr_t": w_in[:, Q_LORA + KV_LORA:Q_LORA + KV_LORA + B_ROPE].T.astype(bf16),
        "z": w_in[:, Q_LORA + KV_LORA + B_ROPE:].astype(bf16),
        "uq_t": b_w_uq[0].T.astype(bf16),
        "g_cq": _col(b_g_cq[0]), "g_ckv": _col(b_g_ckv[0]), "g_kr": _col(b_g_kr[0]),
        "g_qn": _col(b_g_qn[0], scale), "g_qr": _col(b_g_qr[0], scale),
    }
    wukv_t = b_w_ukv[0].T.astype(bf16)
    g_kn = _col(b_g_kn[0])
    w_out = b_w_out[0].astype(bf16)
    g1 = norm_g[1][None, :]
    mod_p, mod_s = mod[1, :bp], mod[1, bp:]

    cos_t, sin_t = _rope_tables(jnp.arange(seq, dtype=jnp.int32))
    q_t, ckv_p, kr_p, zs = _b_project(y_p, mod_p, g1, wb, cos_t, sin_t, min(B_PROJ_TILE, seq))
    k_nat, v_t = _b_kv_up(ckv_p, kr_p, wukv_t, g_kn, tm)
    o = _b_attention(k_nat, q_t, v_t)
    y_p = _out_project(o, zs, w_out, y_p, mod_p, tm)

    cos_t, sin_t = _rope_tables(past + jnp.arange(SAMPLE_PAD, dtype=jnp.int32))
    q_t, ckv_s, kr_s, zs = _b_project(ys_pad, mod_s, g1, wb, cos_t, sin_t, SAMPLE_PAD)
    k_new, v_new = _b_kv_up(ckv_s, kr_s, wukv_t, g_kn, SAMPLE_PAD)
    k_old, v_old = _b_kv_up(cache_mla_ckv[0], cache_mla_krope[0], wukv_t, g_kn, tm)
    o = _b_sample_attention(k_old, k_new, v_old, v_new, q_t, dec)
    ys_pad = _out_project(o, zs, w_out, ys_pad, mod_s, SAMPLE_PAD)

    return (y_p, ys_pad[:, :dec], new_a_k_p, new_a_v_p, new_a_k_s, new_a_v_s,
            ckv_p[None], kr_p[None], ckv_s[None, :, :dec], kr_s[None, :, :dec])
```

```python
import functools

import numpy as np

import jax
import jax.numpy as jnp
from jax import lax
from jax.experimental import pallas as pl
from jax.experimental.pallas import tpu as pltpu

f32 = jnp.float32
bf16 = jnp.bfloat16

EPS = 1e-6
NEG_INF = -1e30
CHUNK = 64
BAND_CHUNKS = 8
WINDOW = BAND_CHUNKS * CHUNK
REL_CLIP = 128
ROPE_THETA = 10000.0
LOG2E = 1.4426950408889634

LANES = 128
N_HEADS = 16
N_PAIRS = N_HEADS // 2
A_HEAD_DIM = 64
B_NOPE = 64
B_ROPE = 32
B_VDIM = 64
B_QK = B_NOPE + B_ROPE
KV_LORA = 256
Q_LORA = 384

Q_GROUP = 256
A_KEYS = WINDOW + Q_GROUP
A_BIAS_ROWS = A_KEYS + WINDOW
A_BIAS_SPAN = A_BIAS_ROWS + Q_GROUP
SAMPLE_PAD = 128
ROW_TILE = 512
B_PROJ_TILE = 256
B_ITEMS_PER_STEP = 4
ONES_ROWS = 8
VMEM_LIMIT = 56 * 1024 * 1024


def _silu(x):
    return x * (1.0 / (1.0 + jnp.exp(-x)))


def _nt(a, b):
    return lax.dot_general(a, b, (((1,), (1,)), ((), ())), preferred_element_type=f32)


def _dot(a, b):
    return jnp.dot(a, b, preferred_element_type=f32)


def _rms_rows(blk, gain_col, n):
    ms = jnp.sum(blk * blk, axis=0, keepdims=True) * (1.0 / n)
    return blk * lax.rsqrt(ms + EPS) * gain_col


def _ada_hidden(x_ref, mod_ref, g_ref):
    x = x_ref[...]
    ms = jnp.mean(x * x, axis=-1, keepdims=True)
    xn = x * lax.rsqrt(ms + EPS) * g_ref[...]
    return (xn * (1.0 + mod_ref[1:2, :]) + mod_ref[0:1, :]).astype(bf16)


def _softmax_cols(s):
    m = jnp.max(s, axis=0, keepdims=True)
    p = jnp.exp2(s - m)
    return p, jnp.sum(p, axis=0, keepdims=True)


def _store_lane_tiles(ref, val):
    for p in range(ref.shape[0]):
        ref[p] = val[:, LANES * p:LANES * (p + 1)]


def _load_lane_tiles(ref):
    return jnp.concatenate([ref[p] for p in range(ref.shape[0])], axis=1)


def _gated(o_t, zs_ref, rows):
    return (o_t.T * zs_ref[rows, :].astype(f32)).astype(bf16)


def _normalise(o_ext, rows):
    return o_ext[:rows] * (1.0 / o_ext[rows:rows + 1])


def _bias_tile(f_row, n_rows, n_cols):
    blocks = []
    for a in range(pl.cdiv(n_rows, Q_GROUP)):
        lo = A_BIAS_ROWS - Q_GROUP * (a + 1)
        x = jnp.broadcast_to(f_row[:, lo:lo + 2 * Q_GROUP], (Q_GROUP, 2 * Q_GROUP))
        y = pltpu.roll(x, Q_GROUP + 1, 1, stride=1, stride_axis=0)[:, :n_cols]
        r = lax.broadcasted_iota(jnp.int32, (Q_GROUP, n_cols), 0) + Q_GROUP * a
        i = lax.broadcasted_iota(jnp.int32, (Q_GROUP, n_cols), 1)
        back = BAND_CHUNKS + i // CHUNK - r // CHUNK
        blocks.append(jnp.where((back >= 0) & (back <= BAND_CHUNKS), y, NEG_INF))
    return jnp.concatenate(blocks, axis=0)[:n_rows]


def _params(*sem):
    return pltpu.CompilerParams(dimension_semantics=sem, vmem_limit_bytes=VMEM_LIMIT)


def _const(shape):
    nd = len(shape)
    return pl.BlockSpec(shape, lambda *_: (0,) * nd)


def _pair_tiles(tm):
    return pl.BlockSpec((None, N_PAIRS, tm, LANES), lambda bi, i: (bi, 0, i, 0))


def _weight(shape):
    nd = len(shape)
    return pl.BlockSpec(shape, lambda *_: (0,) * nd, pipeline_mode=pl.Buffered(1))


def _mod_body(c_ref, w_ref, b_ref, o_ref):
    o_ref[...] = jnp.dot(_silu(c_ref[...]), w_ref[...], preferred_element_type=f32,
                         precision=lax.Precision.HIGHEST) + b_ref[...]


def _modulation(c_all, ada_w, ada_b):
    depth, d, n3 = ada_w.shape
    rows = c_all.shape[0]
    tn = d
    return pl.pallas_call(
        _mod_body,
        grid=(depth, n3 // tn),
        in_specs=[pl.BlockSpec((rows, d), lambda l, j: (0, 0)),
                  pl.BlockSpec((None, d, tn), lambda l, j: (l, 0, j)),
                  pl.BlockSpec((None, 1, tn), lambda l, j: (l, 0, j))],
        out_specs=pl.BlockSpec((None, rows, tn), lambda l, j: (l, 0, j)),
        out_shape=jax.ShapeDtypeStruct((depth, rows, n3), f32),
        compiler_params=_params("arbitrary", "arbitrary"),
        name="modulation",
    )(c_all, ada_w, ada_b.reshape(depth, 1, n3))


def _a_proj_body(n_skip, x_ref, mod_ref, g_ref, wq_ref, wk_ref, wv_ref, wz_ref, gq_ref, gk_ref,
                 qT_ref, k_ref, vT_ref, zs_ref, k32_ref, v32_ref, kf_ref):
    hb = _ada_hidden(x_ref, mod_ref, g_ref)
    z = _dot(hb, wz_ref[...])
    kT = _nt(wk_ref[...], hb)
    _store_lane_tiles(zs_ref, _silu(z).astype(bf16))
    qT = _nt(wq_ref[...], hb)
    gq = gq_ref[...]
    gk = gk_ref[...]
    hd = A_HEAD_DIM
    pairs = [slice(2 * hd * p, 2 * hd * (p + 1)) for p in range(N_HEADS // 2)]
    for p, rows in enumerate(pairs):
        k_pair = [_rms_rows(kT[hd * h:hd * (h + 1)], gk, hd) for h in (2 * p, 2 * p + 1)]
        k_nat = jnp.concatenate(k_pair, axis=0).T
        k_ref[p] = k_nat.astype(bf16)
        kf_ref[:, rows] = k_nat
    vT = _nt(wv_ref[...], hb)
    for p, rows in enumerate(pairs):
        q_pair = [_rms_rows(qT[hd * h:hd * (h + 1)], gq, hd) for h in (2 * p, 2 * p + 1)]
        qT_ref[rows, :] = jnp.concatenate(q_pair, axis=0).astype(bf16)
    vT_ref[...] = vT.astype(bf16)

    @pl.when(pl.program_id(1) >= n_skip)
    def _():
        k32_ref[...] = kf_ref[...]
        v32_ref[...] = vT.T


def _a_project(x, mod, g, wq_t, wk_t, wv_t, wz, gq, gk, tm, cache_rows):
    b, l, d = x.shape
    n_skip = (l - cache_rows) // tm
    tile = pl.BlockSpec((None, tm, d), lambda bi, i: (bi, i, 0))
    tile_t = pl.BlockSpec((None, d, tm), lambda bi, i: (bi, 0, i))
    cache = pl.BlockSpec((None, tm, d), lambda bi, i: (bi, jnp.maximum(i - n_skip, 0), 0))
    return pl.pallas_call(
        functools.partial(_a_proj_body, n_skip),
        grid=(b, l // tm),
        in_specs=[tile, pl.BlockSpec((None, 3, d), lambda bi, i: (bi, 0, 0)), _const((1, d)),
                  _weight((d, d)), _weight((d, d)), _weight((d, d)), _weight((d, d)),
                  _const((A_HEAD_DIM, 1)), _const((A_HEAD_DIM, 1))],
        out_specs=[tile_t, _pair_tiles(tm), tile_t, _pair_tiles(tm), cache, cache],
        out_shape=[jax.ShapeDtypeStruct((b, d, l), bf16), jax.ShapeDtypeStruct((b, N_PAIRS, l, LANES), bf16),
                   jax.ShapeDtypeStruct((b, d, l), bf16), jax.ShapeDtypeStruct((b, N_PAIRS, l, LANES), bf16),
                   jax.ShapeDtypeStruct((b, cache_rows, d), f32),
                   jax.ShapeDtypeStruct((b, cache_rows, d), f32)],
        scratch_shapes=[pltpu.VMEM((tm, d), f32)],
        compiler_params=_params("arbitrary", "arbitrary"),
        name="a_project",
    )(x, mod, g, wq_t, wk_t, wv_t, wz, gq, gk)


def _a_attn_body(seq, f_ref, k_ref, qT_ref, vT_ref, zs_ref, o_ref, e_ref, s0, s1, m0, m1, o0, o1):
    hd = A_HEAD_DIM
    n_groups = seq // Q_GROUP
    s_buf, m_buf, o_buf = (s0, s1), (m0, m1), (o0, o1)

    @pl.when(pl.program_id(1) == 0)
    def _():
        for hh in range(2):
            e_ref[hh] = _bias_tile(f_ref[hh], A_BIAS_ROWS, Q_GROUP)

    zeros = jnp.zeros((hd, Q_GROUP), bf16)
    ones = jnp.ones((ONES_ROWS, A_KEYS), bf16)

    def offsets(g):
        q0 = pl.multiple_of(g * Q_GROUP, Q_GROUP)
        w0 = pl.multiple_of(jnp.maximum(g * Q_GROUP - WINDOW, 0), Q_GROUP)
        e0 = pl.multiple_of(WINDOW - (g * Q_GROUP - w0), Q_GROUP)
        return q0, w0, e0

    def scores(g, par):
        q0, w0, e0 = offsets(g)
        k_win = k_ref[pl.ds(w0, A_KEYS), :]
        for hh in range(2):
            qh = qT_ref[hd * hh:hd * (hh + 1), pl.ds(q0, Q_GROUP)]
            w = jnp.concatenate([qh, zeros] if hh == 0 else [zeros, qh], axis=0)
            s = _dot(k_win, w) + e_ref[hh, pl.ds(e0, A_KEYS), :]
            s_buf[par][hh] = s
            m_buf[par][hh] = jnp.max(s, axis=0, keepdims=True)

    def values(g, par):
        _, w0, _ = offsets(g)
        for hh in range(2):
            p = jnp.exp2(s_buf[par][hh] - m_buf[par][hh]).astype(bf16)
            v_ext = jnp.concatenate([vT_ref[hd * hh:hd * (hh + 1), pl.ds(w0, A_KEYS)], ones], axis=0)
            o_buf[par][hh] = _dot(v_ext, p)

    def finish(g, par):
        q0, _, _ = offsets(g)
        outs = [_normalise(o_buf[par][hh], hd) for hh in range(2)]
        rows = pl.ds(q0, Q_GROUP)
        o_ref[rows, :] = _gated(jnp.concatenate(outs, axis=0), zs_ref, rows)

    scores(0, 0)
    scores(1, 1)
    values(0, 0)

    def body(i, carry):
        g = 2 * i + 1
        scores(g + 1, 0)
        values(g, 1)
        finish(g - 1, 0)
        scores(g + 2, 1)
        values(g + 1, 0)
        finish(g, 1)
        return carry

    lax.fori_loop(0, (n_groups - 2) // 2, body, 0)
    values(n_groups - 1, 1)
    finish(n_groups - 2, 0)
    finish(n_groups - 1, 1)


def _a_attention(f_rows, k_nat, q_t, v_t, zs):
    b, n_pairs, l, pair = k_nat.shape
    slab = pl.BlockSpec((None, None, l, pair), lambda p, bi: (bi, p, 0, 0))
    scratch = ([pltpu.VMEM((2, A_BIAS_ROWS, Q_GROUP), f32)]
               + [pltpu.VMEM((2, A_KEYS, Q_GROUP), f32)] * 2
               + [pltpu.VMEM((2, 1, Q_GROUP), f32)] * 2
               + [pltpu.VMEM((2, A_HEAD_DIM + ONES_ROWS, Q_GROUP), f32)] * 2)
    return pl.pallas_call(
        functools.partial(_a_attn_body, l),
        grid=(n_pairs, b),
        in_specs=[pl.BlockSpec((2, 1, A_BIAS_SPAN), lambda p, bi: (p, 0, 0)),
                  slab,
                  pl.BlockSpec((None, pair, l), lambda p, bi: (bi, p, 0)),
                  pl.BlockSpec((None, pair, l), lambda p, bi: (bi, p, 0)),
                  slab],
        out_specs=slab,
        out_shape=jax.ShapeDtypeStruct((b, n_pairs, l, pair), bf16),
        scratch_shapes=scratch,
        compiler_params=_params("arbitrary", "arbitrary"),
        name="a_attention",
    )(f_rows, k_nat, q_t, v_t, zs)


def _a_sattn_body(n_valid, f_ref, kc_ref, kn_ref, vc_ref, vn_ref, qT_ref, zs_ref, o_ref, e_ref):
    hd = A_HEAD_DIM
    n_keys = e_ref.shape[1]

    @pl.when(pl.program_id(1) == 0)
    def _():
        live = lax.broadcasted_iota(jnp.int32, (n_keys, SAMPLE_PAD), 0) < n_valid
        for hh in range(2):
            e_ref[hh] = jnp.where(live, _bias_tile(f_ref[hh], n_keys, SAMPLE_PAD), NEG_INF)

    zeros = jnp.zeros((hd, SAMPLE_PAD), bf16)
    k_all = jnp.concatenate([kc_ref[...].astype(bf16), kn_ref[...]], axis=0)
    v_t = jnp.concatenate([vc_ref[...], vn_ref[...]], axis=0).T.astype(bf16)
    outs = []
    for hh in range(2):
        qh = qT_ref[hd * hh:hd * (hh + 1), :]
        w = jnp.concatenate([qh, zeros] if hh == 0 else [zeros, qh], axis=0)
        p, l = _softmax_cols(_dot(k_all, w) + e_ref[hh])
        outs.append(_dot(v_t[hd * hh:hd * (hh + 1)], p.astype(bf16)) * (1.0 / l))
    o_ref[...] = _gated(jnp.concatenate(outs, axis=0), zs_ref, slice(None))


def _a_sample_attention(f_rows, k_cache, k_new, v_cache, v_new, q_t, zs, n_new):
    b, n_cache, d = k_cache.shape
    pair = 2 * A_HEAD_DIM
    n_keys = n_cache + SAMPLE_PAD
    new = pl.BlockSpec((None, SAMPLE_PAD, pair), lambda p, bi: (bi, 0, p))
    old = pl.BlockSpec((None, n_cache, pair), lambda p, bi: (bi, 0, p))
    slab = pl.BlockSpec((None, None, SAMPLE_PAD, pair), lambda p, bi: (bi, p, 0, 0))
    return pl.pallas_call(
        functools.partial(_a_sattn_body, n_cache + n_new),
        grid=(d // pair, b),
        in_specs=[pl.BlockSpec((2, 1, A_BIAS_SPAN), lambda p, bi: (p, 0, 0)),
                  old, slab, old, new,
                  pl.BlockSpec((None, pair, SAMPLE_PAD), lambda p, bi: (bi, p, 0)),
                  slab],
        out_specs=slab,
        out_shape=jax.ShapeDtypeStruct((b, d // pair, SAMPLE_PAD, pair), bf16),
        scratch_shapes=[pltpu.VMEM((2, n_keys, SAMPLE_PAD), f32)],
        compiler_params=_params("arbitrary", "arbitrary"),
        name="a_sample_attention",
    )(f_rows, k_cache, k_new, v_cache, v_new, q_t, zs)


def _out_body(oz_ref, w_ref, x_ref, mod_ref, y_ref):
    y = _dot(_load_lane_tiles(oz_ref), w_ref[...])
    y_ref[...] = x_ref[...] + mod_ref[2:3, :] * y


def _out_project(oz, w_out, x, mod, tm):
    b, l, d = x.shape
    tile = pl.BlockSpec((None, tm, d), lambda bi, i: (bi, i, 0))
    return pl.pallas_call(
        _out_body,
        grid=(b, l // tm),
        in_specs=[_pair_tiles(tm), _weight(w_out.shape), tile,
                  pl.BlockSpec((None, 3, d), lambda bi, i: (bi, 0, 0))],
        out_specs=tile,
        out_shape=jax.ShapeDtypeStruct((b, l, d), f32),
        compiler_params=_params("arbitrary", "arbitrary"),
        name="out_project",
    )(oz, w_out, x, mod)


def _rotate(x, cos, sin):
    half = B_ROPE // 2
    x1, x2 = x[:half], x[half:]
    return jnp.concatenate([x1 * cos - x2 * sin, x2 * cos + x1 * sin], axis=0)


def _b_proj_body(x_ref, mod_ref, g_ref, wcq_ref, wckv_ref, wkr_ref, wz_ref, wuq_ref,
                 gcq_ref, gckv_ref, gkr_ref, gqn_ref, gqr_ref, cos_ref, sin_ref,
                 qT_ref, ckv_ref, kr_ref, zs_ref):
    hb = _ada_hidden(x_ref, mod_ref, g_ref)
    tm = hb.shape[0]
    cos = cos_ref[...]
    sin = sin_ref[...]
    cq_raw = _nt(wcq_ref[...], hb)
    z = _dot(hb, wz_ref[...])
    ckv_raw = _nt(wckv_ref[...], hb)
    kr_raw = _nt(wkr_ref[...], hb)
    cq = _rms_rows(cq_raw, gcq_ref[...], Q_LORA).astype(bf16)
    qT = _dot(wuq_ref[...], cq)
    _store_lane_tiles(zs_ref, _silu(z).astype(bf16))
    ckv_ref[...] = _rms_rows(ckv_raw, gckv_ref[...], KV_LORA).T
    kr = _rotate(_rms_rows(kr_raw, gkr_ref[...], B_ROPE), cos, sin)
    kr_pad = jnp.concatenate([kr, jnp.zeros((128 - B_ROPE, tm), f32)], axis=0)
    kr_ref[...] = kr_pad.T[:, :B_ROPE]
    gqn = gqn_ref[...]
    gqr = gqr_ref[...]
    for h in range(N_HEADS):
        base = B_QK * h
        nope = _rms_rows(qT[base:base + B_NOPE], gqn, B_NOPE)
        rope = _rotate(_rms_rows(qT[base + B_NOPE:base + B_QK], gqr, B_ROPE), cos, sin)
        qT_ref[base:base + B_QK, :] = jnp.concatenate([nope, rope], axis=0).astype(bf16)


def _b_project(x, mod, g, w, cos_t, sin_t, tm):
    b, l, d = x.shape
    tile = pl.BlockSpec((None, tm, d), lambda bi, i: (bi, i, 0))
    rot = pl.BlockSpec((B_ROPE // 2, tm), lambda bi, i: (0, i))
    nq = N_HEADS * B_QK
    return pl.pallas_call(
        _b_proj_body,
        grid=(b, l // tm),
        in_specs=[tile, pl.BlockSpec((None, 3, d), lambda bi, i: (bi, 0, 0)), _const((1, d)),
                  _weight((Q_LORA, d)), _weight((KV_LORA, d)), _weight((B_ROPE, d)), _weight((d, d)),
                  _weight((nq, Q_LORA)),
                  _const((Q_LORA, 1)), _const((KV_LORA, 1)), _const((B_ROPE, 1)),
                  _const((B_NOPE, 1)), _const((B_ROPE, 1)), rot, rot],
        out_specs=[pl.BlockSpec((None, nq, tm), lambda bi, i: (bi, 0, i)),
                   pl.BlockSpec((None, tm, KV_LORA), lambda bi, i: (bi, i, 0)),
                   pl.BlockSpec((None, tm, B_ROPE), lambda bi, i: (bi, i, 0)),
                   _pair_tiles(tm)],
        out_shape=[jax.ShapeDtypeStruct((b, nq, l), bf16),
                   jax.ShapeDtypeStruct((b, l, KV_LORA), f32),
                   jax.ShapeDtypeStruct((b, l, B_ROPE), f32),
                   jax.ShapeDtypeStruct((b, N_PAIRS, l, LANES), bf16)],
        compiler_params=_params("arbitrary", "arbitrary"),
        name="b_project",
    )(x, mod, g, w["cq_t"], w["ckv_t"], w["kr_t"], w["z"], w["uq_t"],
      w["g_cq"], w["g_ckv"], w["g_kr"], w["g_qn"], w["g_qr"], cos_t, sin_t)


def _b_kvup_body(ckv_ref, kr_ref, wukv_ref, gkn_ref, k_ref, vT_ref):
    tm = ckv_ref.shape[0]
    kvT = _nt(wukv_ref[...], ckv_ref[...].astype(bf16))
    eye = (lax.broadcasted_iota(jnp.int32, (B_ROPE, B_ROPE), 0)
           == lax.broadcasted_iota(jnp.int32, (B_ROPE, B_ROPE), 1)).astype(bf16)
    krT = _nt(eye, kr_ref[...].astype(bf16))
    n_tile_chunks = Q_GROUP // CHUNK
    key_chunk = (lax.broadcasted_iota(jnp.int32, (128 - B_QK, tm), 1) // CHUNK) % n_tile_chunks
    chunk_rows = (lax.broadcasted_iota(jnp.int32, (128 - B_QK, tm), 0) == key_chunk).astype(f32)
    tail = jnp.concatenate([krT, chunk_rows], axis=0)
    gkn = gkn_ref[...]
    per_head = B_NOPE + B_VDIM
    for h in range(N_HEADS):
        base = per_head * h
        kn = _rms_rows(kvT[base:base + B_NOPE], gkn, B_NOPE)
        k_ref[h] = jnp.concatenate([kn, tail], axis=0).T.astype(bf16)
        vT_ref[B_VDIM * h:B_VDIM * (h + 1), :] = kvT[base + B_NOPE:base + per_head].astype(bf16)


def _b_kv_up(ckv, kr, wukv_t, g_kn, tm):
    b, l, _ = ckv.shape
    return pl.pallas_call(
        _b_kvup_body,
        grid=(b, l // tm),
        in_specs=[pl.BlockSpec((None, tm, KV_LORA), lambda bi, i: (bi, i, 0)),
                  pl.BlockSpec((None, tm, B_ROPE), lambda bi, i: (bi, i, 0)),
                  _const(wukv_t.shape), _const((B_NOPE, 1))],
        out_specs=[pl.BlockSpec((None, N_HEADS, tm, 128), lambda bi, i: (bi, 0, i, 0)),
                   pl.BlockSpec((None, N_HEADS * B_VDIM, tm), lambda bi, i: (bi, 0, i))],
        out_shape=[jax.ShapeDtypeStruct((b, N_HEADS, l, 128), bf16),
                   jax.ShapeDtypeStruct((b, N_HEADS * B_VDIM, l), bf16)],
        compiler_params=_params("arbitrary", "arbitrary"),
        name="b_kv_up",
    )(ckv, kr, wukv_t, g_kn)


def _b_item_table(n_tiles):
    rows = [(t, j, int(j == t)) for t in range(n_tiles) for j in range(t, -1, -1)]
    return np.asarray(rows, np.int32).T.copy()


def _b_items_per_step(n_items):
    return max(u for u in range(1, B_ITEMS_PER_STEP + 1)
               if n_items % u == 0 and n_items // u >= 3 and (n_items // u) % 2 == 1)


def _b_attn_body(seq, per, tab_ref, k_ref, qT_ref, vT_ref, zs_ref, o_ref,
                 s0, s1, c0, c1, pv0, pv1, a0, a1, m_st, acc_st):
    t_sz = Q_GROUP
    n_tiles = seq // t_sz
    n_steps = n_tiles * (n_tiles + 1) // 2 // per
    s_buf, c_buf, pv_buf, a_buf = (s0, s1), (c0, c1), (pv0, pv1), (a0, a1)
    ones = jnp.ones((ONES_ROWS, t_sz), bf16)
    key_chunk = lax.broadcasted_iota(jnp.int32, (128 - B_QK, t_sz), 0)
    qry_chunk = lax.broadcasted_iota(jnp.int32, (128 - B_QK, t_sz), 1) // CHUNK
    mask_rows = jnp.where((key_chunk < t_sz // CHUNK) & (key_chunk > qry_chunk), NEG_INF, 0.0).astype(bf16)
    zero_rows = jnp.zeros((128 - B_QK, t_sz), bf16)
    m_st[...] = jnp.full(m_st.shape, NEG_INF, f32)
    acc_st[...] = jnp.zeros(acc_st.shape, f32)

    def item(step, u):
        n = step * per + u
        return tab_ref[0, n], tab_ref[1, n], tab_ref[2, n] == 1

    def scores(step, par, u):
        t, j, diag = item(step, u)
        q0 = pl.multiple_of(t * t_sz, t_sz)
        k0 = pl.multiple_of(j * t_sz, t_sz)
        pad = jnp.where(diag, mask_rows, zero_rows)
        for hh in range(2):
            w = jnp.concatenate([qT_ref[B_QK * hh:B_QK * (hh + 1), pl.ds(q0, t_sz)], pad], axis=0)
            s = _dot(k_ref[hh, pl.ds(k0, t_sz), :], w)
            s_buf[par][u, hh] = s
            c_buf[par][u, hh] = jnp.max(s, axis=0, keepdims=True)

    def values(step, par, u):
        t, j, _ = item(step, u)
        k0 = pl.multiple_of(j * t_sz, t_sz)
        for hh in range(2):
            m_old = m_st[t, hh]
            m_new = jnp.maximum(m_old, c_buf[par][u, hh])
            m_st[t, hh] = m_new
            a_buf[par][u, hh] = jnp.exp2(m_old - m_new)
            p = jnp.exp2(s_buf[par][u, hh] - m_new).astype(bf16)
            v_ext = jnp.concatenate([vT_ref[B_VDIM * hh:B_VDIM * (hh + 1), pl.ds(k0, t_sz)], ones], axis=0)
            pv_buf[par][u, hh] = _dot(v_ext, p)

    def accumulate(step, par, u):
        t, _, _ = item(step, u)
        for hh in range(2):
            acc_st[t, hh] = a_buf[par][u, hh] * acc_st[t, hh] + pv_buf[par][u, hh]

    def time_step(tau, par, first=False, drain=0):
        for u in range(per):
            if drain == 0:
                scores(tau + 1, 1 - par, u)
            if drain <= 1:
                values(tau, par, u)
            if not first:
                accumulate(tau - 1, 1 - par, u)

    for u in range(per):
        scores(0, 0, u)
    time_step(0, 0, first=True)

    def body(i, carry):
        tau = 2 * i + 1
        time_step(tau, 1)
        time_step(tau + 1, 0)
        return carry

    lax.fori_loop(0, (n_steps - 3) // 2, body, 0)
    time_step(n_steps - 2, 1)
    time_step(n_steps - 1, 0, drain=1)
    time_step(n_steps, 1, drain=2)
    for t in range(n_tiles):
        outs = [_normalise(acc_st[t, hh], B_VDIM) for hh in range(2)]
        rows = slice(t * t_sz, (t + 1) * t_sz)
        o_ref[rows, :] = _gated(jnp.concatenate(outs, axis=0), zs_ref, rows)


def _b_attention(k_nat, q_t, v_t, zs):
    b, _, l, _ = k_nat.shape
    t_sz = Q_GROUP
    n_tiles = l // t_sz
    table = _b_item_table(n_tiles)
    per = _b_items_per_step(table.shape[1])
    ext = B_VDIM + ONES_ROWS
    scratch = ([pltpu.VMEM((per, 2, t_sz, t_sz), f32)] * 2 + [pltpu.VMEM((per, 2, 1, t_sz), f32)] * 2
               + [pltpu.VMEM((per, 2, ext, t_sz), f32)] * 2 + [pltpu.VMEM((per, 2, 1, t_sz), f32)] * 2
               + [pltpu.VMEM((n_tiles, 2, 1, t_sz), f32), pltpu.VMEM((n_tiles, 2, ext, t_sz), f32)])
    grid_spec = pltpu.PrefetchScalarGridSpec(
        num_scalar_prefetch=1,
        grid=(b, N_HEADS // 2),
        in_specs=[pl.BlockSpec((None, 2, l, 128), lambda bi, p, tab: (bi, p, 0, 0)),
                  pl.BlockSpec((None, 2 * B_QK, l), lambda bi, p, tab: (bi, p, 0)),
                  pl.BlockSpec((None, 2 * B_VDIM, l), lambda bi, p, tab: (bi, p, 0)),
                  pl.BlockSpec((None, None, l, LANES), lambda bi, p, tab: (bi, p, 0, 0))],
        out_specs=pl.BlockSpec((None, None, l, LANES), lambda bi, p, tab: (bi, p, 0, 0)),
        scratch_shapes=scratch)
    return pl.pallas_call(
        functools.partial(_b_attn_body, l, per),
        grid_spec=grid_spec,
        out_shape=jax.ShapeDtypeStruct((b, N_PAIRS, l, LANES), bf16),
        compiler_params=_params("arbitrary", "arbitrary"),
        name="b_attention",
    )(jnp.asarray(table), k_nat, q_t, v_t, zs)


def _b_sattn_body(n_new, kc_ref, kn_ref, vc_ref, vn_ref, qT_ref, zs_ref, o_ref):
    t = SAMPLE_PAD
    zpad = jnp.zeros((128 - B_QK, t), bf16)
    new_mask = jnp.where(lax.broadcasted_iota(jnp.int32, (t, t), 0) < n_new, 0.0, NEG_INF).astype(f32)
    outs = []
    for hh in range(2):
        w = jnp.concatenate([qT_ref[B_QK * hh:B_QK * (hh + 1), :], zpad], axis=0)
        s_c = _dot(kc_ref[hh], w)
        s_n = _dot(kn_ref[hh], w) + new_mask
        m = jnp.maximum(jnp.max(s_c, axis=0, keepdims=True), jnp.max(s_n, axis=0, keepdims=True))
        p_c = jnp.exp2(s_c - m)
        p_n = jnp.exp2(s_n - m)
        l = jnp.sum(p_c, axis=0, keepdims=True) + jnp.sum(p_n, axis=0, keepdims=True)
        rows = slice(B_VDIM * hh, B_VDIM * (hh + 1))
        o_t = _dot(vc_ref[rows, :], p_c.astype(bf16)) + _dot(vn_ref[rows, :], p_n.astype(bf16))
        outs.append(o_t * (1.0 / l))
    o_ref[...] = _gated(jnp.concatenate(outs, axis=0), zs_ref, slice(None))


def _b_sample_attention(k_cache, k_new, v_cache, v_new, q_t, zs, n_new):
    b, _, past, _ = k_cache.shape
    t = SAMPLE_PAD
    slab = pl.BlockSpec((None, None, t, LANES), lambda bi, p: (bi, p, 0, 0))
    return pl.pallas_call(
        functools.partial(_b_sattn_body, n_new),
        grid=(b, N_HEADS // 2),
        in_specs=[pl.BlockSpec((None, 2, past, 128), lambda bi, p: (bi, p, 0, 0)),
                  pl.BlockSpec((None, 2, t, 128), lambda bi, p: (bi, p, 0, 0)),
                  pl.BlockSpec((None, 2 * B_VDIM, past), lambda bi, p: (bi, p, 0)),
                  pl.BlockSpec((None, 2 * B_VDIM, t), lambda bi, p: (bi, p, 0)),
                  pl.BlockSpec((None, 2 * B_QK, t), lambda bi, p: (bi, p, 0)),
                  slab],
        out_specs=slab,
        out_shape=jax.ShapeDtypeStruct((b, N_PAIRS, t, LANES), bf16),
        compiler_params=_params("arbitrary", "arbitrary"),
        name="b_sample_attention",
    )(k_cache, k_new, v_cache, v_new, q_t, zs)


def _bias_rows(table):
    left = A_BIAS_ROWS - 1 - WINDOW - REL_CLIP
    right = A_BIAS_SPAN - left - table.shape[-1]
    return (jnp.pad(table.astype(f32), ((0, 0), (left, right)), mode="edge") * LOG2E)[:, None, :]


def _rope_tables(pos):
    half = B_ROPE // 2
    inv = ROPE_THETA ** (-jnp.arange(half, dtype=f32) / half)
    ang = pos.astype(f32)[:, None] * inv[None, :]
    return jnp.cos(ang).T, jnp.sin(ang).T


def _col(g, scale=1.0):
    return (g.astype(f32) * scale)[:, None]


def kernel(x_prompt, x_sample, cache_a_k, cache_a_v, cache_mla_ckv, cache_mla_krope, c_prompt, c_sample, norm_g, ada_w, ada_b, a_w_in, a_g_q, a_g_k, a_rel_bias, a_w_out, b_w_in, b_g_cq, b_w_uq, b_g_ckv, b_w_ukv, b_g_qn, b_g_qr, b_g_kn, b_g_kr, b_w_out):
    bp, seq, d = x_prompt.shape
    bs, dec, _ = x_sample.shape
    past = cache_mla_ckv.shape[2]
    n_cache_a = cache_a_k.shape[2]
    assert d == N_HEADS * A_HEAD_DIM and seq % (2 * Q_GROUP) == 0 and seq >= A_KEYS
    assert past % CHUNK == 0 and dec <= CHUNK and dec <= SAMPLE_PAD and n_cache_a == WINDOW
    cache_rows = min(WINDOW, seq)
    tm = min(ROW_TILE, seq)

    mod = _modulation(jnp.concatenate([c_prompt, c_sample], axis=0), ada_w, ada_b)
    mod = mod.reshape(mod.shape[0], bp + bs, 3, d)
    xs_pad = jnp.pad(x_sample, ((0, 0), (0, SAMPLE_PAD - dec), (0, 0)))

    w_in = a_w_in[0]
    wq_t, wk_t, wv_t = (w_in[:, d * n:d * (n + 1)].T.astype(bf16) for n in range(3))
    wz = w_in[:, 3 * d:].astype(bf16)
    w_out = a_w_out[0].astype(bf16)
    gq = _col(a_g_q[0], A_HEAD_DIM ** -0.5 * LOG2E)
    gk = _col(a_g_k[0])
    g0 = norm_g[0][None, :]
    f_rows = _bias_rows(a_rel_bias[0])
    mod_p, mod_s = mod[0, :bp], mod[0, bp:]

    q_t, k_nat, v_t, zs, k32, v32 = _a_project(x_prompt, mod_p, g0, wq_t, wk_t, wv_t, wz, gq, gk, tm, cache_rows)
    oz = _a_attention(f_rows, k_nat, q_t, v_t, zs)
    y_p = _out_project(oz, w_out, x_prompt, mod_p, tm)
    new_a_k_p = k32.reshape(1, bp, cache_rows, N_HEADS, A_HEAD_DIM)
    new_a_v_p = v32.reshape(1, bp, cache_rows, N_HEADS, A_HEAD_DIM)

    q_t, k_nat, v_t, zs, k32, v32 = _a_project(xs_pad, mod_s, g0, wq_t, wk_t, wv_t, wz, gq, gk,
                                               SAMPLE_PAD, SAMPLE_PAD)
    oz = _a_sample_attention(f_rows, cache_a_k[0].reshape(bs, n_cache_a, d), k_nat,
                             cache_a_v[0].reshape(bs, n_cache_a, d), v32, q_t, zs, dec)
    ys_pad = _out_project(oz, w_out, xs_pad, mod_s, SAMPLE_PAD)
    new_a_k_s = k32[:, :dec].reshape(1, bs, dec, N_HEADS, A_HEAD_DIM)
    new_a_v_s = v32[:, :dec].reshape(1, bs, dec, N_HEADS, A_HEAD_DIM)

    w_in = b_w_in[0]
    scale = B_QK ** -0.5 * LOG2E
    wb = {
        "cq_t": w_in[:, :Q_LORA].T.astype(bf16),
        "ckv_t": w_in[:, Q_LORA:Q_LORA + KV_LORA].T.astype(bf16),
        "kr_t": w_in[:, Q_LORA + KV_LORA:Q_LORA + KV_LORA + B_ROPE].T.astype(bf16),
        "z": w_in[:, Q_LORA + KV_LORA + B_ROPE:].astype(bf16),
        "uq_t": b_w_uq[0].T.astype(bf16),
        "g_cq": _col(b_g_cq[0]), "g_ckv": _col(b_g_ckv[0]), "g_kr": _col(b_g_kr[0]),
        "g_qn": _col(b_g_qn[0], scale), "g_qr": _col(b_g_qr[0], scale),
    }
    wukv_t = b_w_ukv[0].T.astype(bf16)
    g_kn = _col(b_g_kn[0])
    w_out = b_w_out[0].astype(bf16)
    g1 = norm_g[1][None, :]
    mod_p, mod_s = mod[1, :bp], mod[1, bp:]

    cos_t, sin_t = _rope_tables(jnp.arange(seq, dtype=jnp.int32))
    q_t, ckv_p, kr_p, zs = _b_project(y_p, mod_p, g1, wb, cos_t, sin_t, min(B_PROJ_TILE, seq))
    k_nat, v_t = _b_kv_up(ckv_p, kr_p, wukv_t, g_kn, tm)
    oz = _b_attention(k_nat, q_t, v_t, zs)
    y_p = _out_project(oz, w_out, y_p, mod_p, tm)

    cos_t, sin_t = _rope_tables(past + jnp.arange(SAMPLE_PAD, dtype=jnp.int32))
    q_t, ckv_s, kr_s, zs = _b_project(ys_pad, mod_s, g1, wb, cos_t, sin_t, SAMPLE_PAD)
    k_new, v_new = _b_kv_up(ckv_s, kr_s, wukv_t, g_kn, SAMPLE_PAD)
    k_old, v_old = _b_kv_up(cache_mla_ckv[0], cache_mla_krope[0], wukv_t, g_kn, tm)
    oz = _b_sample_attention(k_old, k_new, v_old, v_new, q_t, zs, dec)
    ys_pad = _out_project(oz, w_out, ys_pad, mod_s, SAMPLE_PAD)

    return (y_p, ys_pad[:, :dec], new_a_k_p, new_a_v_p, new_a_k_s, new_a_v_s,
            ckv_p[None], kr_p[None], ckv_s[None, :, :dec], kr_s[None, :, :dec])
```

```python
import functools

import numpy as np

import jax
import jax.numpy as jnp
from jax import lax
from jax.experimental import pallas as pl
from jax.experimental.pallas import tpu as pltpu

f32 = jnp.float32
bf16 = jnp.bfloat16

EPS = 1e-6
NEG_INF = -1e30
CHUNK = 64
BAND_CHUNKS = 8
WINDOW = BAND_CHUNKS * CHUNK
REL_CLIP = 128
ROPE_THETA = 10000.0
LOG2E = 1.4426950408889634

LANES = 128
N_HEADS = 16
N_PAIRS = N_HEADS // 2
A_HEAD_DIM = 64
B_NOPE = 64
B_ROPE = 32
B_VDIM = 64
B_QK = B_NOPE + B_ROPE
KV_LORA = 256
Q_LORA = 384

Q_GROUP = 256
TILE_CHUNKS = Q_GROUP // CHUNK
MAX_LOGIT_RANGE = 100.0
BOUND_MARGIN = 1.02
A_KEYS = WINDOW + Q_GROUP
A_BIAS_ROWS = A_KEYS + WINDOW
A_BIAS_SPAN = A_BIAS_ROWS + Q_GROUP
SAMPLE_PAD = 128
ROW_TILE = 512
B_PROJ_TILE = 256
B_ITEMS_PER_STEP = 4
ONES_ROWS = 8
VMEM_LIMIT = 56 * 1024 * 1024


def _silu(x):
    return x * (1.0 / (1.0 + jnp.exp(-x)))


def _nt(a, b):
    return lax.dot_general(a, b, (((1,), (1,)), ((), ())), preferred_element_type=f32)


def _dot(a, b):
    return jnp.dot(a, b, preferred_element_type=f32)


def _rms_rows(blk, gain_col, n):
    ms = jnp.sum(blk * blk, axis=0, keepdims=True) * (1.0 / n)
    return blk * lax.rsqrt(ms + EPS) * gain_col


def _ada_hidden(x_ref, mod_ref, g_ref):
    x = x_ref[...]
    ms = jnp.mean(x * x, axis=-1, keepdims=True)
    xn = x * lax.rsqrt(ms + EPS) * g_ref[...]
    return (xn * (1.0 + mod_ref[1:2, :]) + mod_ref[0:1, :]).astype(bf16)


def _softmax_cols(s):
    m = jnp.max(s, axis=0, keepdims=True)
    p = jnp.exp2(s - m)
    return p, jnp.sum(p, axis=0, keepdims=True)


def _store_lane_tiles(ref, val):
    for p in range(ref.shape[0]):
        ref[p] = val[:, LANES * p:LANES * (p + 1)]


def _load_lane_tiles(ref):
    return jnp.concatenate([ref[p] for p in range(ref.shape[0])], axis=1)


def _gated(o_t, zs_ref, rows):
    return (o_t.T * zs_ref[rows, :].astype(f32)).astype(bf16)


def _normalise(o_ext, rows):
    return o_ext[:rows] * (1.0 / o_ext[rows:rows + 1])


def _bias_tile(f_row, n_rows, n_cols):
    blocks = []
    for a in range(pl.cdiv(n_rows, Q_GROUP)):
        lo = A_BIAS_ROWS - Q_GROUP * (a + 1)
        x = jnp.broadcast_to(f_row[:, lo:lo + 2 * Q_GROUP], (Q_GROUP, 2 * Q_GROUP))
        y = pltpu.roll(x, Q_GROUP + 1, 1, stride=1, stride_axis=0)[:, :n_cols]
        r = lax.broadcasted_iota(jnp.int32, (Q_GROUP, n_cols), 0) + Q_GROUP * a
        i = lax.broadcasted_iota(jnp.int32, (Q_GROUP, n_cols), 1)
        back = BAND_CHUNKS + i // CHUNK - r // CHUNK
        blocks.append(jnp.where((back >= 0) & (back <= BAND_CHUNKS), y, NEG_INF))
    return jnp.concatenate(blocks, axis=0)[:n_rows]


def _params(*sem):
    return pltpu.CompilerParams(dimension_semantics=sem, vmem_limit_bytes=VMEM_LIMIT)


def _const(shape):
    nd = len(shape)
    return pl.BlockSpec(shape, lambda *_: (0,) * nd)


def _pair_tiles(tm):
    return pl.BlockSpec((None, N_PAIRS, tm, LANES), lambda bi, i: (bi, 0, i, 0))


def _weight(shape):
    nd = len(shape)
    return pl.BlockSpec(shape, lambda *_: (0,) * nd, pipeline_mode=pl.Buffered(1))


def _mod_body(c_ref, w_ref, b_ref, o_ref):
    o_ref[...] = jnp.dot(_silu(c_ref[...]), w_ref[...], preferred_element_type=f32,
                         precision=lax.Precision.HIGHEST) + b_ref[...]


def _modulation(c_all, ada_w, ada_b):
    depth, d, n3 = ada_w.shape
    rows = c_all.shape[0]
    tn = d
    return pl.pallas_call(
        _mod_body,
        grid=(depth, n3 // tn),
        in_specs=[pl.BlockSpec((rows, d), lambda l, j: (0, 0)),
                  pl.BlockSpec((None, d, tn), lambda l, j: (l, 0, j)),
                  pl.BlockSpec((None, 1, tn), lambda l, j: (l, 0, j))],
        out_specs=pl.BlockSpec((None, rows, tn), lambda l, j: (l, 0, j)),
        out_shape=jax.ShapeDtypeStruct((depth, rows, n3), f32),
        compiler_params=_params("arbitrary", "arbitrary"),
        name="modulation",
    )(c_all, ada_w, ada_b.reshape(depth, 1, n3))


def _a_proj_body(n_skip, x_ref, mod_ref, g_ref, wq_ref, wk_ref, wv_ref, wz_ref, gq_ref, gk_ref,
                 qT_ref, k_ref, vT_ref, zs_ref, k32_ref, v32_ref, kf_ref):
    hb = _ada_hidden(x_ref, mod_ref, g_ref)
    z = _dot(hb, wz_ref[...])
    kT = _nt(wk_ref[...], hb)
    _store_lane_tiles(zs_ref, _silu(z).astype(bf16))
    qT = _nt(wq_ref[...], hb)
    gq = gq_ref[...]
    gk = gk_ref[...]
    hd = A_HEAD_DIM
    pairs = [slice(2 * hd * p, 2 * hd * (p + 1)) for p in range(N_HEADS // 2)]
    for p, rows in enumerate(pairs):
        k_pair = [_rms_rows(kT[hd * h:hd * (h + 1)], gk, hd) for h in (2 * p, 2 * p + 1)]
        k_nat = jnp.concatenate(k_pair, axis=0).T
        k_ref[p] = k_nat.astype(bf16)
        kf_ref[:, rows] = k_nat
    vT = _nt(wv_ref[...], hb)
    for p, rows in enumerate(pairs):
        q_pair = [_rms_rows(qT[hd * h:hd * (h + 1)], gq, hd) for h in (2 * p, 2 * p + 1)]
        qT_ref[rows, :] = jnp.concatenate(q_pair, axis=0).astype(bf16)
    vT_ref[...] = vT.astype(bf16)

    @pl.when(pl.program_id(1) >= n_skip)
    def _():
        k32_ref[...] = kf_ref[...]
        v32_ref[...] = vT.T


def _a_project(x, mod, g, wq_t, wk_t, wv_t, wz, gq, gk, tm, cache_rows):
    b, l, d = x.shape
    n_skip = (l - cache_rows) // tm
    tile = pl.BlockSpec((None, tm, d), lambda bi, i: (bi, i, 0))
    tile_t = pl.BlockSpec((None, d, tm), lambda bi, i: (bi, 0, i))
    cache = pl.BlockSpec((None, tm, d), lambda bi, i: (bi, jnp.maximum(i - n_skip, 0), 0))
    return pl.pallas_call(
        functools.partial(_a_proj_body, n_skip),
        grid=(b, l // tm),
        in_specs=[tile, pl.BlockSpec((None, 3, d), lambda bi, i: (bi, 0, 0)), _const((1, d)),
                  _weight((d, d)), _weight((d, d)), _weight((d, d)), _weight((d, d)),
                  _const((A_HEAD_DIM, 1)), _const((A_HEAD_DIM, 1))],
        out_specs=[tile_t, _pair_tiles(tm), tile_t, _pair_tiles(tm), cache, cache],
        out_shape=[jax.ShapeDtypeStruct((b, d, l), bf16), jax.ShapeDtypeStruct((b, N_PAIRS, l, LANES), bf16),
                   jax.ShapeDtypeStruct((b, d, l), bf16), jax.ShapeDtypeStruct((b, N_PAIRS, l, LANES), bf16),
                   jax.ShapeDtypeStruct((b, cache_rows, d), f32),
                   jax.ShapeDtypeStruct((b, cache_rows, d), f32)],
        scratch_shapes=[pltpu.VMEM((tm, d), f32)],
        compiler_params=_params("arbitrary", "arbitrary"),
        name="a_project",
    )(x, mod, g, wq_t, wk_t, wv_t, wz, gq, gk)


def _a_attn_body(seq, f_ref, k_ref, qT_ref, vT_ref, zs_ref, o_ref, e_ref, s0, s1, m0, m1, o0, o1):
    hd = A_HEAD_DIM
    n_groups = seq // Q_GROUP
    s_buf, m_buf, o_buf = (s0, s1), (m0, m1), (o0, o1)

    @pl.when(pl.program_id(1) == 0)
    def _():
        for hh in range(2):
            e_ref[hh] = _bias_tile(f_ref[hh], A_BIAS_ROWS, Q_GROUP)

    zeros = jnp.zeros((hd, Q_GROUP), bf16)
    ones = jnp.ones((ONES_ROWS, A_KEYS), bf16)

    def offsets(g):
        q0 = pl.multiple_of(g * Q_GROUP, Q_GROUP)
        w0 = pl.multiple_of(jnp.maximum(g * Q_GROUP - WINDOW, 0), Q_GROUP)
        e0 = pl.multiple_of(WINDOW - (g * Q_GROUP - w0), Q_GROUP)
        return q0, w0, e0

    def scores(g, par):
        q0, w0, e0 = offsets(g)
        k_win = k_ref[pl.ds(w0, A_KEYS), :]
        for hh in range(2):
            qh = qT_ref[hd * hh:hd * (hh + 1), pl.ds(q0, Q_GROUP)]
            w = jnp.concatenate([qh, zeros] if hh == 0 else [zeros, qh], axis=0)
            s = _dot(k_win, w) + e_ref[hh, pl.ds(e0, A_KEYS), :]
            s_buf[par][hh] = s
            m_buf[par][hh] = jnp.max(s, axis=0, keepdims=True)

    def values(g, par):
        _, w0, _ = offsets(g)
        for hh in range(2):
            p = jnp.exp2(s_buf[par][hh] - m_buf[par][hh]).astype(bf16)
            v_ext = jnp.concatenate([vT_ref[hd * hh:hd * (hh + 1), pl.ds(w0, A_KEYS)], ones], axis=0)
            o_buf[par][hh] = _dot(v_ext, p)

    def finish(g, par):
        q0, _, _ = offsets(g)
        outs = [_normalise(o_buf[par][hh], hd) for hh in range(2)]
        rows = pl.ds(q0, Q_GROUP)
        o_ref[rows, :] = _gated(jnp.concatenate(outs, axis=0), zs_ref, rows)

    scores(0, 0)
    scores(1, 1)
    values(0, 0)

    def body(i, carry):
        g = 2 * i + 1
        scores(g + 1, 0)
        values(g, 1)
        finish(g - 1, 0)
        scores(g + 2, 1)
        values(g + 1, 0)
        finish(g, 1)
        return carry

    lax.fori_loop(0, (n_groups - 2) // 2, body, 0)
    values(n_groups - 1, 1)
    finish(n_groups - 2, 0)
    finish(n_groups - 1, 1)


def _a_attention(f_rows, k_nat, q_t, v_t, zs):
    b, n_pairs, l, pair = k_nat.shape
    slab = pl.BlockSpec((None, None, l, pair), lambda p, bi: (bi, p, 0, 0))
    scratch = ([pltpu.VMEM((2, A_BIAS_ROWS, Q_GROUP), f32)]
               + [pltpu.VMEM((2, A_KEYS, Q_GROUP), f32)] * 2
               + [pltpu.VMEM((2, 1, Q_GROUP), f32)] * 2
               + [pltpu.VMEM((2, A_HEAD_DIM + ONES_ROWS, Q_GROUP), f32)] * 2)
    return pl.pallas_call(
        functools.partial(_a_attn_body, l),
        grid=(n_pairs, b),
        in_specs=[pl.BlockSpec((2, 1, A_BIAS_SPAN), lambda p, bi: (p, 0, 0)),
                  slab,
                  pl.BlockSpec((None, pair, l), lambda p, bi: (bi, p, 0)),
                  pl.BlockSpec((None, pair, l), lambda p, bi: (bi, p, 0)),
                  slab],
        out_specs=slab,
        out_shape=jax.ShapeDtypeStruct((b, n_pairs, l, pair), bf16),
        scratch_shapes=scratch,
        compiler_params=_params("arbitrary", "arbitrary"),
        name="a_attention",
    )(f_rows, k_nat, q_t, v_t, zs)


def _a_attn_bounded_body(seq, shift_ref, f_ref, k_ref, qT_ref, vT_ref, zs_ref, o_ref, e_ref, p0, p1, o0, o1):
    hd = A_HEAD_DIM
    n_groups = seq // Q_GROUP
    p_buf, o_buf = (p0, p1), (o0, o1)

    @pl.when(pl.program_id(1) == 0)
    def _():
        for hh in range(2):
            e_ref[hh] = _bias_tile(f_ref[hh], A_BIAS_ROWS, Q_GROUP) - shift_ref[0]

    zeros = jnp.zeros((hd, Q_GROUP), bf16)
    ones = jnp.ones((ONES_ROWS, A_KEYS), bf16)

    def offsets(g):
        q0 = pl.multiple_of(g * Q_GROUP, Q_GROUP)
        w0 = pl.multiple_of(jnp.maximum(g * Q_GROUP - WINDOW, 0), Q_GROUP)
        e0 = pl.multiple_of(WINDOW - (g * Q_GROUP - w0), Q_GROUP)
        return q0, w0, e0

    def probs(g, par):
        q0, w0, e0 = offsets(g)
        k_win = k_ref[pl.ds(w0, A_KEYS), :]
        for hh in range(2):
            qh = qT_ref[hd * hh:hd * (hh + 1), pl.ds(q0, Q_GROUP)]
            w = jnp.concatenate([qh, zeros] if hh == 0 else [zeros, qh], axis=0)
            p_buf[par][hh] = jnp.exp2(_dot(k_win, w) + e_ref[hh, pl.ds(e0, A_KEYS), :]).astype(bf16)

    def values(g, par):
        _, w0, _ = offsets(g)
        for hh in range(2):
            v_ext = jnp.concatenate([vT_ref[hd * hh:hd * (hh + 1), pl.ds(w0, A_KEYS)], ones], axis=0)
            o_buf[par][hh] = _dot(v_ext, p_buf[par][hh])

    def finish(g, par):
        q0, _, _ = offsets(g)
        outs = [_normalise(o_buf[par][hh], hd) for hh in range(2)]
        rows = pl.ds(q0, Q_GROUP)
        o_ref[rows, :] = _gated(jnp.concatenate(outs, axis=0), zs_ref, rows)

    probs(0, 0)
    probs(1, 1)
    values(0, 0)

    def body(i, carry):
        g = 2 * i + 1
        probs(g + 1, 0)
        values(g, 1)
        finish(g - 1, 0)
        probs(g + 2, 1)
        values(g + 1, 0)
        finish(g, 1)
        return carry

    lax.fori_loop(0, (n_groups - 2) // 2, body, 0)
    values(n_groups - 1, 1)
    finish(n_groups - 2, 0)
    finish(n_groups - 1, 1)


def _a_attention_bounded(shift, f_rows, k_nat, q_t, v_t, zs):
    b, n_pairs, l, pair = k_nat.shape
    slab = pl.BlockSpec((None, None, l, pair), lambda p, bi, sh: (bi, p, 0, 0))
    scratch = ([pltpu.VMEM((2, A_BIAS_ROWS, Q_GROUP), f32)]
               + [pltpu.VMEM((2, A_KEYS, Q_GROUP), bf16)] * 2
               + [pltpu.VMEM((2, A_HEAD_DIM + ONES_ROWS, Q_GROUP), f32)] * 2)
    grid_spec = pltpu.PrefetchScalarGridSpec(
        num_scalar_prefetch=1,
        grid=(n_pairs, b),
        in_specs=[pl.BlockSpec((2, 1, A_BIAS_SPAN), lambda p, bi, sh: (p, 0, 0)),
                  slab,
                  pl.BlockSpec((None, pair, l), lambda p, bi, sh: (bi, p, 0)),
                  pl.BlockSpec((None, pair, l), lambda p, bi, sh: (bi, p, 0)),
                  slab],
        out_specs=slab,
        scratch_shapes=scratch)
    return pl.pallas_call(
        functools.partial(_a_attn_bounded_body, l),
        grid_spec=grid_spec,
        out_shape=jax.ShapeDtypeStruct((b, n_pairs, l, pair), bf16),
        compiler_params=_params("arbitrary", "arbitrary"),
        name="a_attention_bounded",
    )(shift, f_rows, k_nat, q_t, v_t, zs)


def _a_sattn_body(n_valid, f_ref, kc_ref, kn_ref, vc_ref, vn_ref, qT_ref, zs_ref, o_ref, e_ref):
    hd = A_HEAD_DIM
    n_keys = e_ref.shape[1]

    @pl.when(pl.program_id(1) == 0)
    def _():
        live = lax.broadcasted_iota(jnp.int32, (n_keys, SAMPLE_PAD), 0) < n_valid
        for hh in range(2):
            e_ref[hh] = jnp.where(live, _bias_tile(f_ref[hh], n_keys, SAMPLE_PAD), NEG_INF)

    zeros = jnp.zeros((hd, SAMPLE_PAD), bf16)
    k_all = jnp.concatenate([kc_ref[...].astype(bf16), kn_ref[...]], axis=0)
    v_t = jnp.concatenate([vc_ref[...], vn_ref[...]], axis=0).T.astype(bf16)
    outs = []
    for hh in range(2):
        qh = qT_ref[hd * hh:hd * (hh + 1), :]
        w = jnp.concatenate([qh, zeros] if hh == 0 else [zeros, qh], axis=0)
        p, l = _softmax_cols(_dot(k_all, w) + e_ref[hh])
        outs.append(_dot(v_t[hd * hh:hd * (hh + 1)], p.astype(bf16)) * (1.0 / l))
    o_ref[...] = _gated(jnp.concatenate(outs, axis=0), zs_ref, slice(None))


def _a_sample_attention(f_rows, k_cache, k_new, v_cache, v_new, q_t, zs, n_new):
    b, n_cache, d = k_cache.shape
    pair = 2 * A_HEAD_DIM
    n_keys = n_cache + SAMPLE_PAD
    new = pl.BlockSpec((None, SAMPLE_PAD, pair), lambda p, bi: (bi, 0, p))
    old = pl.BlockSpec((None, n_cache, pair), lambda p, bi: (bi, 0, p))
    slab = pl.BlockSpec((None, None, SAMPLE_PAD, pair), lambda p, bi: (bi, p, 0, 0))
    return pl.pallas_call(
        functools.partial(_a_sattn_body, n_cache + n_new),
        grid=(d // pair, b),
        in_specs=[pl.BlockSpec((2, 1, A_BIAS_SPAN), lambda p, bi: (p, 0, 0)),
                  old, slab, old, new,
                  pl.BlockSpec((None, pair, SAMPLE_PAD), lambda p, bi: (bi, p, 0)),
                  slab],
        out_specs=slab,
        out_shape=jax.ShapeDtypeStruct((b, d // pair, SAMPLE_PAD, pair), bf16),
        scratch_shapes=[pltpu.VMEM((2, n_keys, SAMPLE_PAD), f32)],
        compiler_params=_params("arbitrary", "arbitrary"),
        name="a_sample_attention",
    )(f_rows, k_cache, k_new, v_cache, v_new, q_t, zs)


def _out_body(oz_ref, w_ref, x_ref, mod_ref, y_ref):
    y = _dot(_load_lane_tiles(oz_ref), w_ref[...])
    y_ref[...] = x_ref[...] + mod_ref[2:3, :] * y


def _out_project(oz, w_out, x, mod, tm):
    b, l, d = x.shape
    tile = pl.BlockSpec((None, tm, d), lambda bi, i: (bi, i, 0))
    return pl.pallas_call(
        _out_body,
        grid=(b, l // tm),
        in_specs=[_pair_tiles(tm), _weight(w_out.shape), tile,
                  pl.BlockSpec((None, 3, d), lambda bi, i: (bi, 0, 0))],
        out_specs=tile,
        out_shape=jax.ShapeDtypeStruct((b, l, d), f32),
        compiler_params=_params("arbitrary", "arbitrary"),
        name="out_project",
    )(oz, w_out, x, mod)


def _rotate(x, cos, sin):
    half = B_ROPE // 2
    x1, x2 = x[:half], x[half:]
    return jnp.concatenate([x1 * cos - x2 * sin, x2 * cos + x1 * sin], axis=0)


def _b_proj_body(x_ref, mod_ref, g_ref, wcq_ref, wckv_ref, wkr_ref, wz_ref, wuq_ref,
                 gcq_ref, gckv_ref, gkr_ref, gqn_ref, gqr_ref, cos_ref, sin_ref,
                 qT_ref, ckv_ref, kr_ref, zs_ref):
    hb = _ada_hidden(x_ref, mod_ref, g_ref)
    tm = hb.shape[0]
    cos = cos_ref[...]
    sin = sin_ref[...]
    cq_raw = _nt(wcq_ref[...], hb)
    z = _dot(hb, wz_ref[...])
    ckv_raw = _nt(wckv_ref[...], hb)
    kr_raw = _nt(wkr_ref[...], hb)
    cq = _rms_rows(cq_raw, gcq_ref[...], Q_LORA).astype(bf16)
    qT = _dot(wuq_ref[...], cq)
    _store_lane_tiles(zs_ref, _silu(z).astype(bf16))
    ckv_ref[...] = _rms_rows(ckv_raw, gckv_ref[...], KV_LORA).T
    kr = _rotate(_rms_rows(kr_raw, gkr_ref[...], B_ROPE), cos, sin)
    kr_pad = jnp.concatenate([kr, jnp.zeros((128 - B_ROPE, tm), f32)], axis=0)
    kr_ref[...] = kr_pad.T[:, :B_ROPE]
    gqn = gqn_ref[...]
    gqr = gqr_ref[...]
    for h in range(N_HEADS):
        base = B_QK * h
        nope = _rms_rows(qT[base:base + B_NOPE], gqn, B_NOPE)
        rope = _rotate(_rms_rows(qT[base + B_NOPE:base + B_QK], gqr, B_ROPE), cos, sin)
        qT_ref[base:base + B_QK, :] = jnp.concatenate([nope, rope], axis=0).astype(bf16)


def _b_project(x, mod, g, w, cos_t, sin_t, tm):
    b, l, d = x.shape
    tile = pl.BlockSpec((None, tm, d), lambda bi, i: (bi, i, 0))
    rot = pl.BlockSpec((B_ROPE // 2, tm), lambda bi, i: (0, i))
    nq = N_HEADS * B_QK
    return pl.pallas_call(
        _b_proj_body,
        grid=(b, l // tm),
        in_specs=[tile, pl.BlockSpec((None, 3, d), lambda bi, i: (bi, 0, 0)), _const((1, d)),
                  _weight((Q_LORA, d)), _weight((KV_LORA, d)), _weight((B_ROPE, d)), _weight((d, d)),
                  _weight((nq, Q_LORA)),
                  _const((Q_LORA, 1)), _const((KV_LORA, 1)), _const((B_ROPE, 1)),
                  _const((B_NOPE, 1)), _const((B_ROPE, 1)), rot, rot],
        out_specs=[pl.BlockSpec((None, nq, tm), lambda bi, i: (bi, 0, i)),
                   pl.BlockSpec((None, tm, KV_LORA), lambda bi, i: (bi, i, 0)),
                   pl.BlockSpec((None, tm, B_ROPE), lambda bi, i: (bi, i, 0)),
                   _pair_tiles(tm)],
        out_shape=[jax.ShapeDtypeStruct((b, nq, l), bf16),
                   jax.ShapeDtypeStruct((b, l, KV_LORA), f32),
                   jax.ShapeDtypeStruct((b, l, B_ROPE), f32),
                   jax.ShapeDtypeStruct((b, N_PAIRS, l, LANES), bf16)],
        compiler_params=_params("arbitrary", "arbitrary"),
        name="b_project",
    )(x, mod, g, w["cq_t"], w["ckv_t"], w["kr_t"], w["z"], w["uq_t"],
      w["g_cq"], w["g_ckv"], w["g_kr"], w["g_qn"], w["g_qr"], cos_t, sin_t)


def _b_kvup_body(ckv_ref, kr_ref, wukv_ref, gkn_ref, k_ref, vT_ref):
    tm = ckv_ref.shape[0]
    kvT = _nt(wukv_ref[...], ckv_ref[...].astype(bf16))
    eye = (lax.broadcasted_iota(jnp.int32, (B_ROPE, B_ROPE), 0)
           == lax.broadcasted_iota(jnp.int32, (B_ROPE, B_ROPE), 1)).astype(bf16)
    krT = _nt(eye, kr_ref[...].astype(bf16))
    key_chunk = (lax.broadcasted_iota(jnp.int32, (128 - B_QK, tm), 1) // CHUNK) % TILE_CHUNKS
    spare_row = lax.broadcasted_iota(jnp.int32, (128 - B_QK, tm), 0)
    chunk_rows = ((spare_row == key_chunk) | (spare_row == TILE_CHUNKS)).astype(f32)
    tail = jnp.concatenate([krT, chunk_rows], axis=0)
    gkn = gkn_ref[...]
    per_head = B_NOPE + B_VDIM
    for h in range(N_HEADS):
        base = per_head * h
        kn = _rms_rows(kvT[base:base + B_NOPE], gkn, B_NOPE)
        k_ref[h] = jnp.concatenate([kn, tail], axis=0).T.astype(bf16)
        vT_ref[B_VDIM * h:B_VDIM * (h + 1), :] = kvT[base + B_NOPE:base + per_head].astype(bf16)


def _b_kv_up(ckv, kr, wukv_t, g_kn, tm):
    b, l, _ = ckv.shape
    return pl.pallas_call(
        _b_kvup_body,
        grid=(b, l // tm),
        in_specs=[pl.BlockSpec((None, tm, KV_LORA), lambda bi, i: (bi, i, 0)),
                  pl.BlockSpec((None, tm, B_ROPE), lambda bi, i: (bi, i, 0)),
                  _const(wukv_t.shape), _const((B_NOPE, 1))],
        out_specs=[pl.BlockSpec((None, N_HEADS, tm, 128), lambda bi, i: (bi, 0, i, 0)),
                   pl.BlockSpec((None, N_HEADS * B_VDIM, tm), lambda bi, i: (bi, 0, i))],
        out_shape=[jax.ShapeDtypeStruct((b, N_HEADS, l, 128), bf16),
                   jax.ShapeDtypeStruct((b, N_HEADS * B_VDIM, l), bf16)],
        compiler_params=_params("arbitrary", "arbitrary"),
        name="b_kv_up",
    )(ckv, kr, wukv_t, g_kn)


def _b_item_table(n_tiles):
    rows = [(t, j, int(j == t)) for t in range(n_tiles) for j in range(t, -1, -1)]
    return np.asarray(rows, np.int32).T.copy()


def _b_items_per_step(n_items):
    return max(u for u in range(1, B_ITEMS_PER_STEP + 1)
               if n_items % u == 0 and n_items // u >= 3 and (n_items // u) % 2 == 1)


def _b_attn_body(seq, per, tab_ref, k_ref, qT_ref, vT_ref, zs_ref, o_ref,
                 s0, s1, c0, c1, pv0, pv1, a0, a1, m_st, acc_st):
    t_sz = Q_GROUP
    n_tiles = seq // t_sz
    n_steps = n_tiles * (n_tiles + 1) // 2 // per
    s_buf, c_buf, pv_buf, a_buf = (s0, s1), (c0, c1), (pv0, pv1), (a0, a1)
    ones = jnp.ones((ONES_ROWS, t_sz), bf16)
    key_chunk = lax.broadcasted_iota(jnp.int32, (128 - B_QK, t_sz), 0)
    qry_chunk = lax.broadcasted_iota(jnp.int32, (128 - B_QK, t_sz), 1) // CHUNK
    mask_rows = jnp.where((key_chunk < t_sz // CHUNK) & (key_chunk > qry_chunk), NEG_INF, 0.0).astype(bf16)
    zero_rows = jnp.zeros((128 - B_QK, t_sz), bf16)
    m_st[...] = jnp.full(m_st.shape, NEG_INF, f32)
    acc_st[...] = jnp.zeros(acc_st.shape, f32)

    def item(step, u):
        n = step * per + u
        return tab_ref[0, n], tab_ref[1, n], tab_ref[2, n] == 1

    def scores(step, par, u):
        t, j, diag = item(step, u)
        q0 = pl.multiple_of(t * t_sz, t_sz)
        k0 = pl.multiple_of(j * t_sz, t_sz)
        pad = jnp.where(diag, mask_rows, zero_rows)
        for hh in range(2):
            w = jnp.concatenate([qT_ref[B_QK * hh:B_QK * (hh + 1), pl.ds(q0, t_sz)], pad], axis=0)
            s = _dot(k_ref[hh, pl.ds(k0, t_sz), :], w)
            s_buf[par][u, hh] = s
            c_buf[par][u, hh] = jnp.max(s, axis=0, keepdims=True)

    def values(step, par, u):
        t, j, _ = item(step, u)
        k0 = pl.multiple_of(j * t_sz, t_sz)
        for hh in range(2):
            m_old = m_st[t, hh]
            m_new = jnp.maximum(m_old, c_buf[par][u, hh])
            m_st[t, hh] = m_new
            a_buf[par][u, hh] = jnp.exp2(m_old - m_new)
            p = jnp.exp2(s_buf[par][u, hh] - m_new).astype(bf16)
            v_ext = jnp.concatenate([vT_ref[B_VDIM * hh:B_VDIM * (hh + 1), pl.ds(k0, t_sz)], ones], axis=0)
            pv_buf[par][u, hh] = _dot(v_ext, p)

    def accumulate(step, par, u):
        t, _, _ = item(step, u)
        for hh in range(2):
            acc_st[t, hh] = a_buf[par][u, hh] * acc_st[t, hh] + pv_buf[par][u, hh]

    def time_step(tau, par, first=False, drain=0):
        for u in range(per):
            if drain == 0:
                scores(tau + 1, 1 - par, u)
            if drain <= 1:
                values(tau, par, u)
            if not first:
                accumulate(tau - 1, 1 - par, u)

    for u in range(per):
        scores(0, 0, u)
    time_step(0, 0, first=True)

    def body(i, carry):
        tau = 2 * i + 1
        time_step(tau, 1)
        time_step(tau + 1, 0)
        return carry

    lax.fori_loop(0, (n_steps - 3) // 2, body, 0)
    time_step(n_steps - 2, 1)
    time_step(n_steps - 1, 0, drain=1)
    time_step(n_steps, 1, drain=2)
    for t in range(n_tiles):
        outs = [_normalise(acc_st[t, hh], B_VDIM) for hh in range(2)]
        rows = slice(t * t_sz, (t + 1) * t_sz)
        o_ref[rows, :] = _gated(jnp.concatenate(outs, axis=0), zs_ref, rows)


def _b_attention(k_nat, q_t, v_t, zs):
    b, _, l, _ = k_nat.shape
    t_sz = Q_GROUP
    n_tiles = l // t_sz
    table = _b_item_table(n_tiles)
    per = _b_items_per_step(table.shape[1])
    ext = B_VDIM + ONES_ROWS
    scratch = ([pltpu.VMEM((per, 2, t_sz, t_sz), f32)] * 2 + [pltpu.VMEM((per, 2, 1, t_sz), f32)] * 2
               + [pltpu.VMEM((per, 2, ext, t_sz), f32)] * 2 + [pltpu.VMEM((per, 2, 1, t_sz), f32)] * 2
               + [pltpu.VMEM((n_tiles, 2, 1, t_sz), f32), pltpu.VMEM((n_tiles, 2, ext, t_sz), f32)])
    grid_spec = pltpu.PrefetchScalarGridSpec(
        num_scalar_prefetch=1,
        grid=(b, N_HEADS // 2),
        in_specs=[pl.BlockSpec((None, 2, l, 128), lambda bi, p, tab: (bi, p, 0, 0)),
                  pl.BlockSpec((None, 2 * B_QK, l), lambda bi, p, tab: (bi, p, 0)),
                  pl.BlockSpec((None, 2 * B_VDIM, l), lambda bi, p, tab: (bi, p, 0)),
                  pl.BlockSpec((None, None, l, LANES), lambda bi, p, tab: (bi, p, 0, 0))],
        out_specs=pl.BlockSpec((None, None, l, LANES), lambda bi, p, tab: (bi, p, 0, 0)),
        scratch_shapes=scratch)
    return pl.pallas_call(
        functools.partial(_b_attn_body, l, per),
        grid_spec=grid_spec,
        out_shape=jax.ShapeDtypeStruct((b, N_PAIRS, l, LANES), bf16),
        compiler_params=_params("arbitrary", "arbitrary"),
        name="b_attention",
    )(jnp.asarray(table), k_nat, q_t, v_t, zs)


def _b_attn_bounded_body(seq, per, tab_ref, shift_ref, k_ref, qT_ref, vT_ref, zs_ref, o_ref, p0, p1, acc_st):
    t_sz = Q_GROUP
    n_tiles = seq // t_sz
    n_steps = n_tiles * (n_tiles + 1) // 2 // per
    p_buf = (p0, p1)
    ones = jnp.ones((ONES_ROWS, t_sz), bf16)
    row = lax.broadcasted_iota(jnp.int32, (128 - B_QK, t_sz), 0)
    qry_chunk = lax.broadcasted_iota(jnp.int32, (128 - B_QK, t_sz), 1) // CHUNK
    shift_rows = jnp.where(row == TILE_CHUNKS, -shift_ref[0], 0.0)
    mask_rows = (jnp.where((row < TILE_CHUNKS) & (row > qry_chunk), NEG_INF, 0.0) + shift_rows).astype(bf16)
    plain_rows = shift_rows.astype(bf16)
    acc_st[...] = jnp.zeros(acc_st.shape, f32)

    def item(step, u):
        n = step * per + u
        return tab_ref[0, n], tab_ref[1, n], tab_ref[2, n] == 1

    def probs(step, par, u):
        t, j, diag = item(step, u)
        q0 = pl.multiple_of(t * t_sz, t_sz)
        k0 = pl.multiple_of(j * t_sz, t_sz)
        pad = jnp.where(diag, mask_rows, plain_rows)
        for hh in range(2):
            w = jnp.concatenate([qT_ref[B_QK * hh:B_QK * (hh + 1), pl.ds(q0, t_sz)], pad], axis=0)
            p_buf[par][u, hh] = jnp.exp2(_dot(k_ref[hh, pl.ds(k0, t_sz), :], w)).astype(bf16)

    def values(step, par, u):
        t, j, _ = item(step, u)
        k0 = pl.multiple_of(j * t_sz, t_sz)
        for hh in range(2):
            v_ext = jnp.concatenate([vT_ref[B_VDIM * hh:B_VDIM * (hh + 1), pl.ds(k0, t_sz)], ones], axis=0)
            acc_st[t, hh] += _dot(v_ext, p_buf[par][u, hh])

    def time_step(tau, par, last=False):
        for u in range(per):
            if not last:
                probs(tau + 1, 1 - par, u)
            values(tau, par, u)

    for u in range(per):
        probs(0, 0, u)

    def body(i, carry):
        time_step(2 * i, 0)
        time_step(2 * i + 1, 1)
        return carry

    lax.fori_loop(0, (n_steps - 1) // 2, body, 0)
    time_step(n_steps - 1, 0, last=True)
    for t in range(n_tiles):
        outs = [_normalise(acc_st[t, hh], B_VDIM) for hh in range(2)]
        rows = slice(t * t_sz, (t + 1) * t_sz)
        o_ref[rows, :] = _gated(jnp.concatenate(outs, axis=0), zs_ref, rows)


def _b_attention_bounded(shift, k_nat, q_t, v_t, zs):
    b, _, l, _ = k_nat.shape
    t_sz = Q_GROUP
    n_tiles = l // t_sz
    table = _b_item_table(n_tiles)
    per = _b_items_per_step(table.shape[1])
    scratch = ([pltpu.VMEM((per, 2, t_sz, t_sz), bf16)] * 2
               + [pltpu.VMEM((n_tiles, 2, B_VDIM + ONES_ROWS, t_sz), f32)])
    slab = pl.BlockSpec((None, None, l, LANES), lambda bi, p, tab, sh: (bi, p, 0, 0))
    grid_spec = pltpu.PrefetchScalarGridSpec(
        num_scalar_prefetch=2,
        grid=(b, N_PAIRS),
        in_specs=[pl.BlockSpec((None, 2, l, 128), lambda bi, p, tab, sh: (bi, p, 0, 0)),
                  pl.BlockSpec((None, 2 * B_QK, l), lambda bi, p, tab, sh: (bi, p, 0)),
                  pl.BlockSpec((None, 2 * B_VDIM, l), lambda bi, p, tab, sh: (bi, p, 0)),
                  slab],
        out_specs=slab,
        scratch_shapes=scratch)
    return pl.pallas_call(
        functools.partial(_b_attn_bounded_body, l, per),
        grid_spec=grid_spec,
        out_shape=jax.ShapeDtypeStruct((b, N_PAIRS, l, LANES), bf16),
        compiler_params=_params("arbitrary", "arbitrary"),
        name="b_attention_bounded",
    )(jnp.asarray(table), shift, k_nat, q_t, v_t, zs)


def _b_sattn_body(n_new, kc_ref, kn_ref, vc_ref, vn_ref, qT_ref, zs_ref, o_ref):
    t = SAMPLE_PAD
    zpad = jnp.zeros((128 - B_QK, t), bf16)
    new_mask = jnp.where(lax.broadcasted_iota(jnp.int32, (t, t), 0) < n_new, 0.0, NEG_INF).astype(f32)
    outs = []
    for hh in range(2):
        w = jnp.concatenate([qT_ref[B_QK * hh:B_QK * (hh + 1), :], zpad], axis=0)
        s_c = _dot(kc_ref[hh], w)
        s_n = _dot(kn_ref[hh], w) + new_mask
        m = jnp.maximum(jnp.max(s_c, axis=0, keepdims=True), jnp.max(s_n, axis=0, keepdims=True))
        p_c = jnp.exp2(s_c - m)
        p_n = jnp.exp2(s_n - m)
        l = jnp.sum(p_c, axis=0, keepdims=True) + jnp.sum(p_n, axis=0, keepdims=True)
        rows = slice(B_VDIM * hh, B_VDIM * (hh + 1))
        o_t = _dot(vc_ref[rows, :], p_c.astype(bf16)) + _dot(vn_ref[rows, :], p_n.astype(bf16))
        outs.append(o_t * (1.0 / l))
    o_ref[...] = _gated(jnp.concatenate(outs, axis=0), zs_ref, slice(None))


def _b_sample_attention(k_cache, k_new, v_cache, v_new, q_t, zs, n_new):
    b, _, past, _ = k_cache.shape
    t = SAMPLE_PAD
    slab = pl.BlockSpec((None, None, t, LANES), lambda bi, p: (bi, p, 0, 0))
    return pl.pallas_call(
        functools.partial(_b_sattn_body, n_new),
        grid=(b, N_HEADS // 2),
        in_specs=[pl.BlockSpec((None, 2, past, 128), lambda bi, p: (bi, p, 0, 0)),
                  pl.BlockSpec((None, 2, t, 128), lambda bi, p: (bi, p, 0, 0)),
                  pl.BlockSpec((None, 2 * B_VDIM, past), lambda bi, p: (bi, p, 0)),
                  pl.BlockSpec((None, 2 * B_VDIM, t), lambda bi, p: (bi, p, 0)),
                  pl.BlockSpec((None, 2 * B_QK, t), lambda bi, p: (bi, p, 0)),
                  slab],
        out_specs=slab,
        out_shape=jax.ShapeDtypeStruct((b, N_PAIRS, t, LANES), bf16),
        compiler_params=_params("arbitrary", "arbitrary"),
        name="b_sample_attention",
    )(k_cache, k_new, v_cache, v_new, q_t, zs)


def _bias_rows(table):
    left = A_BIAS_ROWS - 1 - WINDOW - REL_CLIP
    right = A_BIAS_SPAN - left - table.shape[-1]
    return (jnp.pad(table.astype(f32), ((0, 0), (left, right)), mode="edge") * LOG2E)[:, None, :]


def _rope_tables(pos):
    half = B_ROPE // 2
    inv = ROPE_THETA ** (-jnp.arange(half, dtype=f32) / half)
    ang = pos.astype(f32)[:, None] * inv[None, :]
    return jnp.cos(ang).T, jnp.sin(ang).T


def _col(g, scale=1.0):
    return (g.astype(f32) * scale)[:, None]


def kernel(x_prompt, x_sample, cache_a_k, cache_a_v, cache_mla_ckv, cache_mla_krope, c_prompt, c_sample, norm_g, ada_w, ada_b, a_w_in, a_g_q, a_g_k, a_rel_bias, a_w_out, b_w_in, b_g_cq, b_w_uq, b_g_ckv, b_w_ukv, b_g_qn, b_g_qr, b_g_kn, b_g_kr, b_w_out):
    bp, seq, d = x_prompt.shape
    bs, dec, _ = x_sample.shape
    past = cache_mla_ckv.shape[2]
    n_cache_a = cache_a_k.shape[2]
    assert d == N_HEADS * A_HEAD_DIM and seq % (2 * Q_GROUP) == 0 and seq >= A_KEYS
    assert past % CHUNK == 0 and dec <= CHUNK and dec <= SAMPLE_PAD and n_cache_a == WINDOW
    cache_rows = min(WINDOW, seq)
    tm = min(ROW_TILE, seq)

    mod = _modulation(jnp.concatenate([c_prompt, c_sample], axis=0), ada_w, ada_b)
    mod = mod.reshape(mod.shape[0], bp + bs, 3, d)
    xs_pad = jnp.pad(x_sample, ((0, 0), (0, SAMPLE_PAD - dec), (0, 0)))

    w_in = a_w_in[0]
    wq_t, wk_t, wv_t = (w_in[:, d * n:d * (n + 1)].T.astype(bf16) for n in range(3))
    wz = w_in[:, 3 * d:].astype(bf16)
    w_out = a_w_out[0].astype(bf16)
    gq = _col(a_g_q[0], A_HEAD_DIM ** -0.5 * LOG2E)
    gk = _col(a_g_k[0])
    g0 = norm_g[0][None, :]
    f_rows = _bias_rows(a_rel_bias[0])
    mod_p, mod_s = mod[0, :bp], mod[0, bp:]

    q_t, k_nat, v_t, zs, k32, v32 = _a_project(x_prompt, mod_p, g0, wq_t, wk_t, wv_t, wz, gq, gk, tm, cache_rows)
    qk_max = A_HEAD_DIM * jnp.max(jnp.abs(gq)) * jnp.max(jnp.abs(gk)) * BOUND_MARGIN
    upper = qk_max + jnp.max(f_rows)
    oz = lax.cond(upper - (jnp.min(f_rows) - qk_max) <= MAX_LOGIT_RANGE,
                  lambda: _a_attention_bounded(upper[None], f_rows, k_nat, q_t, v_t, zs),
                  lambda: _a_attention(f_rows, k_nat, q_t, v_t, zs))
    y_p = _out_project(oz, w_out, x_prompt, mod_p, tm)
    new_a_k_p = k32.reshape(1, bp, cache_rows, N_HEADS, A_HEAD_DIM)
    new_a_v_p = v32.reshape(1, bp, cache_rows, N_HEADS, A_HEAD_DIM)

    q_t, k_nat, v_t, zs, k32, v32 = _a_project(xs_pad, mod_s, g0, wq_t, wk_t, wv_t, wz, gq, gk,
                                               SAMPLE_PAD, SAMPLE_PAD)
    oz = _a_sample_attention(f_rows, cache_a_k[0].reshape(bs, n_cache_a, d), k_nat,
                             cache_a_v[0].reshape(bs, n_cache_a, d), v32, q_t, zs, dec)
    ys_pad = _out_project(oz, w_out, xs_pad, mod_s, SAMPLE_PAD)
    new_a_k_s = k32[:, :dec].reshape(1, bs, dec, N_HEADS, A_HEAD_DIM)
    new_a_v_s = v32[:, :dec].reshape(1, bs, dec, N_HEADS, A_HEAD_DIM)

    w_in = b_w_in[0]
    scale = B_QK ** -0.5 * LOG2E
    wb = {
        "cq_t": w_in[:, :Q_LORA].T.astype(bf16),
        "ckv_t": w_in[:, Q_LORA:Q_LORA + KV_LORA].T.astype(bf16),
        "kr_t": w_in[:, Q_LORA + KV_LORA:Q_LORA + KV_LORA + B_ROPE].T.astype(bf16),
        "z": w_in[:, Q_LORA + KV_LORA + B_ROPE:].astype(bf16),
        "uq_t": b_w_uq[0].T.astype(bf16),
        "g_cq": _col(b_g_cq[0]), "g_ckv": _col(b_g_ckv[0]), "g_kr": _col(b_g_kr[0]),
        "g_qn": _col(b_g_qn[0], scale), "g_qr": _col(b_g_qr[0], scale),
    }
    wukv_t = b_w_ukv[0].T.astype(bf16)
    g_kn = _col(b_g_kn[0])
    w_out = b_w_out[0].astype(bf16)
    g1 = norm_g[1][None, :]
    mod_p, mod_s = mod[1, :bp], mod[1, bp:]

    cos_t, sin_t = _rope_tables(jnp.arange(seq, dtype=jnp.int32))
    q_t, ckv_p, kr_p, zs = _b_project(y_p, mod_p, g1, wb, cos_t, sin_t, min(B_PROJ_TILE, seq))
    k_nat, v_t = _b_kv_up(ckv_p, kr_p, wukv_t, g_kn, tm)
    q_sq = B_NOPE * jnp.max(jnp.abs(wb["g_qn"])) ** 2 + B_ROPE * jnp.max(jnp.abs(wb["g_qr"])) ** 2
    k_sq = B_NOPE * jnp.max(jnp.abs(g_kn)) ** 2 + B_ROPE * jnp.max(jnp.abs(wb["g_kr"])) ** 2
    qk_max = jnp.sqrt(q_sq * k_sq) * BOUND_MARGIN
    oz = lax.cond(2.0 * qk_max <= MAX_LOGIT_RANGE,
                  lambda: _b_attention_bounded(qk_max[None], k_nat, q_t, v_t, zs),
                  lambda: _b_attention(k_nat, q_t, v_t, zs))
    y_p = _out_project(oz, w_out, y_p, mod_p, tm)

    cos_t, sin_t = _rope_tables(past + jnp.arange(SAMPLE_PAD, dtype=jnp.int32))
    q_t, ckv_s, kr_s, zs = _b_project(ys_pad, mod_s, g1, wb, cos_t, sin_t, SAMPLE_PAD)
    k_new, v_new = _b_kv_up(ckv_s, kr_s, wukv_t, g_kn, SAMPLE_PAD)
    k_old, v_old = _b_kv_up(cache_mla_ckv[0], cache_mla_krope[0], wukv_t, g_kn, tm)
    oz = _b_sample_attention(k_old, k_new, v_old, v_new, q_t, zs, dec)
    ys_pad = _out_project(oz, w_out, ys_pad, mod_s, SAMPLE_PAD)

    return (y_p, ys_pad[:, :dec], new_a_k_p, new_a_v_p, new_a_k_s, new_a_v_s,
            ckv_p[None], kr_p[None], ckv_s[None, :, :dec], kr_s[None, :, :dec])
```

```python
import functools

import numpy as np

import jax
import jax.numpy as jnp
from jax import lax
from jax.experimental import pallas as pl
from jax.experimental.pallas import tpu as pltpu

f32 = jnp.float32
bf16 = jnp.bfloat16

EPS = 1e-6
NEG_INF = -1e30
CHUNK = 64
BAND_CHUNKS = 8
WINDOW = BAND_CHUNKS * CHUNK
REL_CLIP = 128
ROPE_THETA = 10000.0
LOG2E = 1.4426950408889634

LANES = 128
N_HEADS = 16
N_PAIRS = N_HEADS // 2
A_HEAD_DIM = 64
B_NOPE = 64
B_ROPE = 32
B_VDIM = 64
B_QK = B_NOPE + B_ROPE
KV_LORA = 256
Q_LORA = 384

Q_GROUP = 256
TILE_CHUNKS = Q_GROUP // CHUNK
MAX_LOGIT_RANGE = 100.0
BOUND_MARGIN = 1.02
A_KEYS = WINDOW + Q_GROUP
A_BIAS_ROWS = A_KEYS + WINDOW
A_BIAS_SPAN = A_BIAS_ROWS + Q_GROUP
SAMPLE_PAD = 128
ROW_TILE = 512
B_PROJ_TILE = 256
B_ITEMS_PER_STEP = 4
PAIRS_PER_STEP = 4
ONES_ROWS = 8
VMEM_LIMIT = 56 * 1024 * 1024


def _silu(x):
    return x * (1.0 / (1.0 + jnp.exp(-x)))


def _nt(a, b):
    return lax.dot_general(a, b, (((1,), (1,)), ((), ())), preferred_element_type=f32)


def _dot(a, b):
    return jnp.dot(a, b, preferred_element_type=f32)


def _rms_rows(blk, gain_col, n):
    ms = jnp.sum(blk * blk, axis=0, keepdims=True) * (1.0 / n)
    return blk * lax.rsqrt(ms + EPS) * gain_col


def _ada_hidden(x_ref, mod_ref, g_ref):
    x = x_ref[...]
    ms = jnp.mean(x * x, axis=-1, keepdims=True)
    xn = x * lax.rsqrt(ms + EPS) * g_ref[...]
    return (xn * (1.0 + mod_ref[1:2, :]) + mod_ref[0:1, :]).astype(bf16)


def _softmax_cols(s):
    m = jnp.max(s, axis=0, keepdims=True)
    p = jnp.exp2(s - m)
    return p, jnp.sum(p, axis=0, keepdims=True)


def _store_lane_tiles(ref, val):
    for p in range(ref.shape[0]):
        ref[p] = val[:, LANES * p:LANES * (p + 1)]


def _load_lane_tiles(ref):
    return jnp.concatenate([ref[p] for p in range(ref.shape[0])], axis=1)


def _gated(o_t, zs_ref, rows):
    return (o_t.T * zs_ref[rows, :].astype(f32)).astype(bf16)


def _normalise(o_ext, rows):
    return o_ext[:rows] * (1.0 / o_ext[rows:rows + 1])


def _bias_tile(f_row, n_rows, n_cols):
    blocks = []
    for a in range(pl.cdiv(n_rows, Q_GROUP)):
        lo = A_BIAS_ROWS - Q_GROUP * (a + 1)
        x = jnp.broadcast_to(f_row[:, lo:lo + 2 * Q_GROUP], (Q_GROUP, 2 * Q_GROUP))
        y = pltpu.roll(x, Q_GROUP + 1, 1, stride=1, stride_axis=0)[:, :n_cols]
        r = lax.broadcasted_iota(jnp.int32, (Q_GROUP, n_cols), 0) + Q_GROUP * a
        i = lax.broadcasted_iota(jnp.int32, (Q_GROUP, n_cols), 1)
        back = BAND_CHUNKS + i // CHUNK - r // CHUNK
        blocks.append(jnp.where((back >= 0) & (back <= BAND_CHUNKS), y, NEG_INF))
    return jnp.concatenate(blocks, axis=0)[:n_rows]


def _params(*sem):
    return pltpu.CompilerParams(dimension_semantics=sem, vmem_limit_bytes=VMEM_LIMIT)


def _const(shape):
    nd = len(shape)
    return pl.BlockSpec(shape, lambda *_: (0,) * nd)


def _pair_tiles(tm):
    return pl.BlockSpec((None, N_PAIRS, tm, LANES), lambda bi, i: (bi, 0, i, 0))


def _weight(shape):
    nd = len(shape)
    return pl.BlockSpec(shape, lambda *_: (0,) * nd, pipeline_mode=pl.Buffered(1))


def _mod_body(c_ref, w_ref, b_ref, o_ref):
    o_ref[...] = jnp.dot(_silu(c_ref[...]), w_ref[...], preferred_element_type=f32,
                         precision=lax.Precision.HIGHEST) + b_ref[...]


def _modulation(c_all, ada_w, ada_b):
    depth, d, n3 = ada_w.shape
    rows = c_all.shape[0]
    tn = d
    return pl.pallas_call(
        _mod_body,
        grid=(depth, n3 // tn),
        in_specs=[pl.BlockSpec((rows, d), lambda l, j: (0, 0)),
                  pl.BlockSpec((None, d, tn), lambda l, j: (l, 0, j)),
                  pl.BlockSpec((None, 1, tn), lambda l, j: (l, 0, j))],
        out_specs=pl.BlockSpec((None, rows, tn), lambda l, j: (l, 0, j)),
        out_shape=jax.ShapeDtypeStruct((depth, rows, n3), f32),
        compiler_params=_params("arbitrary", "arbitrary"),
        name="modulation",
    )(c_all, ada_w, ada_b.reshape(depth, 1, n3))


def _a_proj_body(n_skip, x_ref, mod_ref, g_ref, wq_ref, wk_ref, wv_ref, wz_ref, gq_ref, gk_ref,
                 qT_ref, k_ref, vT_ref, zs_ref, k32_ref, v32_ref, kf_ref):
    hb = _ada_hidden(x_ref, mod_ref, g_ref)
    z = _dot(hb, wz_ref[...])
    kT = _nt(wk_ref[...], hb)
    _store_lane_tiles(zs_ref, _silu(z).astype(bf16))
    qT = _nt(wq_ref[...], hb)
    gq = gq_ref[...]
    gk = gk_ref[...]
    hd = A_HEAD_DIM
    pairs = [slice(2 * hd * p, 2 * hd * (p + 1)) for p in range(N_HEADS // 2)]
    for p, rows in enumerate(pairs):
        k_pair = [_rms_rows(kT[hd * h:hd * (h + 1)], gk, hd) for h in (2 * p, 2 * p + 1)]
        k_nat = jnp.concatenate(k_pair, axis=0).T
        k_ref[p] = k_nat.astype(bf16)
        kf_ref[:, rows] = k_nat
    vT = _nt(wv_ref[...], hb)
    for p, rows in enumerate(pairs):
        q_pair = [_rms_rows(qT[hd * h:hd * (h + 1)], gq, hd) for h in (2 * p, 2 * p + 1)]
        qT_ref[rows, :] = jnp.concatenate(q_pair, axis=0).astype(bf16)
    vT_ref[...] = vT.astype(bf16)

    @pl.when(pl.program_id(1) >= n_skip)
    def _():
        k32_ref[...] = kf_ref[...]
        v32_ref[...] = vT.T


def _a_project(x, mod, g, wq_t, wk_t, wv_t, wz, gq, gk, tm, cache_rows):
    b, l, d = x.shape
    n_skip = (l - cache_rows) // tm
    tile = pl.BlockSpec((None, tm, d), lambda bi, i: (bi, i, 0))
    tile_t = pl.BlockSpec((None, d, tm), lambda bi, i: (bi, 0, i))
    cache = pl.BlockSpec((None, tm, d), lambda bi, i: (bi, jnp.maximum(i - n_skip, 0), 0))
    return pl.pallas_call(
        functools.partial(_a_proj_body, n_skip),
        grid=(b, l // tm),
        in_specs=[tile, pl.BlockSpec((None, 3, d), lambda bi, i: (bi, 0, 0)), _const((1, d)),
                  _weight((d, d)), _weight((d, d)), _weight((d, d)), _weight((d, d)),
                  _const((A_HEAD_DIM, 1)), _const((A_HEAD_DIM, 1))],
        out_specs=[tile_t, _pair_tiles(tm), tile_t, _pair_tiles(tm), cache, cache],
        out_shape=[jax.ShapeDtypeStruct((b, d, l), bf16), jax.ShapeDtypeStruct((b, N_PAIRS, l, LANES), bf16),
                   jax.ShapeDtypeStruct((b, d, l), bf16), jax.ShapeDtypeStruct((b, N_PAIRS, l, LANES), bf16),
                   jax.ShapeDtypeStruct((b, cache_rows, d), f32),
                   jax.ShapeDtypeStruct((b, cache_rows, d), f32)],
        scratch_shapes=[pltpu.VMEM((tm, d), f32)],
        compiler_params=_params("arbitrary", "arbitrary"),
        name="a_project",
    )(x, mod, g, wq_t, wk_t, wv_t, wz, gq, gk)


def _a_attn_body(seq, f_ref, k_ref, qT_ref, vT_ref, zs_ref, o_ref, e_ref, s0, s1, m0, m1, o0, o1):
    hd = A_HEAD_DIM
    n_groups = seq // Q_GROUP
    s_buf, m_buf, o_buf = (s0, s1), (m0, m1), (o0, o1)

    @pl.when(pl.program_id(1) == 0)
    def _():
        for hh in range(2):
            e_ref[hh] = _bias_tile(f_ref[hh], A_BIAS_ROWS, Q_GROUP)

    zeros = jnp.zeros((hd, Q_GROUP), bf16)
    ones = jnp.ones((ONES_ROWS, A_KEYS), bf16)

    def offsets(g):
        q0 = pl.multiple_of(g * Q_GROUP, Q_GROUP)
        w0 = pl.multiple_of(jnp.maximum(g * Q_GROUP - WINDOW, 0), Q_GROUP)
        e0 = pl.multiple_of(WINDOW - (g * Q_GROUP - w0), Q_GROUP)
        return q0, w0, e0

    def scores(g, par):
        q0, w0, e0 = offsets(g)
        k_win = k_ref[pl.ds(w0, A_KEYS), :]
        for hh in range(2):
            qh = qT_ref[hd * hh:hd * (hh + 1), pl.ds(q0, Q_GROUP)]
            w = jnp.concatenate([qh, zeros] if hh == 0 else [zeros, qh], axis=0)
            s = _dot(k_win, w) + e_ref[hh, pl.ds(e0, A_KEYS), :]
            s_buf[par][hh] = s
            m_buf[par][hh] = jnp.max(s, axis=0, keepdims=True)

    def values(g, par):
        _, w0, _ = offsets(g)
        for hh in range(2):
            p = jnp.exp2(s_buf[par][hh] - m_buf[par][hh]).astype(bf16)
            v_ext = jnp.concatenate([vT_ref[hd * hh:hd * (hh + 1), pl.ds(w0, A_KEYS)], ones], axis=0)
            o_buf[par][hh] = _dot(v_ext, p)

    def finish(g, par):
        q0, _, _ = offsets(g)
        outs = [_normalise(o_buf[par][hh], hd) for hh in range(2)]
        rows = pl.ds(q0, Q_GROUP)
        o_ref[rows, :] = _gated(jnp.concatenate(outs, axis=0), zs_ref, rows)

    scores(0, 0)
    scores(1, 1)
    values(0, 0)

    def body(i, carry):
        g = 2 * i + 1
        scores(g + 1, 0)
        values(g, 1)
        finish(g - 1, 0)
        scores(g + 2, 1)
        values(g + 1, 0)
        finish(g, 1)
        return carry

    lax.fori_loop(0, (n_groups - 2) // 2, body, 0)
    values(n_groups - 1, 1)
    finish(n_groups - 2, 0)
    finish(n_groups - 1, 1)


def _a_attention(f_rows, k_nat, q_t, v_t, zs):
    b, n_pairs, l, pair = k_nat.shape
    slab = pl.BlockSpec((None, None, l, pair), lambda p, bi: (bi, p, 0, 0))
    scratch = ([pltpu.VMEM((2, A_BIAS_ROWS, Q_GROUP), f32)]
               + [pltpu.VMEM((2, A_KEYS, Q_GROUP), f32)] * 2
               + [pltpu.VMEM((2, 1, Q_GROUP), f32)] * 2
               + [pltpu.VMEM((2, A_HEAD_DIM + ONES_ROWS, Q_GROUP), f32)] * 2)
    return pl.pallas_call(
        functools.partial(_a_attn_body, l),
        grid=(n_pairs, b),
        in_specs=[pl.BlockSpec((2, 1, A_BIAS_SPAN), lambda p, bi: (p, 0, 0)),
                  slab,
                  pl.BlockSpec((None, pair, l), lambda p, bi: (bi, p, 0)),
                  pl.BlockSpec((None, pair, l), lambda p, bi: (bi, p, 0)),
                  slab],
        out_specs=slab,
        out_shape=jax.ShapeDtypeStruct((b, n_pairs, l, pair), bf16),
        scratch_shapes=scratch,
        compiler_params=_params("arbitrary", "arbitrary"),
        name="a_attention",
    )(f_rows, k_nat, q_t, v_t, zs)


def _a_attn_bounded_body(seq, shift_ref, f_ref, k_ref, qT_ref, vT_ref, zs_ref, o_ref, e_ref, p0, p1, o0, o1):
    hd = A_HEAD_DIM
    n_groups = seq // Q_GROUP
    n_pairs = k_ref.shape[0]
    n_items = n_pairs * n_groups
    p_buf, o_buf = (p0, p1), (o0, o1)

    @pl.when(pl.program_id(1) == 0)
    def _():
        for h in range(2 * n_pairs):
            e_ref[h] = _bias_tile(f_ref[h], A_BIAS_ROWS, Q_GROUP) - shift_ref[0]

    zeros = jnp.zeros((hd, Q_GROUP), bf16)
    ones = jnp.ones((ONES_ROWS, A_KEYS), bf16)

    def offsets(n):
        pp = n // n_groups
        g = n - pp * n_groups
        q0 = pl.multiple_of(g * Q_GROUP, Q_GROUP)
        w0 = pl.multiple_of(jnp.maximum(g * Q_GROUP - WINDOW, 0), Q_GROUP)
        e0 = pl.multiple_of(WINDOW - (g * Q_GROUP - w0), Q_GROUP)
        return pp, q0, w0, e0

    def probs(n, par):
        pp, q0, w0, e0 = offsets(n)
        k_win = k_ref[pp, pl.ds(w0, A_KEYS), :]
        for hh in range(2):
            r0 = pl.multiple_of((2 * pp + hh) * hd, hd)
            qh = qT_ref[pl.ds(r0, hd), pl.ds(q0, Q_GROUP)]
            w = jnp.concatenate([qh, zeros] if hh == 0 else [zeros, qh], axis=0)
            bias = e_ref[2 * pp + hh, pl.ds(e0, A_KEYS), :]
            p_buf[par][hh] = jnp.exp2(_dot(k_win, w) + bias).astype(bf16)

    def values(n, par):
        pp, _, w0, _ = offsets(n)
        for hh in range(2):
            r0 = pl.multiple_of((2 * pp + hh) * hd, hd)
            v_ext = jnp.concatenate([vT_ref[pl.ds(r0, hd), pl.ds(w0, A_KEYS)], ones], axis=0)
            o_buf[par][hh] = _dot(v_ext, p_buf[par][hh])

    def finish(n, par):
        pp, q0, _, _ = offsets(n)
        outs = [_normalise(o_buf[par][hh], hd) for hh in range(2)]
        rows = pl.ds(q0, Q_GROUP)
        o_ref[pp, rows, :] = _gated(jnp.concatenate(outs, axis=0), zs_ref.at[pp], rows)

    probs(0, 0)
    probs(1, 1)
    values(0, 0)

    def body(i, carry):
        n = 2 * i + 1
        probs(n + 1, 0)
        values(n, 1)
        finish(n - 1, 0)
        probs(n + 2, 1)
        values(n + 1, 0)
        finish(n, 1)
        return carry

    lax.fori_loop(0, (n_items - 2) // 2, body, 0)
    values(n_items - 1, 1)
    finish(n_items - 2, 0)
    finish(n_items - 1, 1)


def _a_attention_bounded(shift, f_rows, k_nat, q_t, v_t, zs):
    b, n_pairs, l, pair = k_nat.shape
    pps = PAIRS_PER_STEP
    slab = pl.BlockSpec((None, pps, l, pair), lambda p, bi, sh: (bi, p, 0, 0))
    scratch = ([pltpu.VMEM((2 * pps, A_BIAS_ROWS, Q_GROUP), f32)]
               + [pltpu.VMEM((2, A_KEYS, Q_GROUP), bf16)] * 2
               + [pltpu.VMEM((2, A_HEAD_DIM + ONES_ROWS, Q_GROUP), f32)] * 2)
    grid_spec = pltpu.PrefetchScalarGridSpec(
        num_scalar_prefetch=1,
        grid=(n_pairs // pps, b),
        in_specs=[pl.BlockSpec((2 * pps, 1, A_BIAS_SPAN), lambda p, bi, sh: (p, 0, 0)),
                  slab,
                  pl.BlockSpec((None, pps * pair, l), lambda p, bi, sh: (bi, p, 0)),
                  pl.BlockSpec((None, pps * pair, l), lambda p, bi, sh: (bi, p, 0)),
                  slab],
        out_specs=slab,
        scratch_shapes=scratch)
    return pl.pallas_call(
        functools.partial(_a_attn_bounded_body, l),
        grid_spec=grid_spec,
        out_shape=jax.ShapeDtypeStruct((b, n_pairs, l, pair), bf16),
        compiler_params=_params("arbitrary", "arbitrary"),
        name="a_attention_bounded",
    )(shift, f_rows, k_nat, q_t, v_t, zs)


def _a_sattn_body(n_valid, f_ref, kc_ref, kn_ref, vc_ref, vn_ref, qT_ref, zs_ref, o_ref, e_ref):
    hd = A_HEAD_DIM
    n_keys = e_ref.shape[1]

    @pl.when(pl.program_id(1) == 0)
    def _():
        live = lax.broadcasted_iota(jnp.int32, (n_keys, SAMPLE_PAD), 0) < n_valid
        for hh in range(2):
            e_ref[hh] = jnp.where(live, _bias_tile(f_ref[hh], n_keys, SAMPLE_PAD), NEG_INF)

    zeros = jnp.zeros((hd, SAMPLE_PAD), bf16)
    k_all = jnp.concatenate([kc_ref[...].astype(bf16), kn_ref[...]], axis=0)
    v_t = jnp.concatenate([vc_ref[...], vn_ref[...]], axis=0).T.astype(bf16)
    outs = []
    for hh in range(2):
        qh = qT_ref[hd * hh:hd * (hh + 1), :]
        w = jnp.concatenate([qh, zeros] if hh == 0 else [zeros, qh], axis=0)
        p, l = _softmax_cols(_dot(k_all, w) + e_ref[hh])
        outs.append(_dot(v_t[hd * hh:hd * (hh + 1)], p.astype(bf16)) * (1.0 / l))
    o_ref[...] = _gated(jnp.concatenate(outs, axis=0), zs_ref, slice(None))


def _a_sample_attention(f_rows, k_cache, k_new, v_cache, v_new, q_t, zs, n_new):
    b, n_cache, d = k_cache.shape
    pair = 2 * A_HEAD_DIM
    n_keys = n_cache + SAMPLE_PAD
    new = pl.BlockSpec((None, SAMPLE_PAD, pair), lambda p, bi: (bi, 0, p))
    old = pl.BlockSpec((None, n_cache, pair), lambda p, bi: (bi, 0, p))
    slab = pl.BlockSpec((None, None, SAMPLE_PAD, pair), lambda p, bi: (bi, p, 0, 0))
    return pl.pallas_call(
        functools.partial(_a_sattn_body, n_cache + n_new),
        grid=(d // pair, b),
        in_specs=[pl.BlockSpec((2, 1, A_BIAS_SPAN), lambda p, bi: (p, 0, 0)),
                  old, slab, old, new,
                  pl.BlockSpec((None, pair, SAMPLE_PAD), lambda p, bi: (bi, p, 0)),
                  slab],
        out_specs=slab,
        out_shape=jax.ShapeDtypeStruct((b, d // pair, SAMPLE_PAD, pair), bf16),
        scratch_shapes=[pltpu.VMEM((2, n_keys, SAMPLE_PAD), f32)],
        compiler_params=_params("arbitrary", "arbitrary"),
        name="a_sample_attention",
    )(f_rows, k_cache, k_new, v_cache, v_new, q_t, zs)


def _out_body(oz_ref, w_ref, x_ref, mod_ref, y_ref):
    y = _dot(_load_lane_tiles(oz_ref), w_ref[...])
    y_ref[...] = x_ref[...] + mod_ref[2:3, :] * y


def _out_project(oz, w_out, x, mod, tm):
    b, l, d = x.shape
    tile = pl.BlockSpec((None, tm, d), lambda bi, i: (bi, i, 0))
    return pl.pallas_call(
        _out_body,
        grid=(b, l // tm),
        in_specs=[_pair_tiles(tm), _weight(w_out.shape), tile,
                  pl.BlockSpec((None, 3, d), lambda bi, i: (bi, 0, 0))],
        out_specs=tile,
        out_shape=jax.ShapeDtypeStruct((b, l, d), f32),
        compiler_params=_params("arbitrary", "arbitrary"),
        name="out_project",
    )(oz, w_out, x, mod)


def _rotate(x, cos, sin):
    half = B_ROPE // 2
    x1, x2 = x[:half], x[half:]
    return jnp.concatenate([x1 * cos - x2 * sin, x2 * cos + x1 * sin], axis=0)


def _b_proj_body(x_ref, mod_ref, g_ref, wcq_ref, wckv_ref, wkr_ref, wz_ref, wuq_ref,
                 gcq_ref, gckv_ref, gkr_ref, gqn_ref, gqr_ref, cos_ref, sin_ref,
                 qT_ref, ckv_ref, kr_ref, zs_ref):
    hb = _ada_hidden(x_ref, mod_ref, g_ref)
    tm = hb.shape[0]
    cos = cos_ref[...]
    sin = sin_ref[...]
    cq_raw = _nt(wcq_ref[...], hb)
    z = _dot(hb, wz_ref[...])
    ckv_raw = _nt(wckv_ref[...], hb)
    kr_raw = _nt(wkr_ref[...], hb)
    cq = _rms_rows(cq_raw, gcq_ref[...], Q_LORA).astype(bf16)
    qT = _dot(wuq_ref[...], cq)
    _store_lane_tiles(zs_ref, _silu(z).astype(bf16))
    ckv_ref[...] = _rms_rows(ckv_raw, gckv_ref[...], KV_LORA).T
    kr = _rotate(_rms_rows(kr_raw, gkr_ref[...], B_ROPE), cos, sin)
    kr_pad = jnp.concatenate([kr, jnp.zeros((128 - B_ROPE, tm), f32)], axis=0)
    kr_ref[...] = kr_pad.T[:, :B_ROPE]
    gqn = gqn_ref[...]
    gqr = gqr_ref[...]
    for h in range(N_HEADS):
        base = B_QK * h
        nope = _rms_rows(qT[base:base + B_NOPE], gqn, B_NOPE)
        rope = _rotate(_rms_rows(qT[base + B_NOPE:base + B_QK], gqr, B_ROPE), cos, sin)
        qT_ref[base:base + B_QK, :] = jnp.concatenate([nope, rope], axis=0).astype(bf16)


def _b_project(x, mod, g, w, cos_t, sin_t, tm):
    b, l, d = x.shape
    tile = pl.BlockSpec((None, tm, d), lambda bi, i: (bi, i, 0))
    rot = pl.BlockSpec((B_ROPE // 2, tm), lambda bi, i: (0, i))
    nq = N_HEADS * B_QK
    return pl.pallas_call(
        _b_proj_body,
        grid=(b, l // tm),
        in_specs=[tile, pl.BlockSpec((None, 3, d), lambda bi, i: (bi, 0, 0)), _const((1, d)),
                  _weight((Q_LORA, d)), _weight((KV_LORA, d)), _weight((B_ROPE, d)), _weight((d, d)),
                  _weight((nq, Q_LORA)),
                  _const((Q_LORA, 1)), _const((KV_LORA, 1)), _const((B_ROPE, 1)),
                  _const((B_NOPE, 1)), _const((B_ROPE, 1)), rot, rot],
        out_specs=[pl.BlockSpec((None, nq, tm), lambda bi, i: (bi, 0, i)),
                   pl.BlockSpec((None, tm, KV_LORA), lambda bi, i: (bi, i, 0)),
                   pl.BlockSpec((None, tm, B_ROPE), lambda bi, i: (bi, i, 0)),
                   _pair_tiles(tm)],
        out_shape=[jax.ShapeDtypeStruct((b, nq, l), bf16),
                   jax.ShapeDtypeStruct((b, l, KV_LORA), f32),
                   jax.ShapeDtypeStruct((b, l, B_ROPE), f32),
                   jax.ShapeDtypeStruct((b, N_PAIRS, l, LANES), bf16)],
        compiler_params=_params("arbitrary", "arbitrary"),
        name="b_project",
    )(x, mod, g, w["cq_t"], w["ckv_t"], w["kr_t"], w["z"], w["uq_t"],
      w["g_cq"], w["g_ckv"], w["g_kr"], w["g_qn"], w["g_qr"], cos_t, sin_t)


def _b_kvup_body(ckv_ref, kr_ref, wukv_ref, gkn_ref, k_ref, vT_ref):
    tm = ckv_ref.shape[0]
    kvT = _nt(wukv_ref[...], ckv_ref[...].astype(bf16))
    eye = (lax.broadcasted_iota(jnp.int32, (B_ROPE, B_ROPE), 0)
           == lax.broadcasted_iota(jnp.int32, (B_ROPE, B_ROPE), 1)).astype(bf16)
    krT = _nt(eye, kr_ref[...].astype(bf16))
    key_chunk = (lax.broadcasted_iota(jnp.int32, (128 - B_QK, tm), 1) // CHUNK) % TILE_CHUNKS
    spare_row = lax.broadcasted_iota(jnp.int32, (128 - B_QK, tm), 0)
    chunk_rows = ((spare_row == key_chunk) | (spare_row == TILE_CHUNKS)).astype(f32)
    tail = jnp.concatenate([krT, chunk_rows], axis=0)
    gkn = gkn_ref[...]
    per_head = B_NOPE + B_VDIM
    for h in range(N_HEADS):
        base = per_head * h
        kn = _rms_rows(kvT[base:base + B_NOPE], gkn, B_NOPE)
        k_ref[h] = jnp.concatenate([kn, tail], axis=0).T.astype(bf16)
        vT_ref[B_VDIM * h:B_VDIM * (h + 1), :] = kvT[base + B_NOPE:base + per_head].astype(bf16)


def _b_kv_up(ckv, kr, wukv_t, g_kn, tm):
    b, l, _ = ckv.shape
    return pl.pallas_call(
        _b_kvup_body,
        grid=(b, l // tm),
        in_specs=[pl.BlockSpec((None, tm, KV_LORA), lambda bi, i: (bi, i, 0)),
                  pl.BlockSpec((None, tm, B_ROPE), lambda bi, i: (bi, i, 0)),
                  _const(wukv_t.shape), _const((B_NOPE, 1))],
        out_specs=[pl.BlockSpec((None, N_HEADS, tm, 128), lambda bi, i: (bi, 0, i, 0)),
                   pl.BlockSpec((None, N_HEADS * B_VDIM, tm), lambda bi, i: (bi, 0, i))],
        out_shape=[jax.ShapeDtypeStruct((b, N_HEADS, l, 128), bf16),
                   jax.ShapeDtypeStruct((b, N_HEADS * B_VDIM, l), bf16)],
        compiler_params=_params("arbitrary", "arbitrary"),
        name="b_kv_up",
    )(ckv, kr, wukv_t, g_kn)


def _b_item_table(n_tiles, n_pairs=1):
    rows = [(pp, t, j, int(j == t)) for pp in range(n_pairs) for t in range(n_tiles) for j in range(t, -1, -1)]
    return np.asarray(rows, np.int32).T.copy()


def _b_items_per_step(n_items, odd_steps):
    return max(u for u in range(1, B_ITEMS_PER_STEP + 1)
               if n_items % u == 0 and n_items // u >= 3 and (not odd_steps or (n_items // u) % 2 == 1))


def _b_attn_body(seq, per, tab_ref, k_ref, qT_ref, vT_ref, zs_ref, o_ref,
                 s0, s1, c0, c1, pv0, pv1, a0, a1, m_st, acc_st):
    t_sz = Q_GROUP
    n_tiles = seq // t_sz
    n_steps = n_tiles * (n_tiles + 1) // 2 // per
    s_buf, c_buf, pv_buf, a_buf = (s0, s1), (c0, c1), (pv0, pv1), (a0, a1)
    ones = jnp.ones((ONES_ROWS, t_sz), bf16)
    key_chunk = lax.broadcasted_iota(jnp.int32, (128 - B_QK, t_sz), 0)
    qry_chunk = lax.broadcasted_iota(jnp.int32, (128 - B_QK, t_sz), 1) // CHUNK
    mask_rows = jnp.where((key_chunk < t_sz // CHUNK) & (key_chunk > qry_chunk), NEG_INF, 0.0).astype(bf16)
    zero_rows = jnp.zeros((128 - B_QK, t_sz), bf16)
    m_st[...] = jnp.full(m_st.shape, NEG_INF, f32)
    acc_st[...] = jnp.zeros(acc_st.shape, f32)

    def item(step, u):
        n = step * per + u
        return tab_ref[1, n], tab_ref[2, n], tab_ref[3, n] == 1

    def scores(step, par, u):
        t, j, diag = item(step, u)
        q0 = pl.multiple_of(t * t_sz, t_sz)
        k0 = pl.multiple_of(j * t_sz, t_sz)
        pad = jnp.where(diag, mask_rows, zero_rows)
        for hh in range(2):
            w = jnp.concatenate([qT_ref[B_QK * hh:B_QK * (hh + 1), pl.ds(q0, t_sz)], pad], axis=0)
            s = _dot(k_ref[hh, pl.ds(k0, t_sz), :], w)
            s_buf[par][u, hh] = s
            c_buf[par][u, hh] = jnp.max(s, axis=0, keepdims=True)

    def values(step, par, u):
        t, j, _ = item(step, u)
        k0 = pl.multiple_of(j * t_sz, t_sz)
        for hh in range(2):
            m_old = m_st[t, hh]
            m_new = jnp.maximum(m_old, c_buf[par][u, hh])
            m_st[t, hh] = m_new
            a_buf[par][u, hh] = jnp.exp2(m_old - m_new)
            p = jnp.exp2(s_buf[par][u, hh] - m_new).astype(bf16)
            v_ext = jnp.concatenate([vT_ref[B_VDIM * hh:B_VDIM * (hh + 1), pl.ds(k0, t_sz)], ones], axis=0)
            pv_buf[par][u, hh] = _dot(v_ext, p)

    def accumulate(step, par, u):
        t, _, _ = item(step, u)
        for hh in range(2):
            acc_st[t, hh] = a_buf[par][u, hh] * acc_st[t, hh] + pv_buf[par][u, hh]

    def time_step(tau, par, first=False, drain=0):
        for u in range(per):
            if drain == 0:
                scores(tau + 1, 1 - par, u)
            if drain <= 1:
                values(tau, par, u)
            if not first:
                accumulate(tau - 1, 1 - par, u)

    for u in range(per):
        scores(0, 0, u)
    time_step(0, 0, first=True)

    def body(i, carry):
        tau = 2 * i + 1
        time_step(tau, 1)
        time_step(tau + 1, 0)
        return carry

    lax.fori_loop(0, (n_steps - 3) // 2, body, 0)
    time_step(n_steps - 2, 1)
    time_step(n_steps - 1, 0, drain=1)
    time_step(n_steps, 1, drain=2)
    for t in range(n_tiles):
        outs = [_normalise(acc_st[t, hh], B_VDIM) for hh in range(2)]
        rows = slice(t * t_sz, (t + 1) * t_sz)
        o_ref[rows, :] = _gated(jnp.concatenate(outs, axis=0), zs_ref, rows)


def _b_attention(k_nat, q_t, v_t, zs):
    b, _, l, _ = k_nat.shape
    t_sz = Q_GROUP
    n_tiles = l // t_sz
    table = _b_item_table(n_tiles)
    per = _b_items_per_step(table.shape[1], odd_steps=True)
    ext = B_VDIM + ONES_ROWS
    scratch = ([pltpu.VMEM((per, 2, t_sz, t_sz), f32)] * 2 + [pltpu.VMEM((per, 2, 1, t_sz), f32)] * 2
               + [pltpu.VMEM((per, 2, ext, t_sz), f32)] * 2 + [pltpu.VMEM((per, 2, 1, t_sz), f32)] * 2
               + [pltpu.VMEM((n_tiles, 2, 1, t_sz), f32), pltpu.VMEM((n_tiles, 2, ext, t_sz), f32)])
    grid_spec = pltpu.PrefetchScalarGridSpec(
        num_scalar_prefetch=1,
        grid=(b, N_HEADS // 2),
        in_specs=[pl.BlockSpec((None, 2, l, 128), lambda bi, p, tab: (bi, p, 0, 0)),
                  pl.BlockSpec((None, 2 * B_QK, l), lambda bi, p, tab: (bi, p, 0)),
                  pl.BlockSpec((None, 2 * B_VDIM, l), lambda bi, p, tab: (bi, p, 0)),
                  pl.BlockSpec((None, None, l, LANES), lambda bi, p, tab: (bi, p, 0, 0))],
        out_specs=pl.BlockSpec((None, None, l, LANES), lambda bi, p, tab: (bi, p, 0, 0)),
        scratch_shapes=scratch)
    return pl.pallas_call(
        functools.partial(_b_attn_body, l, per),
        grid_spec=grid_spec,
        out_shape=jax.ShapeDtypeStruct((b, N_PAIRS, l, LANES), bf16),
        compiler_params=_params("arbitrary", "arbitrary"),
        name="b_attention",
    )(jnp.asarray(table), k_nat, q_t, v_t, zs)


def _b_attn_bounded_body(seq, per, tab_ref, shift_ref, k_ref, qT_ref, vT_ref, zs_ref, o_ref, p0, p1, acc_st):
    t_sz = Q_GROUP
    n_tiles = seq // t_sz
    n_pairs = acc_st.shape[0]
    n_steps = n_pairs * n_tiles * (n_tiles + 1) // 2 // per
    p_buf = (p0, p1)
    ones = jnp.ones((ONES_ROWS, t_sz), bf16)
    row = lax.broadcasted_iota(jnp.int32, (128 - B_QK, t_sz), 0)
    qry_chunk = lax.broadcasted_iota(jnp.int32, (128 - B_QK, t_sz), 1) // CHUNK
    shift_rows = jnp.where(row == TILE_CHUNKS, -shift_ref[0], 0.0)
    mask_rows = (jnp.where((row < TILE_CHUNKS) & (row > qry_chunk), NEG_INF, 0.0) + shift_rows).astype(bf16)
    plain_rows = shift_rows.astype(bf16)
    acc_st[...] = jnp.zeros(acc_st.shape, f32)

    def item(step, u):
        n = step * per + u
        return tab_ref[0, n], tab_ref[1, n], tab_ref[2, n], tab_ref[3, n] == 1

    def probs(step, par, u):
        pp, t, j, diag = item(step, u)
        q0 = pl.multiple_of(t * t_sz, t_sz)
        k0 = pl.multiple_of(j * t_sz, t_sz)
        pad = jnp.where(diag, mask_rows, plain_rows)
        for hh in range(2):
            r0 = pl.multiple_of((2 * pp + hh) * B_QK, B_ROPE)
            w = jnp.concatenate([qT_ref[pl.ds(r0, B_QK), pl.ds(q0, t_sz)], pad], axis=0)
            p_buf[par][u, hh] = jnp.exp2(_dot(k_ref[2 * pp + hh, pl.ds(k0, t_sz), :], w)).astype(bf16)

    def values(step, par, u):
        pp, t, j, _ = item(step, u)
        k0 = pl.multiple_of(j * t_sz, t_sz)
        for hh in range(2):
            r0 = pl.multiple_of((2 * pp + hh) * B_VDIM, B_VDIM)
            v_ext = jnp.concatenate([vT_ref[pl.ds(r0, B_VDIM), pl.ds(k0, t_sz)], ones], axis=0)
            acc_st[pp, t, hh] += _dot(v_ext, p_buf[par][u, hh])

    def time_step(tau, last):
        par = tau % 2
        for u in range(per):
            if not last:
                probs(tau + 1, 1 - par, u)
            values(tau, par, u)

    def step_pair(i, carry):
        for par in range(2):
            for u in range(per):
                probs(2 * i + par + 1, 1 - par, u)
                values(2 * i + par, par, u)
        return carry

    for u in range(per):
        probs(0, 0, u)
    n_loop = (n_steps - 1) // 2
    lax.fori_loop(0, n_loop, step_pair, 0)
    for tau in range(2 * n_loop, n_steps):
        time_step(tau, last=tau == n_steps - 1)
    for pp in range(n_pairs):
        for t in range(n_tiles):
            outs = [_normalise(acc_st[pp, t, hh], B_VDIM) for hh in range(2)]
            rows = slice(t * t_sz, (t + 1) * t_sz)
            o_ref[pp, rows, :] = _gated(jnp.concatenate(outs, axis=0), zs_ref.at[pp], rows)


def _b_attention_bounded(shift, k_nat, q_t, v_t, zs):
    b, _, l, _ = k_nat.shape
    t_sz = Q_GROUP
    n_tiles = l // t_sz
    pps = PAIRS_PER_STEP
    table = _b_item_table(n_tiles, pps)
    per = _b_items_per_step(table.shape[1], odd_steps=False)
    scratch = ([pltpu.VMEM((per, 2, t_sz, t_sz), bf16)] * 2
               + [pltpu.VMEM((pps, n_tiles, 2, B_VDIM + ONES_ROWS, t_sz), f32)])
    slab = pl.BlockSpec((None, pps, l, LANES), lambda bi, p, tab, sh: (bi, p, 0, 0))
    grid_spec = pltpu.PrefetchScalarGridSpec(
        num_scalar_prefetch=2,
        grid=(b, N_PAIRS // pps),
        in_specs=[pl.BlockSpec((None, 2 * pps, l, 128), lambda bi, p, tab, sh: (bi, p, 0, 0)),
                  pl.BlockSpec((None, 2 * pps * B_QK, l), lambda bi, p, tab, sh: (bi, p, 0)),
                  pl.BlockSpec((None, 2 * pps * B_VDIM, l), lambda bi, p, tab, sh: (bi, p, 0)),
                  slab],
        out_specs=slab,
        scratch_shapes=scratch)
    return pl.pallas_call(
        functools.partial(_b_attn_bounded_body, l, per),
        grid_spec=grid_spec,
        out_shape=jax.ShapeDtypeStruct((b, N_PAIRS, l, LANES), bf16),
        compiler_params=_params("arbitrary", "arbitrary"),
        name="b_attention_bounded",
    )(jnp.asarray(table), shift, k_nat, q_t, v_t, zs)


def _b_sattn_body(n_new, kc_ref, kn_ref, vc_ref, vn_ref, qT_ref, zs_ref, o_ref):
    t = SAMPLE_PAD
    zpad = jnp.zeros((128 - B_QK, t), bf16)
    new_mask = jnp.where(lax.broadcasted_iota(jnp.int32, (t, t), 0) < n_new, 0.0, NEG_INF).astype(f32)
    outs = []
    for hh in range(2):
        w = jnp.concatenate([qT_ref[B_QK * hh:B_QK * (hh + 1), :], zpad], axis=0)
        s_c = _dot(kc_ref[hh], w)
        s_n = _dot(kn_ref[hh], w) + new_mask
        m = jnp.maximum(jnp.max(s_c, axis=0, keepdims=True), jnp.max(s_n, axis=0, keepdims=True))
        p_c = jnp.exp2(s_c - m)
        p_n = jnp.exp2(s_n - m)
        l = jnp.sum(p_c, axis=0, keepdims=True) + jnp.sum(p_n, axis=0, keepdims=True)
        rows = slice(B_VDIM * hh, B_VDIM * (hh + 1))
        o_t = _dot(vc_ref[rows, :], p_c.astype(bf16)) + _dot(vn_ref[rows, :], p_n.astype(bf16))
        outs.append(o_t * (1.0 / l))
    o_ref[...] = _gated(jnp.concatenate(outs, axis=0), zs_ref, slice(None))


def _b_sample_attention(k_cache, k_new, v_cache, v_new, q_t, zs, n_new):
    b, _, past, _ = k_cache.shape
    t = SAMPLE_PAD
    slab = pl.BlockSpec((None, None, t, LANES), lambda bi, p: (bi, p, 0, 0))
    return pl.pallas_call(
        functools.partial(_b_sattn_body, n_new),
        grid=(b, N_HEADS // 2),
        in_specs=[pl.BlockSpec((None, 2, past, 128), lambda bi, p: (bi, p, 0, 0)),
                  pl.BlockSpec((None, 2, t, 128), lambda bi, p: (bi, p, 0, 0)),
                  pl.BlockSpec((None, 2 * B_VDIM, past), lambda bi, p: (bi, p, 0)),
                  pl.BlockSpec((None, 2 * B_VDIM, t), lambda bi, p: (bi, p, 0)),
                  pl.BlockSpec((None, 2 * B_QK, t), lambda bi, p: (bi, p, 0)),
                  slab],
        out_specs=slab,
        out_shape=jax.ShapeDtypeStruct((b, N_PAIRS, t, LANES), bf16),
        compiler_params=_params("arbitrary", "arbitrary"),
        name="b_sample_attention",
    )(k_cache, k_new, v_cache, v_new, q_t, zs)


def _bias_rows(table):
    left = A_BIAS_ROWS - 1 - WINDOW - REL_CLIP
    right = A_BIAS_SPAN - left - table.shape[-1]
    return (jnp.pad(table.astype(f32), ((0, 0), (left, right)), mode="edge") * LOG2E)[:, None, :]


def _rope_tables(pos):
    half = B_ROPE // 2
    inv = ROPE_THETA ** (-jnp.arange(half, dtype=f32) / half)
    ang = pos.astype(f32)[:, None] * inv[None, :]
    return jnp.cos(ang).T, jnp.sin(ang).T


def _col(g, scale=1.0):
    return (g.astype(f32) * scale)[:, None]


def kernel(x_prompt, x_sample, cache_a_k, cache_a_v, cache_mla_ckv, cache_mla_krope, c_prompt, c_sample, norm_g, ada_w, ada_b, a_w_in, a_g_q, a_g_k, a_rel_bias, a_w_out, b_w_in, b_g_cq, b_w_uq, b_g_ckv, b_w_ukv, b_g_qn, b_g_qr, b_g_kn, b_g_kr, b_w_out):
    bp, seq, d = x_prompt.shape
    bs, dec, _ = x_sample.shape
    past = cache_mla_ckv.shape[2]
    n_cache_a = cache_a_k.shape[2]
    assert d == N_HEADS * A_HEAD_DIM and seq % (2 * Q_GROUP) == 0 and seq >= A_KEYS
    assert past % CHUNK == 0 and dec <= CHUNK and dec <= SAMPLE_PAD and n_cache_a == WINDOW
    cache_rows = min(WINDOW, seq)
    tm = min(ROW_TILE, seq)

    mod = _modulation(jnp.concatenate([c_prompt, c_sample], axis=0), ada_w, ada_b)
    mod = mod.reshape(mod.shape[0], bp + bs, 3, d)
    xs_pad = jnp.pad(x_sample, ((0, 0), (0, SAMPLE_PAD - dec), (0, 0)))

    w_in = a_w_in[0]
    wq_t, wk_t, wv_t = (w_in[:, d * n:d * (n + 1)].T.astype(bf16) for n in range(3))
    wz = w_in[:, 3 * d:].astype(bf16)
    w_out = a_w_out[0].astype(bf16)
    gq = _col(a_g_q[0], A_HEAD_DIM ** -0.5 * LOG2E)
    gk = _col(a_g_k[0])
    g0 = norm_g[0][None, :]
    f_rows = _bias_rows(a_rel_bias[0])
    mod_p, mod_s = mod[0, :bp], mod[0, bp:]

    q_t, k_nat, v_t, zs, k32, v32 = _a_project(x_prompt, mod_p, g0, wq_t, wk_t, wv_t, wz, gq, gk, tm, cache_rows)
    qk_max = A_HEAD_DIM * jnp.max(jnp.abs(gq)) * jnp.max(jnp.abs(gk)) * BOUND_MARGIN
    upper = qk_max + jnp.max(f_rows)
    oz = lax.cond(upper - (jnp.min(f_rows) - qk_max) <= MAX_LOGIT_RANGE,
                  lambda: _a_attention_bounded(upper[None], f_rows, k_nat, q_t, v_t, zs),
                  lambda: _a_attention(f_rows, k_nat, q_t, v_t, zs))
    y_p = _out_project(oz, w_out, x_prompt, mod_p, tm)
    new_a_k_p = k32.reshape(1, bp, cache_rows, N_HEADS, A_HEAD_DIM)
    new_a_v_p = v32.reshape(1, bp, cache_rows, N_HEADS, A_HEAD_DIM)

    q_t, k_nat, v_t, zs, k32, v32 = _a_project(xs_pad, mod_s, g0, wq_t, wk_t, wv_t, wz, gq, gk,
                                               SAMPLE_PAD, SAMPLE_PAD)
    oz = _a_sample_attention(f_rows, cache_a_k[0].reshape(bs, n_cache_a, d), k_nat,
                             cache_a_v[0].reshape(bs, n_cache_a, d), v32, q_t, zs, dec)
    ys_pad = _out_project(oz, w_out, xs_pad, mod_s, SAMPLE_PAD)
    new_a_k_s = k32[:, :dec].reshape(1, bs, dec, N_HEADS, A_HEAD_DIM)
    new_a_v_s = v32[:, :dec].reshape(1, bs, dec, N_HEADS, A_HEAD_DIM)

    w_in = b_w_in[0]
    scale = B_QK ** -0.5 * LOG2E
    wb = {
        "cq_t": w_in[:, :Q_LORA].T.astype(bf16),
        "ckv_t": w_in[:, Q_LORA:Q_LORA + KV_LORA].T.astype(bf16),
        "kr_t": w_in[:, Q_LORA + KV_LORA:Q_LORA + KV_LORA + B_ROPE].T.astype(bf16),
        "z": w_in[:, Q_LORA + KV_LORA + B_ROPE:].astype(bf16),
        "uq_t": b_w_uq[0].T.astype(bf16),
        "g_cq": _col(b_g_cq[0]), "g_ckv": _col(b_g_ckv[0]), "g_kr": _col(b_g_kr[0]),
        "g_qn": _col(b_g_qn[0], scale), "g_qr": _col(b_g_qr[0], scale),
    }
    wukv_t = b_w_ukv[0].T.astype(bf16)
    g_kn = _col(b_g_kn[0])
    w_out = b_w_out[0].astype(bf16)
    g1 = norm_g[1][None, :]
    mod_p, mod_s = mod[1, :bp], mod[1, bp:]

    cos_t, sin_t = _rope_tables(jnp.arange(seq, dtype=jnp.int32))
    q_t, ckv_p, kr_p, zs = _b_project(y_p, mod_p, g1, wb, cos_t, sin_t, min(B_PROJ_TILE, seq))
    k_nat, v_t = _b_kv_up(ckv_p, kr_p, wukv_t, g_kn, tm)
    q_sq = B_NOPE * jnp.max(jnp.abs(wb["g_qn"])) ** 2 + B_ROPE * jnp.max(jnp.abs(wb["g_qr"])) ** 2
    k_sq = B_NOPE * jnp.max(jnp.abs(g_kn)) ** 2 + B_ROPE * jnp.max(jnp.abs(wb["g_kr"])) ** 2
    qk_max = jnp.sqrt(q_sq * k_sq) * BOUND_MARGIN
    oz = lax.cond(2.0 * qk_max <= MAX_LOGIT_RANGE,
                  lambda: _b_attention_bounded(qk_max[None], k_nat, q_t, v_t, zs),
                  lambda: _b_attention(k_nat, q_t, v_t, zs))
    y_p = _out_project(oz, w_out, y_p, mod_p, tm)

    cos_t, sin_t = _rope_tables(past + jnp.arange(SAMPLE_PAD, dtype=jnp.int32))
    q_t, ckv_s, kr_s, zs = _b_project(ys_pad, mod_s, g1, wb, cos_t, sin_t, SAMPLE_PAD)
    k_new, v_new = _b_kv_up(ckv_s, kr_s, wukv_t, g_kn, SAMPLE_PAD)
    k_old, v_old = _b_kv_up(cache_mla_ckv[0], cache_mla_krope[0], wukv_t, g_kn, tm)
    oz = _b_sample_attention(k_old, k_new, v_old, v_new, q_t, zs, dec)
    ys_pad = _out_project(oz, w_out, ys_pad, mod_s, SAMPLE_PAD)

    return (y_p, ys_pad[:, :dec], new_a_k_p, new_a_v_p, new_a_k_s, new_a_v_s,
            ckv_p[None], kr_p[None], ckv_s[None, :, :dec], kr_s[None, :, :dec])
```

```python
import functools

import numpy as np

import jax
import jax.numpy as jnp
from jax import lax
from jax.experimental import pallas as pl
from jax.experimental.pallas import tpu as pltpu

f32 = jnp.float32
bf16 = jnp.bfloat16

EPS = 1e-6
NEG_INF = -1e30
CHUNK = 64
BAND_CHUNKS = 8
WINDOW = BAND_CHUNKS * CHUNK
REL_CLIP = 128
ROPE_THETA = 10000.0
LOG2E = 1.4426950408889634

LANES = 128
N_HEADS = 16
N_PAIRS = N_HEADS // 2
A_HEAD_DIM = 64
B_NOPE = 64
B_ROPE = 32
B_VDIM = 64
B_QK = B_NOPE + B_ROPE
KV_LORA = 256
Q_LORA = 384

Q_GROUP = 256
TILE_CHUNKS = Q_GROUP // CHUNK
MAX_LOGIT_RANGE = 100.0
BOUND_MARGIN = 1.02
A_KEYS = WINDOW + Q_GROUP
A_BIAS_ROWS = A_KEYS + WINDOW
A_BIAS_SPAN = A_BIAS_ROWS + Q_GROUP
SAMPLE_PAD = 128
ROW_TILE = 512
B_PROJ_TILE = 256
B_ITEMS_PER_STEP = 4
PAIRS_PER_STEP = 4
ONES_ROWS = 8
VMEM_LIMIT = 56 * 1024 * 1024


def _silu(x):
    return x * (1.0 / (1.0 + jnp.exp(-x)))


def _nt(a, b):
    return lax.dot_general(a, b, (((1,), (1,)), ((), ())), preferred_element_type=f32)


def _dot(a, b):
    return jnp.dot(a, b, preferred_element_type=f32)


def _rms_rows(blk, gain_col, n):
    ms = jnp.sum(blk * blk, axis=0, keepdims=True) * (1.0 / n)
    return blk * lax.rsqrt(ms + EPS) * gain_col


def _ada_hidden(x, mod_ref, g_ref):
    ms = jnp.mean(x * x, axis=-1, keepdims=True)
    xn = x * lax.rsqrt(ms + EPS) * g_ref[...]
    return (xn * (1.0 + mod_ref[1:2, :]) + mod_ref[0:1, :]).astype(bf16)


def _softmax_cols(s):
    m = jnp.max(s, axis=0, keepdims=True)
    p = jnp.exp2(s - m)
    return p, jnp.sum(p, axis=0, keepdims=True)


def _store_lane_tiles(ref, val):
    for p in range(ref.shape[0]):
        ref[p] = val[:, LANES * p:LANES * (p + 1)]


def _load_lane_tiles(ref):
    return jnp.concatenate([ref[p] for p in range(ref.shape[0])], axis=1)


def _gated(o_t, zs_ref, rows):
    return (o_t.T * zs_ref[rows, :].astype(f32)).astype(bf16)


def _normalise(o_ext, rows):
    return o_ext[:rows] * (1.0 / o_ext[rows:rows + 1])


def _bias_tile(f_row, n_rows, n_cols):
    blocks = []
    for a in range(pl.cdiv(n_rows, Q_GROUP)):
        lo = A_BIAS_ROWS - Q_GROUP * (a + 1)
        x = jnp.broadcast_to(f_row[:, lo:lo + 2 * Q_GROUP], (Q_GROUP, 2 * Q_GROUP))
        y = pltpu.roll(x, Q_GROUP + 1, 1, stride=1, stride_axis=0)[:, :n_cols]
        r = lax.broadcasted_iota(jnp.int32, (Q_GROUP, n_cols), 0) + Q_GROUP * a
        i = lax.broadcasted_iota(jnp.int32, (Q_GROUP, n_cols), 1)
        back = BAND_CHUNKS + i // CHUNK - r // CHUNK
        blocks.append(jnp.where((back >= 0) & (back <= BAND_CHUNKS), y, NEG_INF))
    return jnp.concatenate(blocks, axis=0)[:n_rows]


def _params(*sem):
    return pltpu.CompilerParams(dimension_semantics=sem, vmem_limit_bytes=VMEM_LIMIT)


def _const(shape):
    nd = len(shape)
    return pl.BlockSpec(shape, lambda *_: (0,) * nd)


def _pair_tiles(tm):
    return pl.BlockSpec((None, N_PAIRS, tm, LANES), lambda bi, i: (bi, 0, i, 0))


def _weight(shape):
    nd = len(shape)
    return pl.BlockSpec(shape, lambda *_: (0,) * nd, pipeline_mode=pl.Buffered(1))


def _mod_body(c_ref, w_ref, b_ref, o_ref):
    o_ref[...] = jnp.dot(_silu(c_ref[...]), w_ref[...], preferred_element_type=f32,
                         precision=lax.Precision.HIGHEST) + b_ref[...]


def _modulation(c_all, ada_w, ada_b):
    depth, d, n3 = ada_w.shape
    rows = c_all.shape[0]
    tn = d
    return pl.pallas_call(
        _mod_body,
        grid=(depth, n3 // tn),
        in_specs=[pl.BlockSpec((rows, d), lambda l, j: (0, 0)),
                  pl.BlockSpec((None, d, tn), lambda l, j: (l, 0, j)),
                  pl.BlockSpec((None, 1, tn), lambda l, j: (l, 0, j))],
        out_specs=pl.BlockSpec((None, rows, tn), lambda l, j: (l, 0, j)),
        out_shape=jax.ShapeDtypeStruct((depth, rows, n3), f32),
        compiler_params=_params("arbitrary", "arbitrary"),
        name="modulation",
    )(c_all, ada_w, ada_b.reshape(depth, 1, n3))


def _a_proj_body(n_skip, x_ref, mod_ref, g_ref, wq_ref, wk_ref, wv_ref, wz_ref, gq_ref, gk_ref,
                 qT_ref, k_ref, vT_ref, zs_ref, k32_ref, v32_ref, kf_ref):
    hb = _ada_hidden(x_ref[...], mod_ref, g_ref)
    z = _dot(hb, wz_ref[...])
    kT = _nt(wk_ref[...], hb)
    _store_lane_tiles(zs_ref, _silu(z).astype(bf16))
    qT = _nt(wq_ref[...], hb)
    gq = gq_ref[...]
    gk = gk_ref[...]
    hd = A_HEAD_DIM
    pairs = [slice(2 * hd * p, 2 * hd * (p + 1)) for p in range(N_HEADS // 2)]
    for p, rows in enumerate(pairs):
        k_pair = [_rms_rows(kT[hd * h:hd * (h + 1)], gk, hd) for h in (2 * p, 2 * p + 1)]
        k_nat = jnp.concatenate(k_pair, axis=0).T
        k_ref[p] = k_nat.astype(bf16)
        kf_ref[:, rows] = k_nat
    vT = _nt(wv_ref[...], hb)
    for p, rows in enumerate(pairs):
        q_pair = [_rms_rows(qT[hd * h:hd * (h + 1)], gq, hd) for h in (2 * p, 2 * p + 1)]
        qT_ref[rows, :] = jnp.concatenate(q_pair, axis=0).astype(bf16)
    vT_ref[...] = vT.astype(bf16)

    @pl.when(pl.program_id(1) >= n_skip)
    def _():
        k32_ref[...] = kf_ref[...]
        v32_ref[...] = vT.T


def _a_project(x, mod, g, wq_t, wk_t, wv_t, wz, gq, gk, tm, cache_rows):
    b, l, d = x.shape
    n_skip = (l - cache_rows) // tm
    tile = pl.BlockSpec((None, tm, d), lambda bi, i: (bi, i, 0))
    tile_t = pl.BlockSpec((None, d, tm), lambda bi, i: (bi, 0, i))
    cache = pl.BlockSpec((None, tm, d), lambda bi, i: (bi, jnp.maximum(i - n_skip, 0), 0))
    return pl.pallas_call(
        functools.partial(_a_proj_body, n_skip),
        grid=(b, l // tm),
        in_specs=[tile, pl.BlockSpec((None, 3, d), lambda bi, i: (bi, 0, 0)), _const((1, d)),
                  _weight((d, d)), _weight((d, d)), _weight((d, d)), _weight((d, d)),
                  _const((A_HEAD_DIM, 1)), _const((A_HEAD_DIM, 1))],
        out_specs=[tile_t, _pair_tiles(tm), tile_t, _pair_tiles(tm), cache, cache],
        out_shape=[jax.ShapeDtypeStruct((b, d, l), bf16), jax.ShapeDtypeStruct((b, N_PAIRS, l, LANES), bf16),
                   jax.ShapeDtypeStruct((b, d, l), bf16), jax.ShapeDtypeStruct((b, N_PAIRS, l, LANES), bf16),
                   jax.ShapeDtypeStruct((b, cache_rows, d), f32),
                   jax.ShapeDtypeStruct((b, cache_rows, d), f32)],
        scratch_shapes=[pltpu.VMEM((tm, d), f32)],
        compiler_params=_params("arbitrary", "arbitrary"),
        name="a_project",
    )(x, mod, g, wq_t, wk_t, wv_t, wz, gq, gk)


def _a_attn_body(seq, f_ref, k_ref, qT_ref, vT_ref, zs_ref, o_ref, e_ref, s0, s1, m0, m1, o0, o1):
    hd = A_HEAD_DIM
    n_groups = seq // Q_GROUP
    s_buf, m_buf, o_buf = (s0, s1), (m0, m1), (o0, o1)

    @pl.when(pl.program_id(1) == 0)
    def _():
        for hh in range(2):
            e_ref[hh] = _bias_tile(f_ref[hh], A_BIAS_ROWS, Q_GROUP)

    zeros = jnp.zeros((hd, Q_GROUP), bf16)
    ones = jnp.ones((ONES_ROWS, A_KEYS), bf16)

    def offsets(g):
        q0 = pl.multiple_of(g * Q_GROUP, Q_GROUP)
        w0 = pl.multiple_of(jnp.maximum(g * Q_GROUP - WINDOW, 0), Q_GROUP)
        e0 = pl.multiple_of(WINDOW - (g * Q_GROUP - w0), Q_GROUP)
        return q0, w0, e0

    def scores(g, par):
        q0, w0, e0 = offsets(g)
        k_win = k_ref[pl.ds(w0, A_KEYS), :]
        for hh in range(2):
            qh = qT_ref[hd * hh:hd * (hh + 1), pl.ds(q0, Q_GROUP)]
            w = jnp.concatenate([qh, zeros] if hh == 0 else [zeros, qh], axis=0)
            s = _dot(k_win, w) + e_ref[hh, pl.ds(e0, A_KEYS), :]
            s_buf[par][hh] = s
            m_buf[par][hh] = jnp.max(s, axis=0, keepdims=True)

    def values(g, par):
        _, w0, _ = offsets(g)
        for hh in range(2):
            p = jnp.exp2(s_buf[par][hh] - m_buf[par][hh]).astype(bf16)
            v_ext = jnp.concatenate([vT_ref[hd * hh:hd * (hh + 1), pl.ds(w0, A_KEYS)], ones], axis=0)
            o_buf[par][hh] = _dot(v_ext, p)

    def finish(g, par):
        q0, _, _ = offsets(g)
        outs = [_normalise(o_buf[par][hh], hd) for hh in range(2)]
        rows = pl.ds(q0, Q_GROUP)
        o_ref[rows, :] = _gated(jnp.concatenate(outs, axis=0), zs_ref, rows)

    scores(0, 0)
    scores(1, 1)
    values(0, 0)

    def body(i, carry):
        g = 2 * i + 1
        scores(g + 1, 0)
        values(g, 1)
        finish(g - 1, 0)
        scores(g + 2, 1)
        values(g + 1, 0)
        finish(g, 1)
        return carry

    lax.fori_loop(0, (n_groups - 2) // 2, body, 0)
    values(n_groups - 1, 1)
    finish(n_groups - 2, 0)
    finish(n_groups - 1, 1)


def _a_attention(f_rows, k_nat, q_t, v_t, zs):
    b, n_pairs, l, pair = k_nat.shape
    slab = pl.BlockSpec((None, None, l, pair), lambda p, bi: (bi, p, 0, 0))
    scratch = ([pltpu.VMEM((2, A_BIAS_ROWS, Q_GROUP), f32)]
               + [pltpu.VMEM((2, A_KEYS, Q_GROUP), f32)] * 2
               + [pltpu.VMEM((2, 1, Q_GROUP), f32)] * 2
               + [pltpu.VMEM((2, A_HEAD_DIM + ONES_ROWS, Q_GROUP), f32)] * 2)
    return pl.pallas_call(
        functools.partial(_a_attn_body, l),
        grid=(n_pairs, b),
        in_specs=[pl.BlockSpec((2, 1, A_BIAS_SPAN), lambda p, bi: (p, 0, 0)),
                  slab,
                  pl.BlockSpec((None, pair, l), lambda p, bi: (bi, p, 0)),
                  pl.BlockSpec((None, pair, l), lambda p, bi: (bi, p, 0)),
                  slab],
        out_specs=slab,
        out_shape=jax.ShapeDtypeStruct((b, n_pairs, l, pair), bf16),
        scratch_shapes=scratch,
        compiler_params=_params("arbitrary", "arbitrary"),
        name="a_attention",
    )(f_rows, k_nat, q_t, v_t, zs)


def _a_attn_bounded_body(seq, shift_ref, f_ref, k_ref, qT_ref, vT_ref, zs_ref, o_ref, e_ref, p0, p1, o0, o1):
    hd = A_HEAD_DIM
    n_groups = seq // Q_GROUP
    n_pairs = k_ref.shape[0]
    n_items = n_pairs * n_groups
    p_buf, o_buf = (p0, p1), (o0, o1)

    @pl.when(pl.program_id(1) == 0)
    def _():
        for h in range(2 * n_pairs):
            e_ref[h] = _bias_tile(f_ref[h], A_BIAS_ROWS, Q_GROUP) - shift_ref[0]

    zeros = jnp.zeros((hd, Q_GROUP), bf16)
    ones = jnp.ones((ONES_ROWS, A_KEYS), bf16)

    def offsets(n):
        pp = n // n_groups
        g = n - pp * n_groups
        q0 = pl.multiple_of(g * Q_GROUP, Q_GROUP)
        w0 = pl.multiple_of(jnp.maximum(g * Q_GROUP - WINDOW, 0), Q_GROUP)
        e0 = pl.multiple_of(WINDOW - (g * Q_GROUP - w0), Q_GROUP)
        return pp, q0, w0, e0

    def probs(n, par):
        pp, q0, w0, e0 = offsets(n)
        k_win = k_ref[pp, pl.ds(w0, A_KEYS), :]
        for hh in range(2):
            r0 = pl.multiple_of((2 * pp + hh) * hd, hd)
            qh = qT_ref[pl.ds(r0, hd), pl.ds(q0, Q_GROUP)]
            w = jnp.concatenate([qh, zeros] if hh == 0 else [zeros, qh], axis=0)
            bias = e_ref[2 * pp + hh, pl.ds(e0, A_KEYS), :]
            p_buf[par][hh] = jnp.exp2(_dot(k_win, w) + bias).astype(bf16)

    def values(n, par):
        pp, _, w0, _ = offsets(n)
        for hh in range(2):
            r0 = pl.multiple_of((2 * pp + hh) * hd, hd)
            v_ext = jnp.concatenate([vT_ref[pl.ds(r0, hd), pl.ds(w0, A_KEYS)], ones], axis=0)
            o_buf[par][hh] = _dot(v_ext, p_buf[par][hh])

    def finish(n, par):
        pp, q0, _, _ = offsets(n)
        outs = [_normalise(o_buf[par][hh], hd) for hh in range(2)]
        rows = pl.ds(q0, Q_GROUP)
        o_ref[pp, rows, :] = _gated(jnp.concatenate(outs, axis=0), zs_ref.at[pp], rows)

    probs(0, 0)
    probs(1, 1)
    values(0, 0)

    def body(i, carry):
        n = 2 * i + 1
        probs(n + 1, 0)
        values(n, 1)
        finish(n - 1, 0)
        probs(n + 2, 1)
        values(n + 1, 0)
        finish(n, 1)
        return carry

    lax.fori_loop(0, (n_items - 2) // 2, body, 0)
    values(n_items - 1, 1)
    finish(n_items - 2, 0)
    finish(n_items - 1, 1)


def _a_attention_bounded(shift, f_rows, k_nat, q_t, v_t, zs):
    b, n_pairs, l, pair = k_nat.shape
    pps = PAIRS_PER_STEP
    slab = pl.BlockSpec((None, pps, l, pair), lambda p, bi, sh: (bi, p, 0, 0))
    scratch = ([pltpu.VMEM((2 * pps, A_BIAS_ROWS, Q_GROUP), f32)]
               + [pltpu.VMEM((2, A_KEYS, Q_GROUP), bf16)] * 2
               + [pltpu.VMEM((2, A_HEAD_DIM + ONES_ROWS, Q_GROUP), f32)] * 2)
    grid_spec = pltpu.PrefetchScalarGridSpec(
        num_scalar_prefetch=1,
        grid=(n_pairs // pps, b),
        in_specs=[pl.BlockSpec((2 * pps, 1, A_BIAS_SPAN), lambda p, bi, sh: (p, 0, 0)),
                  slab,
                  pl.BlockSpec((None, pps * pair, l), lambda p, bi, sh: (bi, p, 0)),
                  pl.BlockSpec((None, pps * pair, l), lambda p, bi, sh: (bi, p, 0)),
                  slab],
        out_specs=slab,
        scratch_shapes=scratch)
    return pl.pallas_call(
        functools.partial(_a_attn_bounded_body, l),
        grid_spec=grid_spec,
        out_shape=jax.ShapeDtypeStruct((b, n_pairs, l, pair), bf16),
        compiler_params=_params("arbitrary", "arbitrary"),
        name="a_attention_bounded",
    )(shift, f_rows, k_nat, q_t, v_t, zs)


def _a_sattn_body(n_valid, f_ref, kc_ref, kn_ref, vc_ref, vn_ref, qT_ref, zs_ref, o_ref, e_ref):
    hd = A_HEAD_DIM
    n_keys = e_ref.shape[1]

    @pl.when(pl.program_id(1) == 0)
    def _():
        live = lax.broadcasted_iota(jnp.int32, (n_keys, SAMPLE_PAD), 0) < n_valid
        for hh in range(2):
            e_ref[hh] = jnp.where(live, _bias_tile(f_ref[hh], n_keys, SAMPLE_PAD), NEG_INF)

    zeros = jnp.zeros((hd, SAMPLE_PAD), bf16)
    k_all = jnp.concatenate([kc_ref[...].astype(bf16), kn_ref[...]], axis=0)
    v_t = jnp.concatenate([vc_ref[...], vn_ref[...]], axis=0).T.astype(bf16)
    outs = []
    for hh in range(2):
        qh = qT_ref[hd * hh:hd * (hh + 1), :]
        w = jnp.concatenate([qh, zeros] if hh == 0 else [zeros, qh], axis=0)
        p, l = _softmax_cols(_dot(k_all, w) + e_ref[hh])
        outs.append(_dot(v_t[hd * hh:hd * (hh + 1)], p.astype(bf16)) * (1.0 / l))
    o_ref[...] = _gated(jnp.concatenate(outs, axis=0), zs_ref, slice(None))


def _a_sample_attention(f_rows, k_cache, k_new, v_cache, v_new, q_t, zs, n_new):
    b, n_cache, d = k_cache.shape
    pair = 2 * A_HEAD_DIM
    n_keys = n_cache + SAMPLE_PAD
    new = pl.BlockSpec((None, SAMPLE_PAD, pair), lambda p, bi: (bi, 0, p))
    old = pl.BlockSpec((None, n_cache, pair), lambda p, bi: (bi, 0, p))
    slab = pl.BlockSpec((None, None, SAMPLE_PAD, pair), lambda p, bi: (bi, p, 0, 0))
    return pl.pallas_call(
        functools.partial(_a_sattn_body, n_cache + n_new),
        grid=(d // pair, b),
        in_specs=[pl.BlockSpec((2, 1, A_BIAS_SPAN), lambda p, bi: (p, 0, 0)),
                  old, slab, old, new,
                  pl.BlockSpec((None, pair, SAMPLE_PAD), lambda p, bi: (bi, p, 0)),
                  slab],
        out_specs=slab,
        out_shape=jax.ShapeDtypeStruct((b, d // pair, SAMPLE_PAD, pair), bf16),
        scratch_shapes=[pltpu.VMEM((2, n_keys, SAMPLE_PAD), f32)],
        compiler_params=_params("arbitrary", "arbitrary"),
        name="a_sample_attention",
    )(f_rows, k_cache, k_new, v_cache, v_new, q_t, zs)


def _out_body(oz_ref, w_ref, x_ref, mod_ref, y_ref):
    y = _dot(_load_lane_tiles(oz_ref), w_ref[...])
    y_ref[...] = x_ref[...] + mod_ref[2:3, :] * y


def _out_project(oz, w_out, x, mod, tm):
    b, l, d = x.shape
    tile = pl.BlockSpec((None, tm, d), lambda bi, i: (bi, i, 0))
    return pl.pallas_call(
        _out_body,
        grid=(b, l // tm),
        in_specs=[_pair_tiles(tm), _weight(w_out.shape), tile,
                  pl.BlockSpec((None, 3, d), lambda bi, i: (bi, 0, 0))],
        out_specs=tile,
        out_shape=jax.ShapeDtypeStruct((b, l, d), f32),
        compiler_params=_params("arbitrary", "arbitrary"),
        name="out_project",
    )(oz, w_out, x, mod)


def _rotate(x, cos, sin):
    half = B_ROPE // 2
    x1, x2 = x[:half], x[half:]
    return jnp.concatenate([x1 * cos - x2 * sin, x2 * cos + x1 * sin], axis=0)


def _b_proj_body(oz_ref, wout_ref, x_ref, mod_prev_ref, mod_ref, g_ref,
                 wcq_ref, wckv_ref, wkr_ref, wz_ref, wuq_ref,
                 gcq_ref, gckv_ref, gkr_ref, gqn_ref, gqr_ref, cos_ref, sin_ref,
                 y_ref, qT_ref, ckv_ref, kr_ref, zs_ref):
    y = x_ref[...] + mod_prev_ref[2:3, :] * _dot(_load_lane_tiles(oz_ref), wout_ref[...])
    y_ref[...] = y
    hb = _ada_hidden(y, mod_ref, g_ref)
    tm = hb.shape[0]
    cos = cos_ref[...]
    sin = sin_ref[...]
    cq_raw = _nt(wcq_ref[...], hb)
    z = _dot(hb, wz_ref[...])
    ckv_raw = _nt(wckv_ref[...], hb)
    kr_raw = _nt(wkr_ref[...], hb)
    cq = _rms_rows(cq_raw, gcq_ref[...], Q_LORA).astype(bf16)
    qT = _dot(wuq_ref[...], cq)
    _store_lane_tiles(zs_ref, _silu(z).astype(bf16))
    ckv_ref[...] = _rms_rows(ckv_raw, gckv_ref[...], KV_LORA).T
    kr = _rotate(_rms_rows(kr_raw, gkr_ref[...], B_ROPE), cos, sin)
    kr_pad = jnp.concatenate([kr, jnp.zeros((128 - B_ROPE, tm), f32)], axis=0)
    kr_ref[...] = kr_pad.T[:, :B_ROPE]
    gqn = gqn_ref[...]
    gqr = gqr_ref[...]
    for h in range(N_HEADS):
        base = B_QK * h
        nope = _rms_rows(qT[base:base + B_NOPE], gqn, B_NOPE)
        rope = _rotate(_rms_rows(qT[base + B_NOPE:base + B_QK], gqr, B_ROPE), cos, sin)
        qT_ref[base:base + B_QK, :] = jnp.concatenate([nope, rope], axis=0).astype(bf16)


def _b_project(oz_prev, w_out_prev, x, mod_prev, mod, g, w, cos_t, sin_t, tm):
    b, l, d = x.shape
    tile = pl.BlockSpec((None, tm, d), lambda bi, i: (bi, i, 0))
    mod_spec = pl.BlockSpec((None, 3, d), lambda bi, i: (bi, 0, 0))
    rot = pl.BlockSpec((B_ROPE // 2, tm), lambda bi, i: (0, i))
    nq = N_HEADS * B_QK
    return pl.pallas_call(
        _b_proj_body,
        grid=(b, l // tm),
        in_specs=[_pair_tiles(tm), _weight(w_out_prev.shape), tile, mod_spec, mod_spec, _const((1, d)),
                  _weight((Q_LORA, d)), _weight((KV_LORA, d)), _weight((B_ROPE, d)), _weight((d, d)),
                  _weight((nq, Q_LORA)),
                  _const((Q_LORA, 1)), _const((KV_LORA, 1)), _const((B_ROPE, 1)),
                  _const((B_NOPE, 1)), _const((B_ROPE, 1)), rot, rot],
        out_specs=[tile,
                   pl.BlockSpec((None, nq, tm), lambda bi, i: (bi, 0, i)),
                   pl.BlockSpec((None, tm, KV_LORA), lambda bi, i: (bi, i, 0)),
                   pl.BlockSpec((None, tm, B_ROPE), lambda bi, i: (bi, i, 0)),
                   _pair_tiles(tm)],
        out_shape=[jax.ShapeDtypeStruct((b, l, d), f32),
                   jax.ShapeDtypeStruct((b, nq, l), bf16),
                   jax.ShapeDtypeStruct((b, l, KV_LORA), f32),
                   jax.ShapeDtypeStruct((b, l, B_ROPE), f32),
                   jax.ShapeDtypeStruct((b, N_PAIRS, l, LANES), bf16)],
        compiler_params=_params("arbitrary", "arbitrary"),
        name="b_project",
    )(oz_prev, w_out_prev, x, mod_prev, mod, g, w["cq_t"], w["ckv_t"], w["kr_t"], w["z"], w["uq_t"],
      w["g_cq"], w["g_ckv"], w["g_kr"], w["g_qn"], w["g_qr"], cos_t, sin_t)


def _b_kvup_body(ckv_ref, kr_ref, wukv_ref, gkn_ref, k_ref, vT_ref):
    tm = ckv_ref.shape[0]
    kvT = _nt(wukv_ref[...], ckv_ref[...].astype(bf16))
    eye = (lax.broadcasted_iota(jnp.int32, (B_ROPE, B_ROPE), 0)
           == lax.broadcasted_iota(jnp.int32, (B_ROPE, B_ROPE), 1)).astype(bf16)
    krT = _nt(eye, kr_ref[...].astype(bf16))
    key_chunk = (lax.broadcasted_iota(jnp.int32, (128 - B_QK, tm), 1) // CHUNK) % TILE_CHUNKS
    spare_row = lax.broadcasted_iota(jnp.int32, (128 - B_QK, tm), 0)
    chunk_rows = ((spare_row == key_chunk) | (spare_row == TILE_CHUNKS)).astype(f32)
    tail = jnp.concatenate([krT, chunk_rows], axis=0)
    gkn = gkn_ref[...]
    per_head = B_NOPE + B_VDIM
    for h in range(N_HEADS):
        base = per_head * h
        kn = _rms_rows(kvT[base:base + B_NOPE], gkn, B_NOPE)
        k_ref[h] = jnp.concatenate([kn, tail], axis=0).T.astype(bf16)
        vT_ref[B_VDIM * h:B_VDIM * (h + 1), :] = kvT[base + B_NOPE:base + per_head].astype(bf16)


def _b_kv_up(ckv, kr, wukv_t, g_kn, tm):
    b, l, _ = ckv.shape
    return pl.pallas_call(
        _b_kvup_body,
        grid=(b, l // tm),
        in_specs=[pl.BlockSpec((None, tm, KV_LORA), lambda bi, i: (bi, i, 0)),
                  pl.BlockSpec((None, tm, B_ROPE), lambda bi, i: (bi, i, 0)),
                  _const(wukv_t.shape), _const((B_NOPE, 1))],
        out_specs=[pl.BlockSpec((None, N_HEADS, tm, 128), lambda bi, i: (bi, 0, i, 0)),
                   pl.BlockSpec((None, N_HEADS * B_VDIM, tm), lambda bi, i: (bi, 0, i))],
        out_shape=[jax.ShapeDtypeStruct((b, N_HEADS, l, 128), bf16),
                   jax.ShapeDtypeStruct((b, N_HEADS * B_VDIM, l), bf16)],
        compiler_params=_params("arbitrary", "arbitrary"),
        name="b_kv_up",
    )(ckv, kr, wukv_t, g_kn)


def _b_item_table(n_tiles, n_pairs=1):
    rows = [(pp, t, j, int(j == t)) for pp in range(n_pairs) for t in range(n_tiles) for j in range(t, -1, -1)]
    return np.asarray(rows, np.int32).T.copy()


def _b_items_per_step(n_items, odd_steps):
    return max(u for u in range(1, B_ITEMS_PER_STEP + 1)
               if n_items % u == 0 and n_items // u >= 3 and (not odd_steps or (n_items // u) % 2 == 1))


def _b_attn_body(seq, per, tab_ref, k_ref, qT_ref, vT_ref, zs_ref, o_ref,
                 s0, s1, c0, c1, pv0, pv1, a0, a1, m_st, acc_st):
    t_sz = Q_GROUP
    n_tiles = seq // t_sz
    n_steps = n_tiles * (n_tiles + 1) // 2 // per
    s_buf, c_buf, pv_buf, a_buf = (s0, s1), (c0, c1), (pv0, pv1), (a0, a1)
    ones = jnp.ones((ONES_ROWS, t_sz), bf16)
    key_chunk = lax.broadcasted_iota(jnp.int32, (128 - B_QK, t_sz), 0)
    qry_chunk = lax.broadcasted_iota(jnp.int32, (128 - B_QK, t_sz), 1) // CHUNK
    mask_rows = jnp.where((key_chunk < t_sz // CHUNK) & (key_chunk > qry_chunk), NEG_INF, 0.0).astype(bf16)
    zero_rows = jnp.zeros((128 - B_QK, t_sz), bf16)
    m_st[...] = jnp.full(m_st.shape, NEG_INF, f32)
    acc_st[...] = jnp.zeros(acc_st.shape, f32)

    def item(step, u):
        n = step * per + u
        return tab_ref[1, n], tab_ref[2, n], tab_ref[3, n] == 1

    def scores(step, par, u):
        t, j, diag = item(step, u)
        q0 = pl.multiple_of(t * t_sz, t_sz)
        k0 = pl.multiple_of(j * t_sz, t_sz)
        pad = jnp.where(diag, mask_rows, zero_rows)
        for hh in range(2):
            w = jnp.concatenate([qT_ref[B_QK * hh:B_QK * (hh + 1), pl.ds(q0, t_sz)], pad], axis=0)
            s = _dot(k_ref[hh, pl.ds(k0, t_sz), :], w)
            s_buf[par][u, hh] = s
            c_buf[par][u, hh] = jnp.max(s, axis=0, keepdims=True)

    def values(step, par, u):
        t, j, _ = item(step, u)
        k0 = pl.multiple_of(j * t_sz, t_sz)
        for hh in range(2):
            m_old = m_st[t, hh]
            m_new = jnp.maximum(m_old, c_buf[par][u, hh])
            m_st[t, hh] = m_new
            a_buf[par][u, hh] = jnp.exp2(m_old - m_new)
            p = jnp.exp2(s_buf[par][u, hh] - m_new).astype(bf16)
            v_ext = jnp.concatenate([vT_ref[B_VDIM * hh:B_VDIM * (hh + 1), pl.ds(k0, t_sz)], ones], axis=0)
            pv_buf[par][u, hh] = _dot(v_ext, p)

    def accumulate(step, par, u):
        t, _, _ = item(step, u)
        for hh in range(2):
            acc_st[t, hh] = a_buf[par][u, hh] * acc_st[t, hh] + pv_buf[par][u, hh]

    def time_step(tau, par, first=False, drain=0):
        for u in range(per):
            if drain == 0:
                scores(tau + 1, 1 - par, u)
            if drain <= 1:
                values(tau, par, u)
            if not first:
                accumulate(tau - 1, 1 - par, u)

    for u in range(per):
        scores(0, 0, u)
    time_step(0, 0, first=True)

    def body(i, carry):
        tau = 2 * i + 1
        time_step(tau, 1)
        time_step(tau + 1, 0)
        return carry

    lax.fori_loop(0, (n_steps - 3) // 2, body, 0)
    time_step(n_steps - 2, 1)
    time_step(n_steps - 1, 0, drain=1)
    time_step(n_steps, 1, drain=2)
    for t in range(n_tiles):
        outs = [_normalise(acc_st[t, hh], B_VDIM) for hh in range(2)]
        rows = slice(t * t_sz, (t + 1) * t_sz)
        o_ref[rows, :] = _gated(jnp.concatenate(outs, axis=0), zs_ref, rows)


def _b_attention(k_nat, q_t, v_t, zs):
    b, _, l, _ = k_nat.shape
    t_sz = Q_GROUP
    n_tiles = l // t_sz
    table = _b_item_table(n_tiles)
    per = _b_items_per_step(table.shape[1], odd_steps=True)
    ext = B_VDIM + ONES_ROWS
    scratch = ([pltpu.VMEM((per, 2, t_sz, t_sz), f32)] * 2 + [pltpu.VMEM((per, 2, 1, t_sz), f32)] * 2
               + [pltpu.VMEM((per, 2, ext, t_sz), f32)] * 2 + [pltpu.VMEM((per, 2, 1, t_sz), f32)] * 2
               + [pltpu.VMEM((n_tiles, 2, 1, t_sz), f32), pltpu.VMEM((n_tiles, 2, ext, t_sz), f32)])
    grid_spec = pltpu.PrefetchScalarGridSpec(
        num_scalar_prefetch=1,
        grid=(b, N_HEADS // 2),
        in_specs=[pl.BlockSpec((None, 2, l, 128), lambda bi, p, tab: (bi, p, 0, 0)),
                  pl.BlockSpec((None, 2 * B_QK, l), lambda bi, p, tab: (bi, p, 0)),
                  pl.BlockSpec((None, 2 * B_VDIM, l), lambda bi, p, tab: (bi, p, 0)),
                  pl.BlockSpec((None, None, l, LANES), lambda bi, p, tab: (bi, p, 0, 0))],
        out_specs=pl.BlockSpec((None, None, l, LANES), lambda bi, p, tab: (bi, p, 0, 0)),
        scratch_shapes=scratch)
    return pl.pallas_call(
        functools.partial(_b_attn_body, l, per),
        grid_spec=grid_spec,
        out_shape=jax.ShapeDtypeStruct((b, N_PAIRS, l, LANES), bf16),
        compiler_params=_params("arbitrary", "arbitrary"),
        name="b_attention",
    )(jnp.asarray(table), k_nat, q_t, v_t, zs)


def _b_attn_bounded_body(seq, per, tab_ref, shift_ref, k_ref, qT_ref, vT_ref, zs_ref, o_ref, p0, p1, acc_st):
    t_sz = Q_GROUP
    n_tiles = seq // t_sz
    n_pairs = acc_st.shape[0]
    n_steps = n_pairs * n_tiles * (n_tiles + 1) // 2 // per
    p_buf = (p0, p1)
    ones = jnp.ones((ONES_ROWS, t_sz), bf16)
    row = lax.broadcasted_iota(jnp.int32, (128 - B_QK, t_sz), 0)
    qry_chunk = lax.broadcasted_iota(jnp.int32, (128 - B_QK, t_sz), 1) // CHUNK
    shift_rows = jnp.where(row == TILE_CHUNKS, -shift_ref[0], 0.0)
    mask_rows = (jnp.where((row < TILE_CHUNKS) & (row > qry_chunk), NEG_INF, 0.0) + shift_rows).astype(bf16)
    plain_rows = shift_rows.astype(bf16)
    acc_st[...] = jnp.zeros(acc_st.shape, f32)

    def item(step, u):
        n = step * per + u
        return tab_ref[0, n], tab_ref[1, n], tab_ref[2, n], tab_ref[3, n] == 1

    def probs(step, par, u):
        pp, t, j, diag = item(step, u)
        q0 = pl.multiple_of(t * t_sz, t_sz)
        k0 = pl.multiple_of(j * t_sz, t_sz)
        pad = jnp.where(diag, mask_rows, plain_rows)
        for hh in range(2):
            r0 = pl.multiple_of((2 * pp + hh) * B_QK, B_ROPE)
            w = jnp.concatenate([qT_ref[pl.ds(r0, B_QK), pl.ds(q0, t_sz)], pad], axis=0)
            p_buf[par][u, hh] = jnp.exp2(_dot(k_ref[2 * pp + hh, pl.ds(k0, t_sz), :], w)).astype(bf16)

    def values(step, par, u):
        pp, t, j, _ = item(step, u)
        k0 = pl.multiple_of(j * t_sz, t_sz)
        for hh in range(2):
            r0 = pl.multiple_of((2 * pp + hh) * B_VDIM, B_VDIM)
            v_ext = jnp.concatenate([vT_ref[pl.ds(r0, B_VDIM), pl.ds(k0, t_sz)], ones], axis=0)
            acc_st[pp, t, hh] += _dot(v_ext, p_buf[par][u, hh])

    def time_step(tau, last):
        par = tau % 2
        for u in range(per):
            if not last:
                probs(tau + 1, 1 - par, u)
            values(tau, par, u)

    def step_pair(i, carry):
        for par in range(2):
            for u in range(per):
                probs(2 * i + par + 1, 1 - par, u)
                values(2 * i + par, par, u)
        return carry

    for u in range(per):
        probs(0, 0, u)
    n_loop = (n_steps - 1) // 2
    lax.fori_loop(0, n_loop, step_pair, 0)
    for tau in range(2 * n_loop, n_steps):
        time_step(tau, last=tau == n_steps - 1)
    for pp in range(n_pairs):
        for t in range(n_tiles):
            outs = [_normalise(acc_st[pp, t, hh], B_VDIM) for hh in range(2)]
            rows = slice(t * t_sz, (t + 1) * t_sz)
            o_ref[pp, rows, :] = _gated(jnp.concatenate(outs, axis=0), zs_ref.at[pp], rows)


def _b_attention_bounded(shift, k_nat, q_t, v_t, zs):
    b, _, l, _ = k_nat.shape
    t_sz = Q_GROUP
    n_tiles = l // t_sz
    pps = PAIRS_PER_STEP
    table = _b_item_table(n_tiles, pps)
    per = _b_items_per_step(table.shape[1], odd_steps=False)
    scratch = ([pltpu.VMEM((per, 2, t_sz, t_sz), bf16)] * 2
               + [pltpu.VMEM((pps, n_tiles, 2, B_VDIM + ONES_ROWS, t_sz), f32)])
    slab = pl.BlockSpec((None, pps, l, LANES), lambda bi, p, tab, sh: (bi, p, 0, 0))
    grid_spec = pltpu.PrefetchScalarGridSpec(
        num_scalar_prefetch=2,
        grid=(b, N_PAIRS // pps),
        in_specs=[pl.BlockSpec((None, 2 * pps, l, 128), lambda bi, p, tab, sh: (bi, p, 0, 0)),
                  pl.BlockSpec((None, 2 * pps * B_QK, l), lambda bi, p, tab, sh: (bi, p, 0)),
                  pl.BlockSpec((None, 2 * pps * B_VDIM, l), lambda bi, p, tab, sh: (bi, p, 0)),
                  slab],
        out_specs=slab,
        scratch_shapes=scratch)
    return pl.pallas_call(
        functools.partial(_b_attn_bounded_body, l, per),
        grid_spec=grid_spec,
        out_shape=jax.ShapeDtypeStruct((b, N_PAIRS, l, LANES), bf16),
        compiler_params=_params("arbitrary", "arbitrary"),
        name="b_attention_bounded",
    )(jnp.asarray(table), shift, k_nat, q_t, v_t, zs)


def _b_sattn_body(n_new, kc_ref, kn_ref, vc_ref, vn_ref, qT_ref, zs_ref, o_ref):
    t = SAMPLE_PAD
    zpad = jnp.zeros((128 - B_QK, t), bf16)
    new_mask = jnp.where(lax.broadcasted_iota(jnp.int32, (t, t), 0) < n_new, 0.0, NEG_INF).astype(f32)
    outs = []
    for hh in range(2):
        w = jnp.concatenate([qT_ref[B_QK * hh:B_QK * (hh + 1), :], zpad], axis=0)
        s_c = _dot(kc_ref[hh], w)
        s_n = _dot(kn_ref[hh], w) + new_mask
        m = jnp.maximum(jnp.max(s_c, axis=0, keepdims=True), jnp.max(s_n, axis=0, keepdims=True))
        p_c = jnp.exp2(s_c - m)
        p_n = jnp.exp2(s_n - m)
        l = jnp.sum(p_c, axis=0, keepdims=True) + jnp.sum(p_n, axis=0, keepdims=True)
        rows = slice(B_VDIM * hh, B_VDIM * (hh + 1))
        o_t = _dot(vc_ref[rows, :], p_c.astype(bf16)) + _dot(vn_ref[rows, :], p_n.astype(bf16))
        outs.append(o_t * (1.0 / l))
    o_ref[...] = _gated(jnp.concatenate(outs, axis=0), zs_ref, slice(None))


def _b_sample_attention(k_cache, k_new, v_cache, v_new, q_t, zs, n_new):
    b, _, past, _ = k_cache.shape
    t = SAMPLE_PAD
    slab = pl.BlockSpec((None, None, t, LANES), lambda bi, p: (bi, p, 0, 0))
    return pl.pallas_call(
        functools.partial(_b_sattn_body, n_new),
        grid=(b, N_HEADS // 2),
        in_specs=[pl.BlockSpec((None, 2, past, 128), lambda bi, p: (bi, p, 0, 0)),
                  pl.BlockSpec((None, 2, t, 128), lambda bi, p: (bi, p, 0, 0)),
                  pl.BlockSpec((None, 2 * B_VDIM, past), lambda bi, p: (bi, p, 0)),
                  pl.BlockSpec((None, 2 * B_VDIM, t), lambda bi, p: (bi, p, 0)),
                  pl.BlockSpec((None, 2 * B_QK, t), lambda bi, p: (bi, p, 0)),
                  slab],
        out_specs=slab,
        out_shape=jax.ShapeDtypeStruct((b, N_PAIRS, t, LANES), bf16),
        compiler_params=_params("arbitrary", "arbitrary"),
        name="b_sample_attention",
    )(k_cache, k_new, v_cache, v_new, q_t, zs)


def _bias_rows(table):
    left = A_BIAS_ROWS - 1 - WINDOW - REL_CLIP
    right = A_BIAS_SPAN - left - table.shape[-1]
    return (jnp.pad(table.astype(f32), ((0, 0), (left, right)), mode="edge") * LOG2E)[:, None, :]


def _rope_tables(pos):
    half = B_ROPE // 2
    inv = ROPE_THETA ** (-jnp.arange(half, dtype=f32) / half)
    ang = pos.astype(f32)[:, None] * inv[None, :]
    return jnp.cos(ang).T, jnp.sin(ang).T


def _col(g, scale=1.0):
    return (g.astype(f32) * scale)[:, None]


def kernel(x_prompt, x_sample, cache_a_k, cache_a_v, cache_mla_ckv, cache_mla_krope, c_prompt, c_sample, norm_g, ada_w, ada_b, a_w_in, a_g_q, a_g_k, a_rel_bias, a_w_out, b_w_in, b_g_cq, b_w_uq, b_g_ckv, b_w_ukv, b_g_qn, b_g_qr, b_g_kn, b_g_kr, b_w_out):
    bp, seq, d = x_prompt.shape
    bs, dec, _ = x_sample.shape
    past = cache_mla_ckv.shape[2]
    n_cache_a = cache_a_k.shape[2]
    assert d == N_HEADS * A_HEAD_DIM and seq % (2 * Q_GROUP) == 0 and seq >= A_KEYS
    assert past % CHUNK == 0 and dec <= CHUNK and dec <= SAMPLE_PAD and n_cache_a == WINDOW
    cache_rows = min(WINDOW, seq)
    tm = min(ROW_TILE, seq)

    mod = _modulation(jnp.concatenate([c_prompt, c_sample], axis=0), ada_w, ada_b)
    mod = mod.reshape(mod.shape[0], bp + bs, 3, d)
    xs_pad = jnp.pad(x_sample, ((0, 0), (0, SAMPLE_PAD - dec), (0, 0)))

    w_in = a_w_in[0]
    wq_t, wk_t, wv_t = (w_in[:, d * n:d * (n + 1)].T.astype(bf16) for n in range(3))
    wz = w_in[:, 3 * d:].astype(bf16)
    w_out_a = a_w_out[0].astype(bf16)
    gq = _col(a_g_q[0], A_HEAD_DIM ** -0.5 * LOG2E)
    gk = _col(a_g_k[0])
    g0 = norm_g[0][None, :]
    f_rows = _bias_rows(a_rel_bias[0])
    mod_p0, mod_s0 = mod[0, :bp], mod[0, bp:]

    q_t, k_nat, v_t, zs, k32, v32 = _a_project(x_prompt, mod_p0, g0, wq_t, wk_t, wv_t, wz, gq, gk, tm, cache_rows)
    qk_max = A_HEAD_DIM * jnp.max(jnp.abs(gq)) * jnp.max(jnp.abs(gk)) * BOUND_MARGIN
    upper = qk_max + jnp.max(f_rows)
    oz_p0 = lax.cond(upper - (jnp.min(f_rows) - qk_max) <= MAX_LOGIT_RANGE,
                     lambda: _a_attention_bounded(upper[None], f_rows, k_nat, q_t, v_t, zs),
                     lambda: _a_attention(f_rows, k_nat, q_t, v_t, zs))
    new_a_k_p = k32.reshape(1, bp, cache_rows, N_HEADS, A_HEAD_DIM)
    new_a_v_p = v32.reshape(1, bp, cache_rows, N_HEADS, A_HEAD_DIM)

    q_t, k_nat, v_t, zs, k32, v32 = _a_project(xs_pad, mod_s0, g0, wq_t, wk_t, wv_t, wz, gq, gk,
                                               SAMPLE_PAD, SAMPLE_PAD)
    oz_s0 = _a_sample_attention(f_rows, cache_a_k[0].reshape(bs, n_cache_a, d), k_nat,
                                cache_a_v[0].reshape(bs, n_cache_a, d), v32, q_t, zs, dec)
    new_a_k_s = k32[:, :dec].reshape(1, bs, dec, N_HEADS, A_HEAD_DIM)
    new_a_v_s = v32[:, :dec].reshape(1, bs, dec, N_HEADS, A_HEAD_DIM)

    w_in = b_w_in[0]
    scale = B_QK ** -0.5 * LOG2E
    wb = {
        "cq_t": w_in[:, :Q_LORA].T.astype(bf16),
        "ckv_t": w_in[:, Q_LORA:Q_LORA + KV_LORA].T.astype(bf16),
        "kr_t": w_in[:, Q_LORA + KV_LORA:Q_LORA + KV_LORA + B_ROPE].T.astype(bf16),
        "z": w_in[:, Q_LORA + KV_LORA + B_ROPE:].astype(bf16),
        "uq_t": b_w_uq[0].T.astype(bf16),
        "g_cq": _col(b_g_cq[0]), "g_ckv": _col(b_g_ckv[0]), "g_kr": _col(b_g_kr[0]),
        "g_qn": _col(b_g_qn[0], scale), "g_qr": _col(b_g_qr[0], scale),
    }
    wukv_t = b_w_ukv[0].T.astype(bf16)
    g_kn = _col(b_g_kn[0])
    w_out = b_w_out[0].astype(bf16)
    g1 = norm_g[1][None, :]
    mod_p, mod_s = mod[1, :bp], mod[1, bp:]

    cos_t, sin_t = _rope_tables(jnp.arange(seq, dtype=jnp.int32))
    y_p, q_t, ckv_p, kr_p, zs = _b_project(oz_p0, w_out_a, x_prompt, mod_p0, mod_p, g1, wb, cos_t, sin_t,
                                           min(B_PROJ_TILE, seq))
    k_nat, v_t = _b_kv_up(ckv_p, kr_p, wukv_t, g_kn, tm)
    q_sq = B_NOPE * jnp.max(jnp.abs(wb["g_qn"])) ** 2 + B_ROPE * jnp.max(jnp.abs(wb["g_qr"])) ** 2
    k_sq = B_NOPE * jnp.max(jnp.abs(g_kn)) ** 2 + B_ROPE * jnp.max(jnp.abs(wb["g_kr"])) ** 2
    qk_max = jnp.sqrt(q_sq * k_sq) * BOUND_MARGIN
    oz = lax.cond(2.0 * qk_max <= MAX_LOGIT_RANGE,
                  lambda: _b_attention_bounded(qk_max[None], k_nat, q_t, v_t, zs),
                  lambda: _b_attention(k_nat, q_t, v_t, zs))
    y_p = _out_project(oz, w_out, y_p, mod_p, tm)

    cos_t, sin_t = _rope_tables(past + jnp.arange(SAMPLE_PAD, dtype=jnp.int32))
    ys_pad, q_t, ckv_s, kr_s, zs = _b_project(oz_s0, w_out_a, xs_pad, mod_s0, mod_s, g1, wb, cos_t, sin_t,
                                              SAMPLE_PAD)
    k_new, v_new = _b_kv_up(ckv_s, kr_s, wukv_t, g_kn, SAMPLE_PAD)
    k_old, v_old = _b_kv_up(cache_mla_ckv[0], cache_mla_krope[0], wukv_t, g_kn, tm)
    oz = _b_sample_attention(k_old, k_new, v_old, v_new, q_t, zs, dec)
    ys_pad = _out_project(oz, w_out, ys_pad, mod_s, SAMPLE_PAD)

    return (y_p, ys_pad[:, :dec], new_a_k_p, new_a_v_p, new_a_k_s, new_a_v_s,
            ckv_p[None], kr_p[None], ckv_s[None, :, :dec], kr_s[None, :, :dec])
```

```python
import functools

import numpy as np

import jax
import jax.numpy as jnp
from jax import lax
from jax.experimental import pallas as pl
from jax.experimental.pallas import tpu as pltpu

f32 = jnp.float32
bf16 = jnp.bfloat16

EPS = 1e-6
NEG_INF = -1e30
CHUNK = 64
BAND_CHUNKS = 8
WINDOW = BAND_CHUNKS * CHUNK
REL_CLIP = 128
ROPE_THETA = 10000.0
LOG2E = 1.4426950408889634

LANES = 128
N_HEADS = 16
N_PAIRS = N_HEADS // 2
A_HEAD_DIM = 64
B_NOPE = 64
B_ROPE = 32
B_VDIM = 64
B_QK = B_NOPE + B_ROPE
KV_LORA = 256
Q_LORA = 384

Q_GROUP = 256
TILE_CHUNKS = Q_GROUP // CHUNK
MAX_LOGIT_RANGE = 100.0
BOUND_MARGIN = 1.02
A_KEYS = WINDOW + Q_GROUP
A_BIAS_ROWS = A_KEYS + WINDOW
A_BIAS_SPAN = A_BIAS_ROWS + Q_GROUP
SAMPLE_PAD = 128
ROW_TILE = 512
B_PROJ_TILE = 256
B_ITEMS_PER_STEP = 4
PAIRS_PER_STEP = 4
ONES_ROWS = 8
VMEM_LIMIT = 56 * 1024 * 1024


def _silu(x):
    return x * (1.0 / (1.0 + jnp.exp(-x)))


def _nt(a, b):
    return lax.dot_general(a, b, (((1,), (1,)), ((), ())), preferred_element_type=f32)


def _dot(a, b):
    return jnp.dot(a, b, preferred_element_type=f32)


def _rms_rows(blk, gain_col, n):
    ms = jnp.sum(blk * blk, axis=0, keepdims=True) * (1.0 / n)
    return blk * lax.rsqrt(ms + EPS) * gain_col


def _ada_hidden(x, mod_ref, g_ref):
    ms = jnp.mean(x * x, axis=-1, keepdims=True)
    xn = x * lax.rsqrt(ms + EPS) * g_ref[...]
    return (xn * (1.0 + mod_ref[1:2, :]) + mod_ref[0:1, :]).astype(bf16)


def _softmax_cols(s):
    m = jnp.max(s, axis=0, keepdims=True)
    p = jnp.exp2(s - m)
    return p, jnp.sum(p, axis=0, keepdims=True)


def _store_lane_tiles(ref, val):
    for p in range(ref.shape[0]):
        ref[p] = val[:, LANES * p:LANES * (p + 1)]


def _load_lane_tiles(ref):
    return jnp.concatenate([ref[p] for p in range(ref.shape[0])], axis=1)


def _gated(o_t, zs_ref, rows):
    return (o_t.T * zs_ref[rows, :].astype(f32)).astype(bf16)


def _normalise(o_ext, rows):
    return o_ext[:rows] * (1.0 / o_ext[rows:rows + 1])


def _bias_tile(f_row, n_rows, n_cols):
    blocks = []
    for a in range(pl.cdiv(n_rows, Q_GROUP)):
        lo = A_BIAS_ROWS - Q_GROUP * (a + 1)
        x = jnp.broadcast_to(f_row[:, lo:lo + 2 * Q_GROUP], (Q_GROUP, 2 * Q_GROUP))
        y = pltpu.roll(x, Q_GROUP + 1, 1, stride=1, stride_axis=0)[:, :n_cols]
        r = lax.broadcasted_iota(jnp.int32, (Q_GROUP, n_cols), 0) + Q_GROUP * a
        i = lax.broadcasted_iota(jnp.int32, (Q_GROUP, n_cols), 1)
        back = BAND_CHUNKS + i // CHUNK - r // CHUNK
        blocks.append(jnp.where((back >= 0) & (back <= BAND_CHUNKS), y, NEG_INF))
    return jnp.concatenate(blocks, axis=0)[:n_rows]


def _params(*sem):
    return pltpu.CompilerParams(dimension_semantics=sem, vmem_limit_bytes=VMEM_LIMIT)


def _const(shape):
    nd = len(shape)
    return pl.BlockSpec(shape, lambda *_: (0,) * nd)


def _pair_tiles(tm):
    return pl.BlockSpec((None, N_PAIRS, tm, LANES), lambda bi, i: (bi, 0, i, 0))


def _weight(shape):
    nd = len(shape)
    return pl.BlockSpec(shape, lambda *_: (0,) * nd, pipeline_mode=pl.Buffered(1))


def _mod_body(c_ref, w_ref, b_ref, o_ref):
    o_ref[...] = jnp.dot(_silu(c_ref[...]), w_ref[...], preferred_element_type=f32,
                         precision=lax.Precision.HIGHEST) + b_ref[...]


def _modulation(c_all, ada_w, ada_b):
    depth, d, n3 = ada_w.shape
    rows = c_all.shape[0]
    tn = d
    return pl.pallas_call(
        _mod_body,
        grid=(depth, n3 // tn),
        in_specs=[pl.BlockSpec((rows, d), lambda l, j: (0, 0)),
                  pl.BlockSpec((None, d, tn), lambda l, j: (l, 0, j)),
                  pl.BlockSpec((None, 1, tn), lambda l, j: (l, 0, j))],
        out_specs=pl.BlockSpec((None, rows, tn), lambda l, j: (l, 0, j)),
        out_shape=jax.ShapeDtypeStruct((depth, rows, n3), f32),
        compiler_params=_params("arbitrary", "arbitrary"),
        name="modulation",
    )(c_all, ada_w, ada_b.reshape(depth, 1, n3))


def _a_proj_body(n_skip, x_ref, mod_ref, g_ref, wq_ref, wk_ref, wv_ref, wz_ref, gq_ref, gk_ref,
                 qT_ref, k_ref, vT_ref, zs_ref, k32_ref, v32_ref, kf_ref):
    hb = _ada_hidden(x_ref[...], mod_ref, g_ref)
    z = _dot(hb, wz_ref[...])
    kT = _nt(wk_ref[...], hb)
    _store_lane_tiles(zs_ref, _silu(z).astype(bf16))
    qT = _nt(wq_ref[...], hb)
    gq = gq_ref[...]
    gk = gk_ref[...]
    hd = A_HEAD_DIM
    pairs = [slice(2 * hd * p, 2 * hd * (p + 1)) for p in range(N_HEADS // 2)]
    for p, rows in enumerate(pairs):
        k_pair = [_rms_rows(kT[hd * h:hd * (h + 1)], gk, hd) for h in (2 * p, 2 * p + 1)]
        k_nat = jnp.concatenate(k_pair, axis=0).T
        k_ref[p] = k_nat.astype(bf16)
        kf_ref[:, rows] = k_nat
    vT = _nt(wv_ref[...], hb)
    for p, rows in enumerate(pairs):
        q_pair = [_rms_rows(qT[hd * h:hd * (h + 1)], gq, hd) for h in (2 * p, 2 * p + 1)]
        qT_ref[rows, :] = jnp.concatenate(q_pair, axis=0).astype(bf16)
    vT_ref[...] = vT.astype(bf16)

    @pl.when(pl.program_id(1) >= n_skip)
    def _():
        k32_ref[...] = kf_ref[...]
        v32_ref[...] = vT.T


def _a_project(x, mod, g, wq_t, wk_t, wv_t, wz, gq, gk, tm, cache_rows):
    b, l, d = x.shape
    n_skip = (l - cache_rows) // tm
    tile = pl.BlockSpec((None, tm, d), lambda bi, i: (bi, i, 0))
    tile_t = pl.BlockSpec((None, d, tm), lambda bi, i: (bi, 0, i))
    cache = pl.BlockSpec((None, tm, d), lambda bi, i: (bi, jnp.maximum(i - n_skip, 0), 0))
    return pl.pallas_call(
        functools.partial(_a_proj_body, n_skip),
        grid=(b, l // tm),
        in_specs=[tile, pl.BlockSpec((None, 3, d), lambda bi, i: (bi, 0, 0)), _const((1, d)),
                  _weight((d, d)), _weight((d, d)), _weight((d, d)), _weight((d, d)),
                  _const((A_HEAD_DIM, 1)), _const((A_HEAD_DIM, 1))],
        out_specs=[tile_t, _pair_tiles(tm), tile_t, _pair_tiles(tm), cache, cache],
        out_shape=[jax.ShapeDtypeStruct((b, d, l), bf16), jax.ShapeDtypeStruct((b, N_PAIRS, l, LANES), bf16),
                   jax.ShapeDtypeStruct((b, d, l), bf16), jax.ShapeDtypeStruct((b, N_PAIRS, l, LANES), bf16),
                   jax.ShapeDtypeStruct((b, cache_rows, d), f32),
                   jax.ShapeDtypeStruct((b, cache_rows, d), f32)],
        scratch_shapes=[pltpu.VMEM((tm, d), f32)],
        compiler_params=_params("arbitrary", "arbitrary"),
        name="a_project",
    )(x, mod, g, wq_t, wk_t, wv_t, wz, gq, gk)


def _a_attn_body(seq, f_ref, k_ref, qT_ref, vT_ref, zs_ref, o_ref, e_ref, s0, s1, m0, m1, o0, o1):
    hd = A_HEAD_DIM
    n_groups = seq // Q_GROUP
    s_buf, m_buf, o_buf = (s0, s1), (m0, m1), (o0, o1)

    @pl.when(pl.program_id(1) == 0)
    def _():
        for hh in range(2):
            e_ref[hh] = _bias_tile(f_ref[hh], A_BIAS_ROWS, Q_GROUP)

    zeros = jnp.zeros((hd, Q_GROUP), bf16)
    ones = jnp.ones((ONES_ROWS, A_KEYS), bf16)

    def offsets(g):
        q0 = pl.multiple_of(g * Q_GROUP, Q_GROUP)
        w0 = pl.multiple_of(jnp.maximum(g * Q_GROUP - WINDOW, 0), Q_GROUP)
        e0 = pl.multiple_of(WINDOW - (g * Q_GROUP - w0), Q_GROUP)
        return q0, w0, e0

    def scores(g, par):
        q0, w0, e0 = offsets(g)
        k_win = k_ref[pl.ds(w0, A_KEYS), :]
        for hh in range(2):
            qh = qT_ref[hd * hh:hd * (hh + 1), pl.ds(q0, Q_GROUP)]
            w = jnp.concatenate([qh, zeros] if hh == 0 else [zeros, qh], axis=0)
            s = _dot(k_win, w) + e_ref[hh, pl.ds(e0, A_KEYS), :]
            s_buf[par][hh] = s
            m_buf[par][hh] = jnp.max(s, axis=0, keepdims=True)

    def values(g, par):
        _, w0, _ = offsets(g)
        for hh in range(2):
            p = jnp.exp2(s_buf[par][hh] - m_buf[par][hh]).astype(bf16)
            v_ext = jnp.concatenate([vT_ref[hd * hh:hd * (hh + 1), pl.ds(w0, A_KEYS)], ones], axis=0)
            o_buf[par][hh] = _dot(v_ext, p)

    def finish(g, par):
        q0, _, _ = offsets(g)
        outs = [_normalise(o_buf[par][hh], hd) for hh in range(2)]
        rows = pl.ds(q0, Q_GROUP)
        o_ref[rows, :] = _gated(jnp.concatenate(outs, axis=0), zs_ref, rows)

    scores(0, 0)
    scores(1, 1)
    values(0, 0)

    def body(i, carry):
        g = 2 * i + 1
        scores(g + 1, 0)
        values(g, 1)
        finish(g - 1, 0)
        scores(g + 2, 1)
        values(g + 1, 0)
        finish(g, 1)
        return carry

    lax.fori_loop(0, (n_groups - 2) // 2, body, 0)
    values(n_groups - 1, 1)
    finish(n_groups - 2, 0)
    finish(n_groups - 1, 1)


def _a_attention(f_rows, k_nat, q_t, v_t, zs):
    b, n_pairs, l, pair = k_nat.shape
    slab = pl.BlockSpec((None, None, l, pair), lambda p, bi: (bi, p, 0, 0))
    scratch = ([pltpu.VMEM((2, A_BIAS_ROWS, Q_GROUP), f32)]
               + [pltpu.VMEM((2, A_KEYS, Q_GROUP), f32)] * 2
               + [pltpu.VMEM((2, 1, Q_GROUP), f32)] * 2
               + [pltpu.VMEM((2, A_HEAD_DIM + ONES_ROWS, Q_GROUP), f32)] * 2)
    return pl.pallas_call(
        functools.partial(_a_attn_body, l),
        grid=(n_pairs, b),
        in_specs=[pl.BlockSpec((2, 1, A_BIAS_SPAN), lambda p, bi: (p, 0, 0)),
                  slab,
                  pl.BlockSpec((None, pair, l), lambda p, bi: (bi, p, 0)),
                  pl.BlockSpec((None, pair, l), lambda p, bi: (bi, p, 0)),
                  slab],
        out_specs=slab,
        out_shape=jax.ShapeDtypeStruct((b, n_pairs, l, pair), bf16),
        scratch_shapes=scratch,
        compiler_params=_params("arbitrary", "arbitrary"),
        name="a_attention",
    )(f_rows, k_nat, q_t, v_t, zs)


def _a_attn_bounded_body(seq, shift_ref, f_ref, k_ref, qT_ref, vT_ref, zs_ref, o_ref, e_ref, p0, p1, o0, o1):
    hd = A_HEAD_DIM
    n_groups = seq // Q_GROUP
    n_pairs = k_ref.shape[0]
    n_items = n_pairs * n_groups
    p_buf, o_buf = (p0, p1), (o0, o1)

    @pl.when(pl.program_id(1) == 0)
    def _():
        for h in range(2 * n_pairs):
            e_ref[h] = _bias_tile(f_ref[h], A_BIAS_ROWS, Q_GROUP) - shift_ref[0]

    zeros = jnp.zeros((hd, Q_GROUP), bf16)
    ones = jnp.ones((ONES_ROWS, A_KEYS), bf16)

    def offsets(n):
        pp = n // n_groups
        g = n - pp * n_groups
        q0 = pl.multiple_of(g * Q_GROUP, Q_GROUP)
        w0 = pl.multiple_of(jnp.maximum(g * Q_GROUP - WINDOW, 0), Q_GROUP)
        e0 = pl.multiple_of(WINDOW - (g * Q_GROUP - w0), Q_GROUP)
        return pp, q0, w0, e0

    def probs(n, par):
        pp, q0, w0, e0 = offsets(n)
        k_win = k_ref[pp, pl.ds(w0, A_KEYS), :]
        for hh in range(2):
            r0 = pl.multiple_of((2 * pp + hh) * hd, hd)
            qh = qT_ref[pl.ds(r0, hd), pl.ds(q0, Q_GROUP)]
            w = jnp.concatenate([qh, zeros] if hh == 0 else [zeros, qh], axis=0)
            bias = e_ref[2 * pp + hh, pl.ds(e0, A_KEYS), :]
            p_buf[par][hh] = jnp.exp2(_dot(k_win, w) + bias).astype(bf16)

    def values(n, par):
        pp, _, w0, _ = offsets(n)
        for hh in range(2):
            r0 = pl.multiple_of((2 * pp + hh) * hd, hd)
            v_ext = jnp.concatenate([vT_ref[pl.ds(r0, hd), pl.ds(w0, A_KEYS)], ones], axis=0)
            o_buf[par][hh] = _dot(v_ext, p_buf[par][hh])

    def finish(n, par):
        pp, q0, _, _ = offsets(n)
        outs = [_normalise(o_buf[par][hh], hd) for hh in range(2)]
        rows = pl.ds(q0, Q_GROUP)
        o_ref[pp, rows, :] = _gated(jnp.concatenate(outs, axis=0), zs_ref.at[pp], rows)

    probs(0, 0)
    probs(1, 1)
    values(0, 0)

    def body(i, carry):
        n = 2 * i + 1
        probs(n + 1, 0)
        values(n, 1)
        finish(n - 1, 0)
        probs(n + 2, 1)
        values(n + 1, 0)
        finish(n, 1)
        return carry

    lax.fori_loop(0, (n_items - 2) // 2, body, 0)
    values(n_items - 1, 1)
    finish(n_items - 2, 0)
    finish(n_items - 1, 1)


def _a_attention_bounded(shift, f_rows, k_nat, q_t, v_t, zs):
    b, n_pairs, l, pair = k_nat.shape
    pps = PAIRS_PER_STEP
    slab = pl.BlockSpec((None, pps, l, pair), lambda p, bi, sh: (bi, p, 0, 0))
    scratch = ([pltpu.VMEM((2 * pps, A_BIAS_ROWS, Q_GROUP), f32)]
               + [pltpu.VMEM((2, A_KEYS, Q_GROUP), bf16)] * 2
               + [pltpu.VMEM((2, A_HEAD_DIM + ONES_ROWS, Q_GROUP), f32)] * 2)
    grid_spec = pltpu.PrefetchScalarGridSpec(
        num_scalar_prefetch=1,
        grid=(n_pairs // pps, b),
        in_specs=[pl.BlockSpec((2 * pps, 1, A_BIAS_SPAN), lambda p, bi, sh: (p, 0, 0)),
                  slab,
                  pl.BlockSpec((None, pps * pair, l), lambda p, bi, sh: (bi, p, 0)),
                  pl.BlockSpec((None, pps * pair, l), lambda p, bi, sh: (bi, p, 0)),
                  slab],
        out_specs=slab,
        scratch_shapes=scratch)
    return pl.pallas_call(
        functools.partial(_a_attn_bounded_body, l),
        grid_spec=grid_spec,
        out_shape=jax.ShapeDtypeStruct((b, n_pairs, l, pair), bf16),
        compiler_params=_params("arbitrary", "arbitrary"),
        name="a_attention_bounded",
    )(shift, f_rows, k_nat, q_t, v_t, zs)


def _a_sattn_body(n_valid, f_ref, kc_ref, kn_ref, vc_ref, vn_ref, qT_ref, zs_ref, o_ref, e_ref):
    hd = A_HEAD_DIM
    n_keys = e_ref.shape[1]

    @pl.when(pl.program_id(1) == 0)
    def _():
        live = lax.broadcasted_iota(jnp.int32, (n_keys, SAMPLE_PAD), 0) < n_valid
        for hh in range(2):
            e_ref[hh] = jnp.where(live, _bias_tile(f_ref[hh], n_keys, SAMPLE_PAD), NEG_INF)

    zeros = jnp.zeros((hd, SAMPLE_PAD), bf16)
    k_all = jnp.concatenate([kc_ref[...].astype(bf16), kn_ref[...]], axis=0)
    v_t = jnp.concatenate([vc_ref[...], vn_ref[...]], axis=0).T.astype(bf16)
    outs = []
    for hh in range(2):
        qh = qT_ref[hd * hh:hd * (hh + 1), :]
        w = jnp.concatenate([qh, zeros] if hh == 0 else [zeros, qh], axis=0)
        p, l = _softmax_cols(_dot(k_all, w) + e_ref[hh])
        outs.append(_dot(v_t[hd * hh:hd * (hh + 1)], p.astype(bf16)) * (1.0 / l))
    o_ref[...] = _gated(jnp.concatenate(outs, axis=0), zs_ref, slice(None))


def _a_sample_attention(f_rows, k_cache, k_new, v_cache, v_new, q_t, zs, n_new):
    b, n_cache, d = k_cache.shape
    pair = 2 * A_HEAD_DIM
    n_keys = n_cache + SAMPLE_PAD
    new = pl.BlockSpec((None, SAMPLE_PAD, pair), lambda p, bi: (bi, 0, p))
    old = pl.BlockSpec((None, n_cache, pair), lambda p, bi: (bi, 0, p))
    slab = pl.BlockSpec((None, None, SAMPLE_PAD, pair), lambda p, bi: (bi, p, 0, 0))
    return pl.pallas_call(
        functools.partial(_a_sattn_body, n_cache + n_new),
        grid=(d // pair, b),
        in_specs=[pl.BlockSpec((2, 1, A_BIAS_SPAN), lambda p, bi: (p, 0, 0)),
                  old, slab, old, new,
                  pl.BlockSpec((None, pair, SAMPLE_PAD), lambda p, bi: (bi, p, 0)),
                  slab],
        out_specs=slab,
        out_shape=jax.ShapeDtypeStruct((b, d // pair, SAMPLE_PAD, pair), bf16),
        scratch_shapes=[pltpu.VMEM((2, n_keys, SAMPLE_PAD), f32)],
        compiler_params=_params("arbitrary", "arbitrary"),
        name="a_sample_attention",
    )(f_rows, k_cache, k_new, v_cache, v_new, q_t, zs)


def _out_body(oz_ref, w_ref, x_ref, mod_ref, y_ref):
    y = _dot(_load_lane_tiles(oz_ref), w_ref[...])
    y_ref[...] = x_ref[...] + mod_ref[2:3, :] * y


def _out_project(oz, w_out, x, mod, tm):
    b, l, d = x.shape
    tile = pl.BlockSpec((None, tm, d), lambda bi, i: (bi, i, 0))
    return pl.pallas_call(
        _out_body,
        grid=(b, l // tm),
        in_specs=[_pair_tiles(tm), _weight(w_out.shape), tile,
                  pl.BlockSpec((None, 3, d), lambda bi, i: (bi, 0, 0))],
        out_specs=tile,
        out_shape=jax.ShapeDtypeStruct((b, l, d), f32),
        compiler_params=_params("arbitrary", "arbitrary"),
        name="out_project",
    )(oz, w_out, x, mod)


def _rotate(x, cos, sin):
    half = B_ROPE // 2
    x1, x2 = x[:half], x[half:]
    return jnp.concatenate([x1 * cos - x2 * sin, x2 * cos + x1 * sin], axis=0)


def _b_proj_body(oz_ref, wout_ref, x_ref, mod_prev_ref, mod_ref, g_ref,
                 wcq_ref, wckv_ref, wkr_ref, wz_ref, wuq_ref, wukv_ref,
                 gcq_ref, gckv_ref, gkr_ref, gqn_ref, gqr_ref, gkn_ref, cos_ref, sin_ref,
                 y_ref, qT_ref, ckv_ref, kr_ref, zs_ref, k_ref, vT_ref):
    y = x_ref[...] + mod_prev_ref[2:3, :] * _dot(_load_lane_tiles(oz_ref), wout_ref[...])
    y_ref[...] = y
    hb = _ada_hidden(y, mod_ref, g_ref)
    tm = hb.shape[0]
    cos = cos_ref[...]
    sin = sin_ref[...]
    cq_raw = _nt(wcq_ref[...], hb)
    z = _dot(hb, wz_ref[...])
    ckv_raw = _nt(wckv_ref[...], hb)
    kr_raw = _nt(wkr_ref[...], hb)
    cq = _rms_rows(cq_raw, gcq_ref[...], Q_LORA).astype(bf16)
    qT = _dot(wuq_ref[...], cq)
    ckv = _rms_rows(ckv_raw, gckv_ref[...], KV_LORA)
    kvT = _dot(wukv_ref[...], ckv.astype(bf16))
    _store_lane_tiles(zs_ref, _silu(z).astype(bf16))
    ckv_ref[...] = ckv.T
    kr = _rotate(_rms_rows(kr_raw, gkr_ref[...], B_ROPE), cos, sin)
    kr_pad = jnp.concatenate([kr, jnp.zeros((128 - B_ROPE, tm), f32)], axis=0)
    kr_ref[...] = kr_pad.T[:, :B_ROPE]
    _emit_keys_values(kvT, kr, gkn_ref[...], k_ref, vT_ref)
    gqn = gqn_ref[...]
    gqr = gqr_ref[...]
    for h in range(N_HEADS):
        base = B_QK * h
        nope = _rms_rows(qT[base:base + B_NOPE], gqn, B_NOPE)
        rope = _rotate(_rms_rows(qT[base + B_NOPE:base + B_QK], gqr, B_ROPE), cos, sin)
        qT_ref[base:base + B_QK, :] = jnp.concatenate([nope, rope], axis=0).astype(bf16)


def _b_project(oz_prev, w_out_prev, x, mod_prev, mod, g, w, cos_t, sin_t, tm):
    b, l, d = x.shape
    tile = pl.BlockSpec((None, tm, d), lambda bi, i: (bi, i, 0))
    mod_spec = pl.BlockSpec((None, 3, d), lambda bi, i: (bi, 0, 0))
    rot = pl.BlockSpec((B_ROPE // 2, tm), lambda bi, i: (0, i))
    nq = N_HEADS * B_QK
    return pl.pallas_call(
        _b_proj_body,
        grid=(b, l // tm),
        in_specs=[_pair_tiles(tm), _weight(w_out_prev.shape), tile, mod_spec, mod_spec, _const((1, d)),
                  _weight((Q_LORA, d)), _weight((KV_LORA, d)), _weight((B_ROPE, d)), _weight((d, d)),
                  _weight((nq, Q_LORA)), _weight(w["ukv_t"].shape),
                  _const((Q_LORA, 1)), _const((KV_LORA, 1)), _const((B_ROPE, 1)),
                  _const((B_NOPE, 1)), _const((B_ROPE, 1)), _const((B_NOPE, 1)), rot, rot],
        out_specs=[tile,
                   pl.BlockSpec((None, nq, tm), lambda bi, i: (bi, 0, i)),
                   pl.BlockSpec((None, tm, KV_LORA), lambda bi, i: (bi, i, 0)),
                   pl.BlockSpec((None, tm, B_ROPE), lambda bi, i: (bi, i, 0)),
                   _pair_tiles(tm),
                   pl.BlockSpec((None, N_HEADS, tm, LANES), lambda bi, i: (bi, 0, i, 0)),
                   pl.BlockSpec((None, N_HEADS * B_VDIM, tm), lambda bi, i: (bi, 0, i))],
        out_shape=[jax.ShapeDtypeStruct((b, l, d), f32),
                   jax.ShapeDtypeStruct((b, nq, l), bf16),
                   jax.ShapeDtypeStruct((b, l, KV_LORA), f32),
                   jax.ShapeDtypeStruct((b, l, B_ROPE), f32),
                   jax.ShapeDtypeStruct((b, N_PAIRS, l, LANES), bf16),
                   jax.ShapeDtypeStruct((b, N_HEADS, l, LANES), bf16),
                   jax.ShapeDtypeStruct((b, N_HEADS * B_VDIM, l), bf16)],
        compiler_params=_params("arbitrary", "arbitrary"),
        name="b_project",
    )(oz_prev, w_out_prev, x, mod_prev, mod, g, w["cq_t"], w["ckv_t"], w["kr_t"], w["z"], w["uq_t"], w["ukv_t"],
      w["g_cq"], w["g_ckv"], w["g_kr"], w["g_qn"], w["g_qr"], w["g_kn"], cos_t, sin_t)


def _b_kvup_body(ckv_ref, kr_ref, wukv_ref, gkn_ref, k_ref, vT_ref):
    kvT = _nt(wukv_ref[...], ckv_ref[...].astype(bf16))
    eye = (lax.broadcasted_iota(jnp.int32, (B_ROPE, B_ROPE), 0)
           == lax.broadcasted_iota(jnp.int32, (B_ROPE, B_ROPE), 1)).astype(bf16)
    krT = _nt(eye, kr_ref[...].astype(bf16))
    _emit_keys_values(kvT, krT, gkn_ref[...], k_ref, vT_ref)


def _emit_keys_values(kvT, krT, gkn, k_ref, vT_ref):
    tm = kvT.shape[1]
    key_chunk = (lax.broadcasted_iota(jnp.int32, (128 - B_QK, tm), 1) // CHUNK) % TILE_CHUNKS
    spare_row = lax.broadcasted_iota(jnp.int32, (128 - B_QK, tm), 0)
    chunk_rows = ((spare_row == key_chunk) | (spare_row == TILE_CHUNKS)).astype(f32)
    tail = jnp.concatenate([krT, chunk_rows], axis=0)
    per_head = B_NOPE + B_VDIM
    for h in range(N_HEADS):
        base = per_head * h
        kn = _rms_rows(kvT[base:base + B_NOPE], gkn, B_NOPE)
        k_ref[h] = jnp.concatenate([kn, tail], axis=0).T.astype(bf16)
        vT_ref[B_VDIM * h:B_VDIM * (h + 1), :] = kvT[base + B_NOPE:base + per_head].astype(bf16)


def _b_kv_up(ckv, kr, wukv_t, g_kn, tm):
    b, l, _ = ckv.shape
    return pl.pallas_call(
        _b_kvup_body,
        grid=(b, l // tm),
        in_specs=[pl.BlockSpec((None, tm, KV_LORA), lambda bi, i: (bi, i, 0)),
                  pl.BlockSpec((None, tm, B_ROPE), lambda bi, i: (bi, i, 0)),
                  _const(wukv_t.shape), _const((B_NOPE, 1))],
        out_specs=[pl.BlockSpec((None, N_HEADS, tm, 128), lambda bi, i: (bi, 0, i, 0)),
                   pl.BlockSpec((None, N_HEADS * B_VDIM, tm), lambda bi, i: (bi, 0, i))],
        out_shape=[jax.ShapeDtypeStruct((b, N_HEADS, l, 128), bf16),
                   jax.ShapeDtypeStruct((b, N_HEADS * B_VDIM, l), bf16)],
        compiler_params=_params("arbitrary", "arbitrary"),
        name="b_kv_up",
    )(ckv, kr, wukv_t, g_kn)


def _b_item_table(n_tiles, n_pairs=1):
    rows = [(pp, t, j, int(j == t)) for pp in range(n_pairs) for t in range(n_tiles) for j in range(t, -1, -1)]
    return np.asarray(rows, np.int32).T.copy()


def _b_items_per_step(n_items, odd_steps):
    return max(u for u in range(1, B_ITEMS_PER_STEP + 1)
               if n_items % u == 0 and n_items // u >= 3 and (not odd_steps or (n_items // u) % 2 == 1))


def _b_attn_body(seq, per, tab_ref, k_ref, qT_ref, vT_ref, zs_ref, o_ref,
                 s0, s1, c0, c1, pv0, pv1, a0, a1, m_st, acc_st):
    t_sz = Q_GROUP
    n_tiles = seq // t_sz
    n_steps = n_tiles * (n_tiles + 1) // 2 // per
    s_buf, c_buf, pv_buf, a_buf = (s0, s1), (c0, c1), (pv0, pv1), (a0, a1)
    ones = jnp.ones((ONES_ROWS, t_sz), bf16)
    key_chunk = lax.broadcasted_iota(jnp.int32, (128 - B_QK, t_sz), 0)
    qry_chunk = lax.broadcasted_iota(jnp.int32, (128 - B_QK, t_sz), 1) // CHUNK
    mask_rows = jnp.where((key_chunk < t_sz // CHUNK) & (key_chunk > qry_chunk), NEG_INF, 0.0).astype(bf16)
    zero_rows = jnp.zeros((128 - B_QK, t_sz), bf16)
    m_st[...] = jnp.full(m_st.shape, NEG_INF, f32)
    acc_st[...] = jnp.zeros(acc_st.shape, f32)

    def item(step, u):
        n = step * per + u
        return tab_ref[1, n], tab_ref[2, n], tab_ref[3, n] == 1

    def scores(step, par, u):
        t, j, diag = item(step, u)
        q0 = pl.multiple_of(t * t_sz, t_sz)
        k0 = pl.multiple_of(j * t_sz, t_sz)
        pad = jnp.where(diag, mask_rows, zero_rows)
        for hh in range(2):
            w = jnp.concatenate([qT_ref[B_QK * hh:B_QK * (hh + 1), pl.ds(q0, t_sz)], pad], axis=0)
            s = _dot(k_ref[hh, pl.ds(k0, t_sz), :], w)
            s_buf[par][u, hh] = s
            c_buf[par][u, hh] = jnp.max(s, axis=0, keepdims=True)

    def values(step, par, u):
        t, j, _ = item(step, u)
        k0 = pl.multiple_of(j * t_sz, t_sz)
        for hh in range(2):
            m_old = m_st[t, hh]
            m_new = jnp.maximum(m_old, c_buf[par][u, hh])
            m_st[t, hh] = m_new
            a_buf[par][u, hh] = jnp.exp2(m_old - m_new)
            p = jnp.exp2(s_buf[par][u, hh] - m_new).astype(bf16)
            v_ext = jnp.concatenate([vT_ref[B_VDIM * hh:B_VDIM * (hh + 1), pl.ds(k0, t_sz)], ones], axis=0)
            pv_buf[par][u, hh] = _dot(v_ext, p)

    def accumulate(step, par, u):
        t, _, _ = item(step, u)
        for hh in range(2):
            acc_st[t, hh] = a_buf[par][u, hh] * acc_st[t, hh] + pv_buf[par][u, hh]

    def time_step(tau, par, first=False, drain=0):
        for u in range(per):
            if drain == 0:
                scores(tau + 1, 1 - par, u)
            if drain <= 1:
                values(tau, par, u)
            if not first:
                accumulate(tau - 1, 1 - par, u)

    for u in range(per):
        scores(0, 0, u)
    time_step(0, 0, first=True)

    def body(i, carry):
        tau = 2 * i + 1
        time_step(tau, 1)
        time_step(tau + 1, 0)
        return carry

    lax.fori_loop(0, (n_steps - 3) // 2, body, 0)
    time_step(n_steps - 2, 1)
    time_step(n_steps - 1, 0, drain=1)
    time_step(n_steps, 1, drain=2)
    for t in range(n_tiles):
        outs = [_normalise(acc_st[t, hh], B_VDIM) for hh in range(2)]
        rows = slice(t * t_sz, (t + 1) * t_sz)
        o_ref[rows, :] = _gated(jnp.concatenate(outs, axis=0), zs_ref, rows)


def _b_attention(k_nat, q_t, v_t, zs):
    b, _, l, _ = k_nat.shape
    t_sz = Q_GROUP
    n_tiles = l // t_sz
    table = _b_item_table(n_tiles)
    per = _b_items_per_step(table.shape[1], odd_steps=True)
    ext = B_VDIM + ONES_ROWS
    scratch = ([pltpu.VMEM((per, 2, t_sz, t_sz), f32)] * 2 + [pltpu.VMEM((per, 2, 1, t_sz), f32)] * 2
               + [pltpu.VMEM((per, 2, ext, t_sz), f32)] * 2 + [pltpu.VMEM((per, 2, 1, t_sz), f32)] * 2
               + [pltpu.VMEM((n_tiles, 2, 1, t_sz), f32), pltpu.VMEM((n_tiles, 2, ext, t_sz), f32)])
    grid_spec = pltpu.PrefetchScalarGridSpec(
        num_scalar_prefetch=1,
        grid=(b, N_HEADS // 2),
        in_specs=[pl.BlockSpec((None, 2, l, 128), lambda bi, p, tab: (bi, p, 0, 0)),
                  pl.BlockSpec((None, 2 * B_QK, l), lambda bi, p, tab: (bi, p, 0)),
                  pl.BlockSpec((None, 2 * B_VDIM, l), lambda bi, p, tab: (bi, p, 0)),
                  pl.BlockSpec((None, None, l, LANES), lambda bi, p, tab: (bi, p, 0, 0))],
        out_specs=pl.BlockSpec((None, None, l, LANES), lambda bi, p, tab: (bi, p, 0, 0)),
        scratch_shapes=scratch)
    return pl.pallas_call(
        functools.partial(_b_attn_body, l, per),
        grid_spec=grid_spec,
        out_shape=jax.ShapeDtypeStruct((b, N_PAIRS, l, LANES), bf16),
        compiler_params=_params("arbitrary", "arbitrary"),
        name="b_attention",
    )(jnp.asarray(table), k_nat, q_t, v_t, zs)


def _b_attn_bounded_body(seq, per, tab_ref, shift_ref, k_ref, qT_ref, vT_ref, zs_ref, o_ref, p0, p1, acc_st):
    t_sz = Q_GROUP
    n_tiles = seq // t_sz
    n_pairs = acc_st.shape[0]
    n_steps = n_pairs * n_tiles * (n_tiles + 1) // 2 // per
    p_buf = (p0, p1)
    ones = jnp.ones((ONES_ROWS, t_sz), bf16)
    row = lax.broadcasted_iota(jnp.int32, (128 - B_QK, t_sz), 0)
    qry_chunk = lax.broadcasted_iota(jnp.int32, (128 - B_QK, t_sz), 1) // CHUNK
    shift_rows = jnp.where(row == TILE_CHUNKS, -shift_ref[0], 0.0)
    mask_rows = (jnp.where((row < TILE_CHUNKS) & (row > qry_chunk), NEG_INF, 0.0) + shift_rows).astype(bf16)
    plain_rows = shift_rows.astype(bf16)
    acc_st[...] = jnp.zeros(acc_st.shape, f32)

    def item(step, u):
        n = step * per + u
        return tab_ref[0, n], tab_ref[1, n], tab_ref[2, n], tab_ref[3, n] == 1

    def probs(step, par, u):
        pp, t, j, diag = item(step, u)
        q0 = pl.multiple_of(t * t_sz, t_sz)
        k0 = pl.multiple_of(j * t_sz, t_sz)
        pad = jnp.where(diag, mask_rows, plain_rows)
        for hh in range(2):
            r0 = pl.multiple_of((2 * pp + hh) * B_QK, B_ROPE)
            w = jnp.concatenate([qT_ref[pl.ds(r0, B_QK), pl.ds(q0, t_sz)], pad], axis=0)
            p_buf[par][u, hh] = jnp.exp2(_dot(k_ref[2 * pp + hh, pl.ds(k0, t_sz), :], w)).astype(bf16)

    def values(step, par, u):
        pp, t, j, _ = item(step, u)
        k0 = pl.multiple_of(j * t_sz, t_sz)
        for hh in range(2):
            r0 = pl.multiple_of((2 * pp + hh) * B_VDIM, B_VDIM)
            v_ext = jnp.concatenate([vT_ref[pl.ds(r0, B_VDIM), pl.ds(k0, t_sz)], ones], axis=0)
            acc_st[pp, t, hh] += _dot(v_ext, p_buf[par][u, hh])

    def time_step(tau, last):
        par = tau % 2
        for u in range(per):
            if not last:
                probs(tau + 1, 1 - par, u)
            values(tau, par, u)

    def step_pair(i, carry):
        for par in range(2):
            for u in range(per):
                probs(2 * i + par + 1, 1 - par, u)
                values(2 * i + par, par, u)
        return carry

    for u in range(per):
        probs(0, 0, u)
    n_loop = (n_steps - 1) // 2
    lax.fori_loop(0, n_loop, step_pair, 0)
    for tau in range(2 * n_loop, n_steps):
        time_step(tau, last=tau == n_steps - 1)
    for pp in range(n_pairs):
        for t in range(n_tiles):
            outs = [_normalise(acc_st[pp, t, hh], B_VDIM) for hh in range(2)]
            rows = slice(t * t_sz, (t + 1) * t_sz)
            o_ref[pp, rows, :] = _gated(jnp.concatenate(outs, axis=0), zs_ref.at[pp], rows)


def _b_attention_bounded(shift, k_nat, q_t, v_t, zs):
    b, _, l, _ = k_nat.shape
    t_sz = Q_GROUP
    n_tiles = l // t_sz
    pps = PAIRS_PER_STEP
    table = _b_item_table(n_tiles, pps)
    per = _b_items_per_step(table.shape[1], odd_steps=False)
    scratch = ([pltpu.VMEM((per, 2, t_sz, t_sz), bf16)] * 2
               + [pltpu.VMEM((pps, n_tiles, 2, B_VDIM + ONES_ROWS, t_sz), f32)])
    slab = pl.BlockSpec((None, pps, l, LANES), lambda bi, p, tab, sh: (bi, p, 0, 0))
    grid_spec = pltpu.PrefetchScalarGridSpec(
        num_scalar_prefetch=2,
        grid=(b, N_PAIRS // pps),
        in_specs=[pl.BlockSpec((None, 2 * pps, l, 128), lambda bi, p, tab, sh: (bi, p, 0, 0)),
                  pl.BlockSpec((None, 2 * pps * B_QK, l), lambda bi, p, tab, sh: (bi, p, 0)),
                  pl.BlockSpec((None, 2 * pps * B_VDIM, l), lambda bi, p, tab, sh: (bi, p, 0)),
                  slab],
        out_specs=slab,
        scratch_shapes=scratch)
    return pl.pallas_call(
        functools.partial(_b_attn_bounded_body, l, per),
        grid_spec=grid_spec,
        out_shape=jax.ShapeDtypeStruct((b, N_PAIRS, l, LANES), bf16),
        compiler_params=_params("arbitrary", "arbitrary"),
        name="b_attention_bounded",
    )(jnp.asarray(table), shift, k_nat, q_t, v_t, zs)


def _b_sattn_body(n_new, kc_ref, kn_ref, vc_ref, vn_ref, qT_ref, zs_ref, o_ref):
    t = SAMPLE_PAD
    zpad = jnp.zeros((128 - B_QK, t), bf16)
    new_mask = jnp.where(lax.broadcasted_iota(jnp.int32, (t, t), 0) < n_new, 0.0, NEG_INF).astype(f32)
    outs = []
    for hh in range(2):
        w = jnp.concatenate([qT_ref[B_QK * hh:B_QK * (hh + 1), :], zpad], axis=0)
        s_c = _dot(kc_ref[hh], w)
        s_n = _dot(kn_ref[hh], w) + new_mask
        m = jnp.maximum(jnp.max(s_c, axis=0, keepdims=True), jnp.max(s_n, axis=0, keepdims=True))
        p_c = jnp.exp2(s_c - m)
        p_n = jnp.exp2(s_n - m)
        l = jnp.sum(p_c, axis=0, keepdims=True) + jnp.sum(p_n, axis=0, keepdims=True)
        rows = slice(B_VDIM * hh, B_VDIM * (hh + 1))
        o_t = _dot(vc_ref[rows, :], p_c.astype(bf16)) + _dot(vn_ref[rows, :], p_n.astype(bf16))
        outs.append(o_t * (1.0 / l))
    o_ref[...] = _gated(jnp.concatenate(outs, axis=0), zs_ref, slice(None))


def _b_sample_attention(k_cache, k_new, v_cache, v_new, q_t, zs, n_new):
    b, _, past, _ = k_cache.shape
    t = SAMPLE_PAD
    slab = pl.BlockSpec((None, None, t, LANES), lambda bi, p: (bi, p, 0, 0))
    return pl.pallas_call(
        functools.partial(_b_sattn_body, n_new),
        grid=(b, N_HEADS // 2),
        in_specs=[pl.BlockSpec((None, 2, past, 128), lambda bi, p: (bi, p, 0, 0)),
                  pl.BlockSpec((None, 2, t, 128), lambda bi, p: (bi, p, 0, 0)),
                  pl.BlockSpec((None, 2 * B_VDIM, past), lambda bi, p: (bi, p, 0)),
                  pl.BlockSpec((None, 2 * B_VDIM, t), lambda bi, p: (bi, p, 0)),
                  pl.BlockSpec((None, 2 * B_QK, t), lambda bi, p: (bi, p, 0)),
                  slab],
        out_specs=slab,
        out_shape=jax.ShapeDtypeStruct((b, N_PAIRS, t, LANES), bf16),
        compiler_params=_params("arbitrary", "arbitrary"),
        name="b_sample_attention",
    )(k_cache, k_new, v_cache, v_new, q_t, zs)


def _bias_rows(table):
    left = A_BIAS_ROWS - 1 - WINDOW - REL_CLIP
    right = A_BIAS_SPAN - left - table.shape[-1]
    return (jnp.pad(table.astype(f32), ((0, 0), (left, right)), mode="edge") * LOG2E)[:, None, :]


def _rope_tables(pos):
    half = B_ROPE // 2
    inv = ROPE_THETA ** (-jnp.arange(half, dtype=f32) / half)
    ang = pos.astype(f32)[:, None] * inv[None, :]
    return jnp.cos(ang).T, jnp.sin(ang).T


def _col(g, scale=1.0):
    return (g.astype(f32) * scale)[:, None]


def kernel(x_prompt, x_sample, cache_a_k, cache_a_v, cache_mla_ckv, cache_mla_krope, c_prompt, c_sample, norm_g, ada_w, ada_b, a_w_in, a_g_q, a_g_k, a_rel_bias, a_w_out, b_w_in, b_g_cq, b_w_uq, b_g_ckv, b_w_ukv, b_g_qn, b_g_qr, b_g_kn, b_g_kr, b_w_out):
    bp, seq, d = x_prompt.shape
    bs, dec, _ = x_sample.shape
    past = cache_mla_ckv.shape[2]
    n_cache_a = cache_a_k.shape[2]
    assert d == N_HEADS * A_HEAD_DIM and seq % (2 * Q_GROUP) == 0 and seq >= A_KEYS
    assert past % CHUNK == 0 and dec <= CHUNK and dec <= SAMPLE_PAD and n_cache_a == WINDOW
    cache_rows = min(WINDOW, seq)
    tm = min(ROW_TILE, seq)

    mod = _modulation(jnp.concatenate([c_prompt, c_sample], axis=0), ada_w, ada_b)
    mod = mod.reshape(mod.shape[0], bp + bs, 3, d)
    xs_pad = jnp.pad(x_sample, ((0, 0), (0, SAMPLE_PAD - dec), (0, 0)))

    w_in = a_w_in[0]
    wq_t, wk_t, wv_t = (w_in[:, d * n:d * (n + 1)].T.astype(bf16) for n in range(3))
    wz = w_in[:, 3 * d:].astype(bf16)
    w_out_a = a_w_out[0].astype(bf16)
    gq = _col(a_g_q[0], A_HEAD_DIM ** -0.5 * LOG2E)
    gk = _col(a_g_k[0])
    g0 = norm_g[0][None, :]
    f_rows = _bias_rows(a_rel_bias[0])
    mod_p0, mod_s0 = mod[0, :bp], mod[0, bp:]

    q_t, k_nat, v_t, zs, k32, v32 = _a_project(x_prompt, mod_p0, g0, wq_t, wk_t, wv_t, wz, gq, gk, tm, cache_rows)
    qk_max = A_HEAD_DIM * jnp.max(jnp.abs(gq)) * jnp.max(jnp.abs(gk)) * BOUND_MARGIN
    upper = qk_max + jnp.max(f_rows)
    oz_p0 = lax.cond(upper - (jnp.min(f_rows) - qk_max) <= MAX_LOGIT_RANGE,
                     lambda: _a_attention_bounded(upper[None], f_rows, k_nat, q_t, v_t, zs),
                     lambda: _a_attention(f_rows, k_nat, q_t, v_t, zs))
    new_a_k_p = k32.reshape(1, bp, cache_rows, N_HEADS, A_HEAD_DIM)
    new_a_v_p = v32.reshape(1, bp, cache_rows, N_HEADS, A_HEAD_DIM)

    q_t, k_nat, v_t, zs, k32, v32 = _a_project(xs_pad, mod_s0, g0, wq_t, wk_t, wv_t, wz, gq, gk,
                                               SAMPLE_PAD, SAMPLE_PAD)
    oz_s0 = _a_sample_attention(f_rows, cache_a_k[0].reshape(bs, n_cache_a, d), k_nat,
                                cache_a_v[0].reshape(bs, n_cache_a, d), v32, q_t, zs, dec)
    new_a_k_s = k32[:, :dec].reshape(1, bs, dec, N_HEADS, A_HEAD_DIM)
    new_a_v_s = v32[:, :dec].reshape(1, bs, dec, N_HEADS, A_HEAD_DIM)

    w_in = b_w_in[0]
    scale = B_QK ** -0.5 * LOG2E
    wb = {
        "cq_t": w_in[:, :Q_LORA].T.astype(bf16),
        "ckv_t": w_in[:, Q_LORA:Q_LORA + KV_LORA].T.astype(bf16),
        "kr_t": w_in[:, Q_LORA + KV_LORA:Q_LORA + KV_LORA + B_ROPE].T.astype(bf16),
        "z": w_in[:, Q_LORA + KV_LORA + B_ROPE:].astype(bf16),
        "uq_t": b_w_uq[0].T.astype(bf16),
        "g_cq": _col(b_g_cq[0]), "g_ckv": _col(b_g_ckv[0]), "g_kr": _col(b_g_kr[0]),
        "g_qn": _col(b_g_qn[0], scale), "g_qr": _col(b_g_qr[0], scale),
        "ukv_t": b_w_ukv[0].T.astype(bf16), "g_kn": _col(b_g_kn[0]),
    }
    wukv_t, g_kn = wb["ukv_t"], wb["g_kn"]
    w_out = b_w_out[0].astype(bf16)
    g1 = norm_g[1][None, :]
    mod_p, mod_s = mod[1, :bp], mod[1, bp:]

    cos_t, sin_t = _rope_tables(jnp.arange(seq, dtype=jnp.int32))
    y_p, q_t, ckv_p, kr_p, zs, k_nat, v_t = _b_project(oz_p0, w_out_a, x_prompt, mod_p0, mod_p, g1, wb,
                                                       cos_t, sin_t, min(B_PROJ_TILE, seq))
    q_sq = B_NOPE * jnp.max(jnp.abs(wb["g_qn"])) ** 2 + B_ROPE * jnp.max(jnp.abs(wb["g_qr"])) ** 2
    k_sq = B_NOPE * jnp.max(jnp.abs(g_kn)) ** 2 + B_ROPE * jnp.max(jnp.abs(wb["g_kr"])) ** 2
    qk_max = jnp.sqrt(q_sq * k_sq) * BOUND_MARGIN
    oz = lax.cond(2.0 * qk_max <= MAX_LOGIT_RANGE,
                  lambda: _b_attention_bounded(qk_max[None], k_nat, q_t, v_t, zs),
                  lambda: _b_attention(k_nat, q_t, v_t, zs))
    y_p = _out_project(oz, w_out, y_p, mod_p, tm)

    cos_t, sin_t = _rope_tables(past + jnp.arange(SAMPLE_PAD, dtype=jnp.int32))
    ys_pad, q_t, ckv_s, kr_s, zs, k_new, v_new = _b_project(oz_s0, w_out_a, xs_pad, mod_s0, mod_s, g1, wb,
                                                            cos_t, sin_t, SAMPLE_PAD)
    k_old, v_old = _b_kv_up(cache_mla_ckv[0], cache_mla_krope[0], wukv_t, g_kn, tm)
    oz = _b_sample_attention(k_old, k_new, v_old, v_new, q_t, zs, dec)
    ys_pad = _out_project(oz, w_out, ys_pad, mod_s, SAMPLE_PAD)

    return (y_p, ys_pad[:, :dec], new_a_k_p, new_a_v_p, new_a_k_s, new_a_v_s,
            ckv_p[None], kr_p[None], ckv_s[None, :, :dec], kr_s[None, :, :dec])
```

```python
import functools

import numpy as np

import jax
import jax.numpy as jnp
from jax import lax
from jax.experimental import pallas as pl
from jax.experimental.pallas import tpu as pltpu

f32 = jnp.float32
bf16 = jnp.bfloat16

EPS = 1e-6
NEG_INF = -1e30
CHUNK = 64
BAND_CHUNKS = 8
WINDOW = BAND_CHUNKS * CHUNK
REL_CLIP = 128
ROPE_THETA = 10000.0
LOG2E = 1.4426950408889634

LANES = 128
N_HEADS = 16
N_PAIRS = N_HEADS // 2
A_HEAD_DIM = 64
B_NOPE = 64
B_ROPE = 32
B_VDIM = 64
B_QK = B_NOPE + B_ROPE
KV_LORA = 256
Q_LORA = 384

Q_GROUP = 256
TILE_CHUNKS = Q_GROUP // CHUNK
MAX_LOGIT_RANGE = 100.0
BOUND_MARGIN = 1.02
A_KEYS = WINDOW + Q_GROUP
A_BIAS_ROWS = A_KEYS + WINDOW
A_BIAS_SPAN = A_BIAS_ROWS + Q_GROUP
SAMPLE_PAD = 128
ROW_TILE = 512
OUT_TILE = 1024
B_PROJ_TILE = 256
B_ITEMS_PER_STEP = 4
PAIRS_PER_STEP = 4
ONES_ROWS = 8
VMEM_LIMIT = 56 * 1024 * 1024


def _silu(x):
    return x * (1.0 / (1.0 + jnp.exp(-x)))


def _nt(a, b):
    return lax.dot_general(a, b, (((1,), (1,)), ((), ())), preferred_element_type=f32)


def _dot(a, b):
    return jnp.dot(a, b, preferred_element_type=f32)


def _rms_rows(blk, gain_col, n):
    ms = jnp.sum(blk * blk, axis=0, keepdims=True) * (1.0 / n)
    return blk * lax.rsqrt(ms + EPS) * gain_col


def _ada_hidden(x, mod_ref, g_ref):
    ms = jnp.mean(x * x, axis=-1, keepdims=True)
    xn = x * lax.rsqrt(ms + EPS) * g_ref[...]
    return (xn * (1.0 + mod_ref[1:2, :]) + mod_ref[0:1, :]).astype(bf16)


def _softmax_cols(s):
    m = jnp.max(s, axis=0, keepdims=True)
    p = jnp.exp2(s - m)
    return p, jnp.sum(p, axis=0, keepdims=True)


def _store_lane_tiles(ref, val):
    for p in range(ref.shape[0]):
        ref[p] = val[:, LANES * p:LANES * (p + 1)]


def _load_lane_tiles(ref):
    return jnp.concatenate([ref[p] for p in range(ref.shape[0])], axis=1)


def _gated(o_t, zs_ref, rows):
    return (o_t.T * zs_ref[rows, :].astype(f32)).astype(bf16)


def _normalise(o_ext, rows):
    return o_ext[:rows] * (1.0 / o_ext[rows:rows + 1])


def _bias_tile(f_row, n_rows, n_cols):
    blocks = []
    for a in range(pl.cdiv(n_rows, Q_GROUP)):
        lo = A_BIAS_ROWS - Q_GROUP * (a + 1)
        x = jnp.broadcast_to(f_row[:, lo:lo + 2 * Q_GROUP], (Q_GROUP, 2 * Q_GROUP))
        y = pltpu.roll(x, Q_GROUP + 1, 1, stride=1, stride_axis=0)[:, :n_cols]
        r = lax.broadcasted_iota(jnp.int32, (Q_GROUP, n_cols), 0) + Q_GROUP * a
        i = lax.broadcasted_iota(jnp.int32, (Q_GROUP, n_cols), 1)
        back = BAND_CHUNKS + i // CHUNK - r // CHUNK
        blocks.append(jnp.where((back >= 0) & (back <= BAND_CHUNKS), y, NEG_INF))
    return jnp.concatenate(blocks, axis=0)[:n_rows]


def _params(*sem):
    return pltpu.CompilerParams(dimension_semantics=sem, vmem_limit_bytes=VMEM_LIMIT)


def _const(shape):
    nd = len(shape)
    return pl.BlockSpec(shape, lambda *_: (0,) * nd)


def _pair_tiles(tm):
    return pl.BlockSpec((None, N_PAIRS, tm, LANES), lambda bi, i: (bi, 0, i, 0))


def _weight(shape):
    nd = len(shape)
    return pl.BlockSpec(shape, lambda *_: (0,) * nd, pipeline_mode=pl.Buffered(1))


def _mod_body(c_ref, w_ref, b_ref, o_ref):
    o_ref[...] = jnp.dot(_silu(c_ref[...]), w_ref[...], preferred_element_type=f32,
                         precision=lax.Precision.HIGHEST) + b_ref[...]


def _modulation(c_all, ada_w, ada_b):
    depth, d, n3 = ada_w.shape
    rows = c_all.shape[0]
    tn = d
    return pl.pallas_call(
        _mod_body,
        grid=(depth, n3 // tn),
        in_specs=[pl.BlockSpec((rows, d), lambda l, j: (0, 0)),
                  pl.BlockSpec((None, d, tn), lambda l, j: (l, 0, j)),
                  pl.BlockSpec((None, 1, tn), lambda l, j: (l, 0, j))],
        out_specs=pl.BlockSpec((None, rows, tn), lambda l, j: (l, 0, j)),
        out_shape=jax.ShapeDtypeStruct((depth, rows, n3), f32),
        compiler_params=_params("arbitrary", "arbitrary"),
        name="modulation",
    )(c_all, ada_w, ada_b.reshape(depth, 1, n3))


def _a_proj_body(n_skip, x_ref, mod_ref, g_ref, wq_ref, wk_ref, wv_ref, wz_ref, gq_ref, gk_ref,
                 qT_ref, k_ref, vT_ref, zs_ref, k32_ref, v32_ref, kf_ref):
    hb = _ada_hidden(x_ref[...], mod_ref, g_ref)
    z = _dot(hb, wz_ref[...])
    kT = _nt(wk_ref[...], hb)
    _store_lane_tiles(zs_ref, _silu(z).astype(bf16))
    qT = _nt(wq_ref[...], hb)
    gq = gq_ref[...]
    gk = gk_ref[...]
    hd = A_HEAD_DIM
    pairs = [slice(2 * hd * p, 2 * hd * (p + 1)) for p in range(N_HEADS // 2)]
    for p, rows in enumerate(pairs):
        k_pair = [_rms_rows(kT[hd * h:hd * (h + 1)], gk, hd) for h in (2 * p, 2 * p + 1)]
        k_nat = jnp.concatenate(k_pair, axis=0).T
        k_ref[p] = k_nat.astype(bf16)
        kf_ref[:, rows] = k_nat
    vT = _nt(wv_ref[...], hb)
    for p, rows in enumerate(pairs):
        q_pair = [_rms_rows(qT[hd * h:hd * (h + 1)], gq, hd) for h in (2 * p, 2 * p + 1)]
        qT_ref[rows, :] = jnp.concatenate(q_pair, axis=0).astype(bf16)
    vT_ref[...] = vT.astype(bf16)

    @pl.when(pl.program_id(1) >= n_skip)
    def _():
        k32_ref[...] = kf_ref[...]
        v32_ref[...] = vT.T


def _a_project(x, mod, g, wq_t, wk_t, wv_t, wz, gq, gk, tm, cache_rows):
    b, l, d = x.shape
    n_skip = (l - cache_rows) // tm
    tile = pl.BlockSpec((None, tm, d), lambda bi, i: (bi, i, 0))
    tile_t = pl.BlockSpec((None, d, tm), lambda bi, i: (bi, 0, i))
    cache = pl.BlockSpec((None, tm, d), lambda bi, i: (bi, jnp.maximum(i - n_skip, 0), 0))
    return pl.pallas_call(
        functools.partial(_a_proj_body, n_skip),
        grid=(b, l // tm),
        in_specs=[tile, pl.BlockSpec((None, 3, d), lambda bi, i: (bi, 0, 0)), _const((1, d)),
                  _weight((d, d)), _weight((d, d)), _weight((d, d)), _weight((d, d)),
                  _const((A_HEAD_DIM, 1)), _const((A_HEAD_DIM, 1))],
        out_specs=[tile_t, _pair_tiles(tm), tile_t, _pair_tiles(tm), cache, cache],
        out_shape=[jax.ShapeDtypeStruct((b, d, l), bf16), jax.ShapeDtypeStruct((b, N_PAIRS, l, LANES), bf16),
                   jax.ShapeDtypeStruct((b, d, l), bf16), jax.ShapeDtypeStruct((b, N_PAIRS, l, LANES), bf16),
                   jax.ShapeDtypeStruct((b, cache_rows, d), f32),
                   jax.ShapeDtypeStruct((b, cache_rows, d), f32)],
        scratch_shapes=[pltpu.VMEM((tm, d), f32)],
        compiler_params=_params("arbitrary", "arbitrary"),
        name="a_project",
    )(x, mod, g, wq_t, wk_t, wv_t, wz, gq, gk)


def _a_attn_body(seq, f_ref, k_ref, qT_ref, vT_ref, zs_ref, o_ref, e_ref, s0, s1, m0, m1, o0, o1):
    hd = A_HEAD_DIM
    n_groups = seq // Q_GROUP
    s_buf, m_buf, o_buf = (s0, s1), (m0, m1), (o0, o1)

    @pl.when(pl.program_id(1) == 0)
    def _():
        for hh in range(2):
            e_ref[hh] = _bias_tile(f_ref[hh], A_BIAS_ROWS, Q_GROUP)

    zeros = jnp.zeros((hd, Q_GROUP), bf16)
    ones = jnp.ones((ONES_ROWS, A_KEYS), bf16)

    def offsets(g):
        q0 = pl.multiple_of(g * Q_GROUP, Q_GROUP)
        w0 = pl.multiple_of(jnp.maximum(g * Q_GROUP - WINDOW, 0), Q_GROUP)
        e0 = pl.multiple_of(WINDOW - (g * Q_GROUP - w0), Q_GROUP)
        return q0, w0, e0

    def scores(g, par):
        q0, w0, e0 = offsets(g)
        k_win = k_ref[pl.ds(w0, A_KEYS), :]
        for hh in range(2):
            qh = qT_ref[hd * hh:hd * (hh + 1), pl.ds(q0, Q_GROUP)]
            w = jnp.concatenate([qh, zeros] if hh == 0 else [zeros, qh], axis=0)
            s = _dot(k_win, w) + e_ref[hh, pl.ds(e0, A_KEYS), :]
            s_buf[par][hh] = s
            m_buf[par][hh] = jnp.max(s, axis=0, keepdims=True)

    def values(g, par):
        _, w0, _ = offsets(g)
        for hh in range(2):
            p = jnp.exp2(s_buf[par][hh] - m_buf[par][hh]).astype(bf16)
            v_ext = jnp.concatenate([vT_ref[hd * hh:hd * (hh + 1), pl.ds(w0, A_KEYS)], ones], axis=0)
            o_buf[par][hh] = _dot(v_ext, p)

    def finish(g, par):
        q0, _, _ = offsets(g)
        outs = [_normalise(o_buf[par][hh], hd) for hh in range(2)]
        rows = pl.ds(q0, Q_GROUP)
        o_ref[rows, :] = _gated(jnp.concatenate(outs, axis=0), zs_ref, rows)

    scores(0, 0)
    scores(1, 1)
    values(0, 0)

    def body(i, carry):
        g = 2 * i + 1
        scores(g + 1, 0)
        values(g, 1)
        finish(g - 1, 0)
        scores(g + 2, 1)
        values(g + 1, 0)
        finish(g, 1)
        return carry

    lax.fori_loop(0, (n_groups - 2) // 2, body, 0)
    values(n_groups - 1, 1)
    finish(n_groups - 2, 0)
    finish(n_groups - 1, 1)


def _a_attention(f_rows, k_nat, q_t, v_t, zs):
    b, n_pairs, l, pair = k_nat.shape
    slab = pl.BlockSpec((None, None, l, pair), lambda p, bi: (bi, p, 0, 0))
    scratch = ([pltpu.VMEM((2, A_BIAS_ROWS, Q_GROUP), f32)]
               + [pltpu.VMEM((2, A_KEYS, Q_GROUP), f32)] * 2
               + [pltpu.VMEM((2, 1, Q_GROUP), f32)] * 2
               + [pltpu.VMEM((2, A_HEAD_DIM + ONES_ROWS, Q_GROUP), f32)] * 2)
    return pl.pallas_call(
        functools.partial(_a_attn_body, l),
        grid=(n_pairs, b),
        in_specs=[pl.BlockSpec((2, 1, A_BIAS_SPAN), lambda p, bi: (p, 0, 0)),
                  slab,
                  pl.BlockSpec((None, pair, l), lambda p, bi: (bi, p, 0)),
                  pl.BlockSpec((None, pair, l), lambda p, bi: (bi, p, 0)),
                  slab],
        out_specs=slab,
        out_shape=jax.ShapeDtypeStruct((b, n_pairs, l, pair), bf16),
        scratch_shapes=scratch,
        compiler_params=_params("arbitrary", "arbitrary"),
        name="a_attention",
    )(f_rows, k_nat, q_t, v_t, zs)


def _a_attn_bounded_body(seq, shift_ref, f_ref, k_ref, qT_ref, vT_ref, zs_ref, o_ref, e_ref, p0, p1, o0, o1):
    hd = A_HEAD_DIM
    n_groups = seq // Q_GROUP
    n_pairs = k_ref.shape[0]
    n_items = n_pairs * n_groups
    p_buf, o_buf = (p0, p1), (o0, o1)

    @pl.when(pl.program_id(1) == 0)
    def _():
        for h in range(2 * n_pairs):
            e_ref[h] = _bias_tile(f_ref[h], A_BIAS_ROWS, Q_GROUP) - shift_ref[0]

    zeros = jnp.zeros((hd, Q_GROUP), bf16)
    ones = jnp.ones((ONES_ROWS, A_KEYS), bf16)

    def offsets(n):
        pp = n // n_groups
        g = n - pp * n_groups
        q0 = pl.multiple_of(g * Q_GROUP, Q_GROUP)
        w0 = pl.multiple_of(jnp.maximum(g * Q_GROUP - WINDOW, 0), Q_GROUP)
        e0 = pl.multiple_of(WINDOW - (g * Q_GROUP - w0), Q_GROUP)
        return pp, q0, w0, e0

    def probs(n, par):
        pp, q0, w0, e0 = offsets(n)
        k_win = k_ref[pp, pl.ds(w0, A_KEYS), :]
        for hh in range(2):
            r0 = pl.multiple_of((2 * pp + hh) * hd, hd)
            qh = qT_ref[pl.ds(r0, hd), pl.ds(q0, Q_GROUP)]
            w = jnp.concatenate([qh, zeros] if hh == 0 else [zeros, qh], axis=0)
            bias = e_ref[2 * pp + hh, pl.ds(e0, A_KEYS), :]
            p_buf[par][hh] = jnp.exp2(_dot(k_win, w) + bias).astype(bf16)

    def values(n, par):
        pp, _, w0, _ = offsets(n)
        for hh in range(2):
            r0 = pl.multiple_of((2 * pp + hh) * hd, hd)
            v_ext = jnp.concatenate([vT_ref[pl.ds(r0, hd), pl.ds(w0, A_KEYS)], ones], axis=0)
            o_buf[par][hh] = _dot(v_ext, p_buf[par][hh])

    def finish(n, par):
        pp, q0, _, _ = offsets(n)
        outs = [_normalise(o_buf[par][hh], hd) for hh in range(2)]
        rows = pl.ds(q0, Q_GROUP)
        o_ref[pp, rows, :] = _gated(jnp.concatenate(outs, axis=0), zs_ref.at[pp], rows)

    probs(0, 0)
    probs(1, 1)
    values(0, 0)

    def body(i, carry):
        n = 2 * i + 1
        probs(n + 1, 0)
        values(n, 1)
        finish(n - 1, 0)
        probs(n + 2, 1)
        values(n + 1, 0)
        finish(n, 1)
        return carry

    lax.fori_loop(0, (n_items - 2) // 2, body, 0)
    values(n_items - 1, 1)
    finish(n_items - 2, 0)
    finish(n_items - 1, 1)


def _a_attention_bounded(shift, f_rows, k_nat, q_t, v_t, zs):
    b, n_pairs, l, pair = k_nat.shape
    pps = PAIRS_PER_STEP
    slab = pl.BlockSpec((None, pps, l, pair), lambda p, bi, sh: (bi, p, 0, 0))
    scratch = ([pltpu.VMEM((2 * pps, A_BIAS_ROWS, Q_GROUP), f32)]
               + [pltpu.VMEM((2, A_KEYS, Q_GROUP), bf16)] * 2
               + [pltpu.VMEM((2, A_HEAD_DIM + ONES_ROWS, Q_GROUP), f32)] * 2)
    grid_spec = pltpu.PrefetchScalarGridSpec(
        num_scalar_prefetch=1,
        grid=(n_pairs // pps, b),
        in_specs=[pl.BlockSpec((2 * pps, 1, A_BIAS_SPAN), lambda p, bi, sh: (p, 0, 0)),
                  slab,
                  pl.BlockSpec((None, pps * pair, l), lambda p, bi, sh: (bi, p, 0)),
                  pl.BlockSpec((None, pps * pair, l), lambda p, bi, sh: (bi, p, 0)),
                  slab],
        out_specs=slab,
        scratch_shapes=scratch)
    return pl.pallas_call(
        functools.partial(_a_attn_bounded_body, l),
        grid_spec=grid_spec,
        out_shape=jax.ShapeDtypeStruct((b, n_pairs, l, pair), bf16),
        compiler_params=_params("arbitrary", "arbitrary"),
        name="a_attention_bounded",
    )(shift, f_rows, k_nat, q_t, v_t, zs)


def _a_sattn_body(n_valid, f_ref, kc_ref, kn_ref, vc_ref, vn_ref, qT_ref, zs_ref, o_ref, e_ref):
    hd = A_HEAD_DIM
    n_keys = e_ref.shape[1]
    n_pairs = o_ref.shape[0]

    @pl.when(pl.program_id(1) == 0)
    def _():
        live = lax.broadcasted_iota(jnp.int32, (n_keys, SAMPLE_PAD), 0) < n_valid
        for h in range(2 * n_pairs):
            e_ref[h] = jnp.where(live, _bias_tile(f_ref[h], n_keys, SAMPLE_PAD), NEG_INF)

    zeros = jnp.zeros((hd, SAMPLE_PAD), bf16)
    for pp in range(n_pairs):
        lanes = slice(LANES * pp, LANES * (pp + 1))
        k_all = jnp.concatenate([kc_ref[:, lanes].astype(bf16), kn_ref[pp]], axis=0)
        v_t = jnp.concatenate([vc_ref[:, lanes], vn_ref[:, lanes]], axis=0).T.astype(bf16)
        outs = []
        for hh in range(2):
            qh = qT_ref[LANES * pp + hd * hh:LANES * pp + hd * (hh + 1), :]
            w = jnp.concatenate([qh, zeros] if hh == 0 else [zeros, qh], axis=0)
            p, l = _softmax_cols(_dot(k_all, w) + e_ref[2 * pp + hh])
            outs.append(_dot(v_t[hd * hh:hd * (hh + 1)], p.astype(bf16)) * (1.0 / l))
        o_ref[pp] = _gated(jnp.concatenate(outs, axis=0), zs_ref.at[pp], slice(None))


def _a_sample_attention(f_rows, k_cache, k_new, v_cache, v_new, q_t, zs, n_new):
    b, n_cache, d = k_cache.shape
    pair = 2 * A_HEAD_DIM
    pps = PAIRS_PER_STEP
    n_keys = n_cache + SAMPLE_PAD
    new = pl.BlockSpec((None, SAMPLE_PAD, pps * pair), lambda p, bi: (bi, 0, p))
    old = pl.BlockSpec((None, n_cache, pps * pair), lambda p, bi: (bi, 0, p))
    slab = pl.BlockSpec((None, pps, SAMPLE_PAD, pair), lambda p, bi: (bi, p, 0, 0))
    return pl.pallas_call(
        functools.partial(_a_sattn_body, n_cache + n_new),
        grid=(d // pair // pps, b),
        in_specs=[pl.BlockSpec((2 * pps, 1, A_BIAS_SPAN), lambda p, bi: (p, 0, 0)),
                  old, slab, old, new,
                  pl.BlockSpec((None, pps * pair, SAMPLE_PAD), lambda p, bi: (bi, p, 0)),
                  slab],
        out_specs=slab,
        out_shape=jax.ShapeDtypeStruct((b, d // pair, SAMPLE_PAD, pair), bf16),
        scratch_shapes=[pltpu.VMEM((2 * pps, n_keys, SAMPLE_PAD), f32)],
        compiler_params=_params("arbitrary", "arbitrary"),
        name="a_sample_attention",
    )(f_rows, k_cache, k_new, v_cache, v_new, q_t, zs)


def _out_body(oz_ref, w_ref, x_ref, mod_ref, y_ref):
    y = _dot(_load_lane_tiles(oz_ref), w_ref[...])
    y_ref[...] = x_ref[...] + mod_ref[2:3, :] * y


def _out_project(oz, w_out, x, mod, tm):
    b, l, d = x.shape
    tile = pl.BlockSpec((None, tm, d), lambda bi, i: (bi, i, 0))
    return pl.pallas_call(
        _out_body,
        grid=(b, l // tm),
        in_specs=[_pair_tiles(tm), _weight(w_out.shape), tile,
                  pl.BlockSpec((None, 3, d), lambda bi, i: (bi, 0, 0))],
        out_specs=tile,
        out_shape=jax.ShapeDtypeStruct((b, l, d), f32),
        compiler_params=_params("arbitrary", "arbitrary"),
        name="out_project",
    )(oz, w_out, x, mod)


def _rotate(x, cos, sin):
    half = B_ROPE // 2
    x1, x2 = x[:half], x[half:]
    return jnp.concatenate([x1 * cos - x2 * sin, x2 * cos + x1 * sin], axis=0)


def _b_proj_body(oz_ref, wout_ref, x_ref, mod_prev_ref, mod_ref, g_ref,
                 wcq_ref, wckv_ref, wkr_ref, wz_ref, wuq_ref, wukv_ref,
                 gcq_ref, gckv_ref, gkr_ref, gqn_ref, gqr_ref, gkn_ref, cos_ref, sin_ref,
                 y_ref, qT_ref, ckv_ref, kr_ref, zs_ref, k_ref, vT_ref):
    y = x_ref[...] + mod_prev_ref[2:3, :] * _dot(_load_lane_tiles(oz_ref), wout_ref[...])
    y_ref[...] = y
    hb = _ada_hidden(y, mod_ref, g_ref)
    tm = hb.shape[0]
    cos = cos_ref[...]
    sin = sin_ref[...]
    cq_raw = _nt(wcq_ref[...], hb)
    z = _dot(hb, wz_ref[...])
    ckv_raw = _nt(wckv_ref[...], hb)
    kr_raw = _nt(wkr_ref[...], hb)
    cq = _rms_rows(cq_raw, gcq_ref[...], Q_LORA).astype(bf16)
    qT = _dot(wuq_ref[...], cq)
    ckv = _rms_rows(ckv_raw, gckv_ref[...], KV_LORA)
    kvT = _dot(wukv_ref[...], ckv.astype(bf16))
    _store_lane_tiles(zs_ref, _silu(z).astype(bf16))
    ckv_ref[...] = ckv.T
    kr = _rotate(_rms_rows(kr_raw, gkr_ref[...], B_ROPE), cos, sin)
    kr_pad = jnp.concatenate([kr, jnp.zeros((128 - B_ROPE, tm), f32)], axis=0)
    kr_ref[...] = kr_pad.T[:, :B_ROPE]
    _emit_keys_values(kvT, kr, gkn_ref[...], k_ref, vT_ref)
    gqn = gqn_ref[...]
    gqr = gqr_ref[...]
    for h in range(N_HEADS):
        base = B_QK * h
        nope = _rms_rows(qT[base:base + B_NOPE], gqn, B_NOPE)
        rope = _rotate(_rms_rows(qT[base + B_NOPE:base + B_QK], gqr, B_ROPE), cos, sin)
        qT_ref[base:base + B_QK, :] = jnp.concatenate([nope, rope], axis=0).astype(bf16)


def _b_project(oz_prev, w_out_prev, x, mod_prev, mod, g, w, cos_t, sin_t, tm):
    b, l, d = x.shape
    tile = pl.BlockSpec((None, tm, d), lambda bi, i: (bi, i, 0))
    mod_spec = pl.BlockSpec((None, 3, d), lambda bi, i: (bi, 0, 0))
    rot = pl.BlockSpec((B_ROPE // 2, tm), lambda bi, i: (0, i))
    nq = N_HEADS * B_QK
    return pl.pallas_call(
        _b_proj_body,
        grid=(b, l // tm),
        in_specs=[_pair_tiles(tm), _weight(w_out_prev.shape), tile, mod_spec, mod_spec, _const((1, d)),
                  _weight((Q_LORA, d)), _weight((KV_LORA, d)), _weight((B_ROPE, d)), _weight((d, d)),
                  _weight((nq, Q_LORA)), _weight(w["ukv_t"].shape),
                  _const((Q_LORA, 1)), _const((KV_LORA, 1)), _const((B_ROPE, 1)),
                  _const((B_NOPE, 1)), _const((B_ROPE, 1)), _const((B_NOPE, 1)), rot, rot],
        out_specs=[tile,
                   pl.BlockSpec((None, nq, tm), lambda bi, i: (bi, 0, i)),
                   pl.BlockSpec((None, tm, KV_LORA), lambda bi, i: (bi, i, 0)),
                   pl.BlockSpec((None, tm, B_ROPE), lambda bi, i: (bi, i, 0)),
                   _pair_tiles(tm),
                   pl.BlockSpec((None, N_HEADS, tm, LANES), lambda bi, i: (bi, 0, i, 0)),
                   pl.BlockSpec((None, N_HEADS * B_VDIM, tm), lambda bi, i: (bi, 0, i))],
        out_shape=[jax.ShapeDtypeStruct((b, l, d), f32),
                   jax.ShapeDtypeStruct((b, nq, l), bf16),
                   jax.ShapeDtypeStruct((b, l, KV_LORA), f32),
                   jax.ShapeDtypeStruct((b, l, B_ROPE), f32),
                   jax.ShapeDtypeStruct((b, N_PAIRS, l, LANES), bf16),
                   jax.ShapeDtypeStruct((b, N_HEADS, l, LANES), bf16),
                   jax.ShapeDtypeStruct((b, N_HEADS * B_VDIM, l), bf16)],
        compiler_params=_params("arbitrary", "arbitrary"),
        name="b_project",
    )(oz_prev, w_out_prev, x, mod_prev, mod, g, w["cq_t"], w["ckv_t"], w["kr_t"], w["z"], w["uq_t"], w["ukv_t"],
      w["g_cq"], w["g_ckv"], w["g_kr"], w["g_qn"], w["g_qr"], w["g_kn"], cos_t, sin_t)


def _b_kvup_body(ckv_ref, kr_ref, wukv_ref, gkn_ref, k_ref, vT_ref):
    kvT = _nt(wukv_ref[...], ckv_ref[...].astype(bf16))
    eye = (lax.broadcasted_iota(jnp.int32, (B_ROPE, B_ROPE), 0)
           == lax.broadcasted_iota(jnp.int32, (B_ROPE, B_ROPE), 1)).astype(bf16)
    krT = _nt(eye, kr_ref[...].astype(bf16))
    _emit_keys_values(kvT, krT, gkn_ref[...], k_ref, vT_ref)


def _emit_keys_values(kvT, krT, gkn, k_ref, vT_ref):
    tm = kvT.shape[1]
    key_chunk = (lax.broadcasted_iota(jnp.int32, (128 - B_QK, tm), 1) // CHUNK) % TILE_CHUNKS
    spare_row = lax.broadcasted_iota(jnp.int32, (128 - B_QK, tm), 0)
    chunk_rows = ((spare_row == key_chunk) | (spare_row == TILE_CHUNKS)).astype(f32)
    tail = jnp.concatenate([krT, chunk_rows], axis=0)
    per_head = B_NOPE + B_VDIM
    for h in range(N_HEADS):
        base = per_head * h
        kn = _rms_rows(kvT[base:base + B_NOPE], gkn, B_NOPE)
        k_ref[h] = jnp.concatenate([kn, tail], axis=0).T.astype(bf16)
        vT_ref[B_VDIM * h:B_VDIM * (h + 1), :] = kvT[base + B_NOPE:base + per_head].astype(bf16)


def _b_kv_up(ckv, kr, wukv_t, g_kn, tm):
    b, l, _ = ckv.shape
    return pl.pallas_call(
        _b_kvup_body,
        grid=(b, l // tm),
        in_specs=[pl.BlockSpec((None, tm, KV_LORA), lambda bi, i: (bi, i, 0)),
                  pl.BlockSpec((None, tm, B_ROPE), lambda bi, i: (bi, i, 0)),
                  _const(wukv_t.shape), _const((B_NOPE, 1))],
        out_specs=[pl.BlockSpec((None, N_HEADS, tm, 128), lambda bi, i: (bi, 0, i, 0)),
                   pl.BlockSpec((None, N_HEADS * B_VDIM, tm), lambda bi, i: (bi, 0, i))],
        out_shape=[jax.ShapeDtypeStruct((b, N_HEADS, l, 128), bf16),
                   jax.ShapeDtypeStruct((b, N_HEADS * B_VDIM, l), bf16)],
        compiler_params=_params("arbitrary", "arbitrary"),
        name="b_kv_up",
    )(ckv, kr, wukv_t, g_kn)


def _b_item_table(n_tiles, n_pairs=1):
    rows = [(pp, t, j, int(j == t)) for pp in range(n_pairs) for t in range(n_tiles) for j in range(t, -1, -1)]
    return np.asarray(rows, np.int32).T.copy()


def _b_items_per_step(n_items, odd_steps):
    return max(u for u in range(1, B_ITEMS_PER_STEP + 1)
               if n_items % u == 0 and n_items // u >= 3 and (not odd_steps or (n_items // u) % 2 == 1))


def _b_attn_body(seq, per, tab_ref, k_ref, qT_ref, vT_ref, zs_ref, o_ref,
                 s0, s1, c0, c1, pv0, pv1, a0, a1, m_st, acc_st):
    t_sz = Q_GROUP
    n_tiles = seq // t_sz
    n_steps = n_tiles * (n_tiles + 1) // 2 // per
    s_buf, c_buf, pv_buf, a_buf = (s0, s1), (c0, c1), (pv0, pv1), (a0, a1)
    ones = jnp.ones((ONES_ROWS, t_sz), bf16)
    key_chunk = lax.broadcasted_iota(jnp.int32, (128 - B_QK, t_sz), 0)
    qry_chunk = lax.broadcasted_iota(jnp.int32, (128 - B_QK, t_sz), 1) // CHUNK
    mask_rows = jnp.where((key_chunk < t_sz // CHUNK) & (key_chunk > qry_chunk), NEG_INF, 0.0).astype(bf16)
    zero_rows = jnp.zeros((128 - B_QK, t_sz), bf16)
    m_st[...] = jnp.full(m_st.shape, NEG_INF, f32)
    acc_st[...] = jnp.zeros(acc_st.shape, f32)

    def item(step, u):
        n = step * per + u
        return tab_ref[1, n], tab_ref[2, n], tab_ref[3, n] == 1

    def scores(step, par, u):
        t, j, diag = item(step, u)
        q0 = pl.multiple_of(t * t_sz, t_sz)
        k0 = pl.multiple_of(j * t_sz, t_sz)
        pad = jnp.where(diag, mask_rows, zero_rows)
        for hh in range(2):
            w = jnp.concatenate([qT_ref[B_QK * hh:B_QK * (hh + 1), pl.ds(q0, t_sz)], pad], axis=0)
            s = _dot(k_ref[hh, pl.ds(k0, t_sz), :], w)
            s_buf[par][u, hh] = s
            c_buf[par][u, hh] = jnp.max(s, axis=0, keepdims=True)

    def values(step, par, u):
        t, j, _ = item(step, u)
        k0 = pl.multiple_of(j * t_sz, t_sz)
        for hh in range(2):
            m_old = m_st[t, hh]
            m_new = jnp.maximum(m_old, c_buf[par][u, hh])
            m_st[t, hh] = m_new
            a_buf[par][u, hh] = jnp.exp2(m_old - m_new)
            p = jnp.exp2(s_buf[par][u, hh] - m_new).astype(bf16)
            v_ext = jnp.concatenate([vT_ref[B_VDIM * hh:B_VDIM * (hh + 1), pl.ds(k0, t_sz)], ones], axis=0)
            pv_buf[par][u, hh] = _dot(v_ext, p)

    def accumulate(step, par, u):
        t, _, _ = item(step, u)
        for hh in range(2):
            acc_st[t, hh] = a_buf[par][u, hh] * acc_st[t, hh] + pv_buf[par][u, hh]

    def time_step(tau, par, first=False, drain=0):
        for u in range(per):
            if drain == 0:
                scores(tau + 1, 1 - par, u)
            if drain <= 1:
                values(tau, par, u)
            if not first:
                accumulate(tau - 1, 1 - par, u)

    for u in range(per):
        scores(0, 0, u)
    time_step(0, 0, first=True)

    def body(i, carry):
        tau = 2 * i + 1
        time_step(tau, 1)
        time_step(tau + 1, 0)
        return carry

    lax.fori_loop(0, (n_steps - 3) // 2, body, 0)
    time_step(n_steps - 2, 1)
    time_step(n_steps - 1, 0, drain=1)
    time_step(n_steps, 1, drain=2)
    for t in range(n_tiles):
        outs = [_normalise(acc_st[t, hh], B_VDIM) for hh in range(2)]
        rows = slice(t * t_sz, (t + 1) * t_sz)
        o_ref[rows, :] = _gated(jnp.concatenate(outs, axis=0), zs_ref, rows)


def _b_attention(k_nat, q_t, v_t, zs):
    b, _, l, _ = k_nat.shape
    t_sz = Q_GROUP
    n_tiles = l // t_sz
    table = _b_item_table(n_tiles)
    per = _b_items_per_step(table.shape[1], odd_steps=True)
    ext = B_VDIM + ONES_ROWS
    scratch = ([pltpu.VMEM((per, 2, t_sz, t_sz), f32)] * 2 + [pltpu.VMEM((per, 2, 1, t_sz), f32)] * 2
               + [pltpu.VMEM((per, 2, ext, t_sz), f32)] * 2 + [pltpu.VMEM((per, 2, 1, t_sz), f32)] * 2
               + [pltpu.VMEM((n_tiles, 2, 1, t_sz), f32), pltpu.VMEM((n_tiles, 2, ext, t_sz), f32)])
    grid_spec = pltpu.PrefetchScalarGridSpec(
        num_scalar_prefetch=1,
        grid=(b, N_HEADS // 2),
        in_specs=[pl.BlockSpec((None, 2, l, 128), lambda bi, p, tab: (bi, p, 0, 0)),
                  pl.BlockSpec((None, 2 * B_QK, l), lambda bi, p, tab: (bi, p, 0)),
                  pl.BlockSpec((None, 2 * B_VDIM, l), lambda bi, p, tab: (bi, p, 0)),
                  pl.BlockSpec((None, None, l, LANES), lambda bi, p, tab: (bi, p, 0, 0))],
        out_specs=pl.BlockSpec((None, None, l, LANES), lambda bi, p, tab: (bi, p, 0, 0)),
        scratch_shapes=scratch)
    return pl.pallas_call(
        functools.partial(_b_attn_body, l, per),
        grid_spec=grid_spec,
        out_shape=jax.ShapeDtypeStruct((b, N_PAIRS, l, LANES), bf16),
        compiler_params=_params("arbitrary", "arbitrary"),
        name="b_attention",
    )(jnp.asarray(table), k_nat, q_t, v_t, zs)


def _b_attn_bounded_body(seq, per, tab_ref, shift_ref, k_ref, qT_ref, vT_ref, zs_ref, o_ref, p0, p1, acc_st):
    t_sz = Q_GROUP
    n_tiles = seq // t_sz
    n_pairs = acc_st.shape[0]
    n_steps = n_pairs * n_tiles * (n_tiles + 1) // 2 // per
    p_buf = (p0, p1)
    ones = jnp.ones((ONES_ROWS, t_sz), bf16)
    row = lax.broadcasted_iota(jnp.int32, (128 - B_QK, t_sz), 0)
    qry_chunk = lax.broadcasted_iota(jnp.int32, (128 - B_QK, t_sz), 1) // CHUNK
    shift_rows = jnp.where(row == TILE_CHUNKS, -shift_ref[0], 0.0)
    mask_rows = (jnp.where((row < TILE_CHUNKS) & (row > qry_chunk), NEG_INF, 0.0) + shift_rows).astype(bf16)
    plain_rows = shift_rows.astype(bf16)
    acc_st[...] = jnp.zeros(acc_st.shape, f32)

    def item(step, u):
        n = step * per + u
        return tab_ref[0, n], tab_ref[1, n], tab_ref[2, n], tab_ref[3, n] == 1

    def probs(step, par, u):
        pp, t, j, diag = item(step, u)
        q0 = pl.multiple_of(t * t_sz, t_sz)
        k0 = pl.multiple_of(j * t_sz, t_sz)
        pad = jnp.where(diag, mask_rows, plain_rows)
        for hh in range(2):
            r0 = pl.multiple_of((2 * pp + hh) * B_QK, B_ROPE)
            w = jnp.concatenate([qT_ref[pl.ds(r0, B_QK), pl.ds(q0, t_sz)], pad], axis=0)
            p_buf[par][u, hh] = jnp.exp2(_dot(k_ref[2 * pp + hh, pl.ds(k0, t_sz), :], w)).astype(bf16)

    def values(step, par, u):
        pp, t, j, _ = item(step, u)
        k0 = pl.multiple_of(j * t_sz, t_sz)
        for hh in range(2):
            r0 = pl.multiple_of((2 * pp + hh) * B_VDIM, B_VDIM)
            v_ext = jnp.concatenate([vT_ref[pl.ds(r0, B_VDIM), pl.ds(k0, t_sz)], ones], axis=0)
            acc_st[pp, t, hh] += _dot(v_ext, p_buf[par][u, hh])

    def time_step(tau, last):
        par = tau % 2
        for u in range(per):
            if not last:
                probs(tau + 1, 1 - par, u)
            values(tau, par, u)

    def step_pair(i, carry):
        for par in range(2):
            for u in range(per):
                probs(2 * i + par + 1, 1 - par, u)
                values(2 * i + par, par, u)
        return carry

    for u in range(per):
        probs(0, 0, u)
    n_loop = (n_steps - 1) // 2
    lax.fori_loop(0, n_loop, step_pair, 0)
    for tau in range(2 * n_loop, n_steps):
        time_step(tau, last=tau == n_steps - 1)
    for pp in range(n_pairs):
        for t in range(n_tiles):
            outs = [_normalise(acc_st[pp, t, hh], B_VDIM) for hh in range(2)]
            rows = slice(t * t_sz, (t + 1) * t_sz)
            o_ref[pp, rows, :] = _gated(jnp.concatenate(outs, axis=0), zs_ref.at[pp], rows)


def _b_attention_bounded(shift, k_nat, q_t, v_t, zs):
    b, _, l, _ = k_nat.shape
    t_sz = Q_GROUP
    n_tiles = l // t_sz
    pps = PAIRS_PER_STEP
    table = _b_item_table(n_tiles, pps)
    per = _b_items_per_step(table.shape[1], odd_steps=False)
    scratch = ([pltpu.VMEM((per, 2, t_sz, t_sz), bf16)] * 2
               + [pltpu.VMEM((pps, n_tiles, 2, B_VDIM + ONES_ROWS, t_sz), f32)])
    slab = pl.BlockSpec((None, pps, l, LANES), lambda bi, p, tab, sh: (bi, p, 0, 0))
    grid_spec = pltpu.PrefetchScalarGridSpec(
        num_scalar_prefetch=2,
        grid=(b, N_PAIRS // pps),
        in_specs=[pl.BlockSpec((None, 2 * pps, l, 128), lambda bi, p, tab, sh: (bi, p, 0, 0)),
                  pl.BlockSpec((None, 2 * pps * B_QK, l), lambda bi, p, tab, sh: (bi, p, 0)),
                  pl.BlockSpec((None, 2 * pps * B_VDIM, l), lambda bi, p, tab, sh: (bi, p, 0)),
                  slab],
        out_specs=slab,
        scratch_shapes=scratch)
    return pl.pallas_call(
        functools.partial(_b_attn_bounded_body, l, per),
        grid_spec=grid_spec,
        out_shape=jax.ShapeDtypeStruct((b, N_PAIRS, l, LANES), bf16),
        compiler_params=_params("arbitrary", "arbitrary"),
        name="b_attention_bounded",
    )(jnp.asarray(table), shift, k_nat, q_t, v_t, zs)


def _b_sattn_body(n_new, kc_ref, kn_ref, vc_ref, vn_ref, qT_ref, zs_ref, o_ref):
    t = SAMPLE_PAD
    zpad = jnp.zeros((128 - B_QK, t), bf16)
    new_mask = jnp.where(lax.broadcasted_iota(jnp.int32, (t, t), 0) < n_new, 0.0, NEG_INF).astype(f32)
    for pp in range(o_ref.shape[0]):
        outs = []
        for h in (2 * pp, 2 * pp + 1):
            w = jnp.concatenate([qT_ref[B_QK * h:B_QK * (h + 1), :], zpad], axis=0)
            s_c = _dot(kc_ref[h], w)
            s_n = _dot(kn_ref[h], w) + new_mask
            m = jnp.maximum(jnp.max(s_c, axis=0, keepdims=True), jnp.max(s_n, axis=0, keepdims=True))
            p_c = jnp.exp2(s_c - m)
            p_n = jnp.exp2(s_n - m)
            l = jnp.sum(p_c, axis=0, keepdims=True) + jnp.sum(p_n, axis=0, keepdims=True)
            rows = slice(B_VDIM * h, B_VDIM * (h + 1))
            o_t = _dot(vc_ref[rows, :], p_c.astype(bf16)) + _dot(vn_ref[rows, :], p_n.astype(bf16))
            outs.append(o_t * (1.0 / l))
        o_ref[pp] = _gated(jnp.concatenate(outs, axis=0), zs_ref.at[pp], slice(None))


def _b_sample_attention(k_cache, k_new, v_cache, v_new, q_t, zs, n_new):
    b, _, past, _ = k_cache.shape
    t = SAMPLE_PAD
    pps = PAIRS_PER_STEP
    slab = pl.BlockSpec((None, pps, t, LANES), lambda bi, p: (bi, p, 0, 0))
    return pl.pallas_call(
        functools.partial(_b_sattn_body, n_new),
        grid=(b, N_PAIRS // pps),
        in_specs=[pl.BlockSpec((None, 2 * pps, past, 128), lambda bi, p: (bi, p, 0, 0)),
                  pl.BlockSpec((None, 2 * pps, t, 128), lambda bi, p: (bi, p, 0, 0)),
                  pl.BlockSpec((None, 2 * pps * B_VDIM, past), lambda bi, p: (bi, p, 0)),
                  pl.BlockSpec((None, 2 * pps * B_VDIM, t), lambda bi, p: (bi, p, 0)),
                  pl.BlockSpec((None, 2 * pps * B_QK, t), lambda bi, p: (bi, p, 0)),
                  slab],
        out_specs=slab,
        out_shape=jax.ShapeDtypeStruct((b, N_PAIRS, t, LANES), bf16),
        compiler_params=_params("arbitrary", "arbitrary"),
        name="b_sample_attention",
    )(k_cache, k_new, v_cache, v_new, q_t, zs)


def _bias_rows(table):
    left = A_BIAS_ROWS - 1 - WINDOW - REL_CLIP
    right = A_BIAS_SPAN - left - table.shape[-1]
    return (jnp.pad(table.astype(f32), ((0, 0), (left, right)), mode="edge") * LOG2E)[:, None, :]


def _rope_tables(pos):
    half = B_ROPE // 2
    inv = ROPE_THETA ** (-jnp.arange(half, dtype=f32) / half)
    ang = pos.astype(f32)[:, None] * inv[None, :]
    return jnp.cos(ang).T, jnp.sin(ang).T


def _col(g, scale=1.0):
    return (g.astype(f32) * scale)[:, None]


def kernel(x_prompt, x_sample, cache_a_k, cache_a_v, cache_mla_ckv, cache_mla_krope, c_prompt, c_sample, norm_g, ada_w, ada_b, a_w_in, a_g_q, a_g_k, a_rel_bias, a_w_out, b_w_in, b_g_cq, b_w_uq, b_g_ckv, b_w_ukv, b_g_qn, b_g_qr, b_g_kn, b_g_kr, b_w_out):
    bp, seq, d = x_prompt.shape
    bs, dec, _ = x_sample.shape
    past = cache_mla_ckv.shape[2]
    n_cache_a = cache_a_k.shape[2]
    assert d == N_HEADS * A_HEAD_DIM and seq % (2 * Q_GROUP) == 0 and seq >= A_KEYS
    assert past % CHUNK == 0 and dec <= CHUNK and dec <= SAMPLE_PAD and n_cache_a == WINDOW
    cache_rows = min(WINDOW, seq)
    tm = min(ROW_TILE, seq)

    mod = _modulation(jnp.concatenate([c_prompt, c_sample], axis=0), ada_w, ada_b)
    mod = mod.reshape(mod.shape[0], bp + bs, 3, d)
    xs_pad = jnp.pad(x_sample, ((0, 0), (0, SAMPLE_PAD - dec), (0, 0)))

    w_in = a_w_in[0]
    wq_t, wk_t, wv_t = (w_in[:, d * n:d * (n + 1)].T.astype(bf16) for n in range(3))
    wz = w_in[:, 3 * d:].astype(bf16)
    w_out_a = a_w_out[0].astype(bf16)
    gq = _col(a_g_q[0], A_HEAD_DIM ** -0.5 * LOG2E)
    gk = _col(a_g_k[0])
    g0 = norm_g[0][None, :]
    f_rows = _bias_rows(a_rel_bias[0])
    mod_p0, mod_s0 = mod[0, :bp], mod[0, bp:]

    q_t, k_nat, v_t, zs, k32, v32 = _a_project(x_prompt, mod_p0, g0, wq_t, wk_t, wv_t, wz, gq, gk, tm, cache_rows)
    qk_max = A_HEAD_DIM * jnp.max(jnp.abs(gq)) * jnp.max(jnp.abs(gk)) * BOUND_MARGIN
    upper = qk_max + jnp.max(f_rows)
    oz_p0 = lax.cond(upper - (jnp.min(f_rows) - qk_max) <= MAX_LOGIT_RANGE,
                     lambda: _a_attention_bounded(upper[None], f_rows, k_nat, q_t, v_t, zs),
                     lambda: _a_attention(f_rows, k_nat, q_t, v_t, zs))
    new_a_k_p = k32.reshape(1, bp, cache_rows, N_HEADS, A_HEAD_DIM)
    new_a_v_p = v32.reshape(1, bp, cache_rows, N_HEADS, A_HEAD_DIM)

    q_t, k_nat, v_t, zs, k32, v32 = _a_project(xs_pad, mod_s0, g0, wq_t, wk_t, wv_t, wz, gq, gk,
                                               SAMPLE_PAD, SAMPLE_PAD)
    oz_s0 = _a_sample_attention(f_rows, cache_a_k[0].reshape(bs, n_cache_a, d), k_nat,
                                cache_a_v[0].reshape(bs, n_cache_a, d), v32, q_t, zs, dec)
    new_a_k_s = k32[:, :dec].reshape(1, bs, dec, N_HEADS, A_HEAD_DIM)
    new_a_v_s = v32[:, :dec].reshape(1, bs, dec, N_HEADS, A_HEAD_DIM)

    w_in = b_w_in[0]
    scale = B_QK ** -0.5 * LOG2E
    wb = {
        "cq_t": w_in[:, :Q_LORA].T.astype(bf16),
        "ckv_t": w_in[:, Q_LORA:Q_LORA + KV_LORA].T.astype(bf16),
        "kr_t": w_in[:, Q_LORA + KV_LORA:Q_LORA + KV_LORA + B_ROPE].T.astype(bf16),
        "z": w_in[:, Q_LORA + KV_LORA + B_ROPE:].astype(bf16),
        "uq_t": b_w_uq[0].T.astype(bf16),
        "g_cq": _col(b_g_cq[0]), "g_ckv": _col(b_g_ckv[0]), "g_kr": _col(b_g_kr[0]),
        "g_qn": _col(b_g_qn[0], scale), "g_qr": _col(b_g_qr[0], scale),
        "ukv_t": b_w_ukv[0].T.astype(bf16), "g_kn": _col(b_g_kn[0]),
    }
    wukv_t, g_kn = wb["ukv_t"], wb["g_kn"]
    w_out = b_w_out[0].astype(bf16)
    g1 = norm_g[1][None, :]
    mod_p, mod_s = mod[1, :bp], mod[1, bp:]

    cos_t, sin_t = _rope_tables(jnp.arange(seq, dtype=jnp.int32))
    y_p, q_t, ckv_p, kr_p, zs, k_nat, v_t = _b_project(oz_p0, w_out_a, x_prompt, mod_p0, mod_p, g1, wb,
                                                       cos_t, sin_t, min(B_PROJ_TILE, seq))
    q_sq = B_NOPE * jnp.max(jnp.abs(wb["g_qn"])) ** 2 + B_ROPE * jnp.max(jnp.abs(wb["g_qr"])) ** 2
    k_sq = B_NOPE * jnp.max(jnp.abs(g_kn)) ** 2 + B_ROPE * jnp.max(jnp.abs(wb["g_kr"])) ** 2
    qk_max = jnp.sqrt(q_sq * k_sq) * BOUND_MARGIN
    oz = lax.cond(2.0 * qk_max <= MAX_LOGIT_RANGE,
                  lambda: _b_attention_bounded(qk_max[None], k_nat, q_t, v_t, zs),
                  lambda: _b_attention(k_nat, q_t, v_t, zs))
    y_p = _out_project(oz, w_out, y_p, mod_p, min(OUT_TILE, seq))

    cos_t, sin_t = _rope_tables(past + jnp.arange(SAMPLE_PAD, dtype=jnp.int32))
    ys_pad, q_t, ckv_s, kr_s, zs, k_new, v_new = _b_project(oz_s0, w_out_a, xs_pad, mod_s0, mod_s, g1, wb,
                                                            cos_t, sin_t, SAMPLE_PAD)
    k_old, v_old = _b_kv_up(cache_mla_ckv[0], cache_mla_krope[0], wukv_t, g_kn, tm)
    oz = _b_sample_attention(k_old, k_new, v_old, v_new, q_t, zs, dec)
    ys_pad = _out_project(oz, w_out, ys_pad, mod_s, SAMPLE_PAD)

    return (y_p, ys_pad[:, :dec], new_a_k_p, new_a_v_p, new_a_k_s, new_a_v_s,
            ckv_p[None], kr_p[None], ckv_s[None, :, :dec], kr_s[None, :, :dec])
```

```python
import functools

import numpy as np

import jax
import jax.numpy as jnp
from jax import lax
from jax.experimental import pallas as pl
from jax.experimental.pallas import tpu as pltpu

f32 = jnp.float32
bf16 = jnp.bfloat16

EPS = 1e-6
NEG_INF = -1e30
CHUNK = 64
BAND_CHUNKS = 8
WINDOW = BAND_CHUNKS * CHUNK
REL_CLIP = 128
ROPE_THETA = 10000.0
LOG2E = 1.4426950408889634

LANES = 128
N_HEADS = 16
N_PAIRS = N_HEADS // 2
A_HEAD_DIM = 64
B_NOPE = 64
B_ROPE = 32
B_VDIM = 64
B_QK = B_NOPE + B_ROPE
KV_LORA = 256
Q_LORA = 384

Q_GROUP = 256
TILE_CHUNKS = Q_GROUP // CHUNK
MAX_LOGIT_RANGE = 100.0
BOUND_MARGIN = 1.02
A_KEYS = WINDOW + Q_GROUP
A_BIAS_ROWS = A_KEYS + WINDOW
A_BIAS_SPAN = A_BIAS_ROWS + Q_GROUP
SAMPLE_PAD = 128
ROW_TILE = 512
OUT_TILE = 2048
B_PROJ_TILE = 512
B_ITEMS_PER_STEP = 4
PAIRS_PER_STEP = 4
ONES_ROWS = 8
VMEM_LIMIT = 56 * 1024 * 1024


def _silu(x):
    return x * (1.0 / (1.0 + jnp.exp(-x)))


def _nt(a, b):
    return lax.dot_general(a, b, (((1,), (1,)), ((), ())), preferred_element_type=f32)


def _dot(a, b):
    return jnp.dot(a, b, preferred_element_type=f32)


def _rms_rows(blk, gain_col, n):
    ms = jnp.sum(blk * blk, axis=0, keepdims=True) * (1.0 / n)
    return blk * lax.rsqrt(ms + EPS) * gain_col


def _ada_hidden(x, mod_ref, g_ref):
    ms = jnp.mean(x * x, axis=-1, keepdims=True)
    xn = x * lax.rsqrt(ms + EPS) * g_ref[...]
    return (xn * (1.0 + mod_ref[1:2, :]) + mod_ref[0:1, :]).astype(bf16)


def _softmax_cols(s):
    m = jnp.max(s, axis=0, keepdims=True)
    p = jnp.exp2(s - m)
    return p, jnp.sum(p, axis=0, keepdims=True)


def _store_lane_tiles(ref, val):
    for p in range(ref.shape[0]):
        ref[p] = val[:, LANES * p:LANES * (p + 1)]


def _load_lane_tiles(ref):
    return jnp.concatenate([ref[p] for p in range(ref.shape[0])], axis=1)


def _gated(o_t, zs_ref, rows):
    return (o_t.T * zs_ref[rows, :].astype(f32)).astype(bf16)


def _normalise(o_ext, rows):
    return o_ext[:rows] * (1.0 / o_ext[rows:rows + 1])


def _bias_tile(f_row, n_rows, n_cols):
    blocks = []
    for a in range(pl.cdiv(n_rows, Q_GROUP)):
        lo = A_BIAS_ROWS - Q_GROUP * (a + 1)
        x = jnp.broadcast_to(f_row[:, lo:lo + 2 * Q_GROUP], (Q_GROUP, 2 * Q_GROUP))
        y = pltpu.roll(x, Q_GROUP + 1, 1, stride=1, stride_axis=0)[:, :n_cols]
        r = lax.broadcasted_iota(jnp.int32, (Q_GROUP, n_cols), 0) + Q_GROUP * a
        i = lax.broadcasted_iota(jnp.int32, (Q_GROUP, n_cols), 1)
        back = BAND_CHUNKS + i // CHUNK - r // CHUNK
        blocks.append(jnp.where((back >= 0) & (back <= BAND_CHUNKS), y, NEG_INF))
    return jnp.concatenate(blocks, axis=0)[:n_rows]


def _params(*sem):
    return pltpu.CompilerParams(dimension_semantics=sem, vmem_limit_bytes=VMEM_LIMIT)


def _const(shape):
    nd = len(shape)
    return pl.BlockSpec(shape, lambda *_: (0,) * nd)


def _pair_tiles(tm):
    return pl.BlockSpec((None, N_PAIRS, tm, LANES), lambda bi, i: (bi, 0, i, 0))


def _weight(shape):
    nd = len(shape)
    return pl.BlockSpec(shape, lambda *_: (0,) * nd, pipeline_mode=pl.Buffered(1))


def _mod_body(c_ref, w_ref, b_ref, o_ref):
    o_ref[...] = jnp.dot(_silu(c_ref[...]), w_ref[...], preferred_element_type=f32,
                         precision=lax.Precision.HIGHEST) + b_ref[...]


def _modulation(c_all, ada_w, ada_b):
    depth, d, n3 = ada_w.shape
    rows = c_all.shape[0]
    tn = d
    return pl.pallas_call(
        _mod_body,
        grid=(depth, n3 // tn),
        in_specs=[pl.BlockSpec((rows, d), lambda l, j: (0, 0)),
                  pl.BlockSpec((None, d, tn), lambda l, j: (l, 0, j)),
                  pl.BlockSpec((None, 1, tn), lambda l, j: (l, 0, j))],
        out_specs=pl.BlockSpec((None, rows, tn), lambda l, j: (l, 0, j)),
        out_shape=jax.ShapeDtypeStruct((depth, rows, n3), f32),
        compiler_params=_params("arbitrary", "arbitrary"),
        name="modulation",
    )(c_all, ada_w, ada_b.reshape(depth, 1, n3))


def _a_proj_body(n_skip, x_ref, mod_ref, g_ref, wq_ref, wk_ref, wv_ref, wz_ref, gq_ref, gk_ref,
                 qT_ref, k_ref, vT_ref, zs_ref, k32_ref, v32_ref, kf_ref):
    hb = _ada_hidden(x_ref[...], mod_ref, g_ref)
    z = _dot(hb, wz_ref[...])
    kT = _nt(wk_ref[...], hb)
    _store_lane_tiles(zs_ref, _silu(z).astype(bf16))
    qT = _nt(wq_ref[...], hb)
    gq = gq_ref[...]
    gk = gk_ref[...]
    hd = A_HEAD_DIM
    pairs = [slice(2 * hd * p, 2 * hd * (p + 1)) for p in range(N_HEADS // 2)]
    for p, rows in enumerate(pairs):
        k_pair = [_rms_rows(kT[hd * h:hd * (h + 1)], gk, hd) for h in (2 * p, 2 * p + 1)]
        k_nat = jnp.concatenate(k_pair, axis=0).T
        k_ref[p] = k_nat.astype(bf16)
        kf_ref[:, rows] = k_nat
    vT = _nt(wv_ref[...], hb)
    for p, rows in enumerate(pairs):
        q_pair = [_rms_rows(qT[hd * h:hd * (h + 1)], gq, hd) for h in (2 * p, 2 * p + 1)]
        qT_ref[rows, :] = jnp.concatenate(q_pair, axis=0).astype(bf16)
    vT_ref[...] = vT.astype(bf16)

    @pl.when(pl.program_id(1) >= n_skip)
    def _():
        k32_ref[...] = kf_ref[...]
        v32_ref[...] = vT.T


def _a_project(x, mod, g, wq_t, wk_t, wv_t, wz, gq, gk, tm, cache_rows):
    b, l, d = x.shape
    n_skip = (l - cache_rows) // tm
    tile = pl.BlockSpec((None, tm, d), lambda bi, i: (bi, i, 0))
    tile_t = pl.BlockSpec((None, d, tm), lambda bi, i: (bi, 0, i))
    cache = pl.BlockSpec((None, tm, d), lambda bi, i: (bi, jnp.maximum(i - n_skip, 0), 0))
    return pl.pallas_call(
        functools.partial(_a_proj_body, n_skip),
        grid=(b, l // tm),
        in_specs=[tile, pl.BlockSpec((None, 3, d), lambda bi, i: (bi, 0, 0)), _const((1, d)),
                  _weight((d, d)), _weight((d, d)), _weight((d, d)), _weight((d, d)),
                  _const((A_HEAD_DIM, 1)), _const((A_HEAD_DIM, 1))],
        out_specs=[tile_t, _pair_tiles(tm), tile_t, _pair_tiles(tm), cache, cache],
        out_shape=[jax.ShapeDtypeStruct((b, d, l), bf16), jax.ShapeDtypeStruct((b, N_PAIRS, l, LANES), bf16),
                   jax.ShapeDtypeStruct((b, d, l), bf16), jax.ShapeDtypeStruct((b, N_PAIRS, l, LANES), bf16),
                   jax.ShapeDtypeStruct((b, cache_rows, d), f32),
                   jax.ShapeDtypeStruct((b, cache_rows, d), f32)],
        scratch_shapes=[pltpu.VMEM((tm, d), f32)],
        compiler_params=_params("arbitrary", "arbitrary"),
        name="a_project",
    )(x, mod, g, wq_t, wk_t, wv_t, wz, gq, gk)


def _a_attn_body(seq, f_ref, k_ref, qT_ref, vT_ref, zs_ref, o_ref, e_ref, s0, s1, m0, m1, o0, o1):
    hd = A_HEAD_DIM
    n_groups = seq // Q_GROUP
    s_buf, m_buf, o_buf = (s0, s1), (m0, m1), (o0, o1)

    @pl.when(pl.program_id(1) == 0)
    def _():
        for hh in range(2):
            e_ref[hh] = _bias_tile(f_ref[hh], A_BIAS_ROWS, Q_GROUP)

    zeros = jnp.zeros((hd, Q_GROUP), bf16)
    ones = jnp.ones((ONES_ROWS, A_KEYS), bf16)

    def offsets(g):
        q0 = pl.multiple_of(g * Q_GROUP, Q_GROUP)
        w0 = pl.multiple_of(jnp.maximum(g * Q_GROUP - WINDOW, 0), Q_GROUP)
        e0 = pl.multiple_of(WINDOW - (g * Q_GROUP - w0), Q_GROUP)
        return q0, w0, e0

    def scores(g, par):
        q0, w0, e0 = offsets(g)
        k_win = k_ref[pl.ds(w0, A_KEYS), :]
        for hh in range(2):
            qh = qT_ref[hd * hh:hd * (hh + 1), pl.ds(q0, Q_GROUP)]
            w = jnp.concatenate([qh, zeros] if hh == 0 else [zeros, qh], axis=0)
            s = _dot(k_win, w) + e_ref[hh, pl.ds(e0, A_KEYS), :]
            s_buf[par][hh] = s
            m_buf[par][hh] = jnp.max(s, axis=0, keepdims=True)

    def values(g, par):
        _, w0, _ = offsets(g)
        for hh in range(2):
            p = jnp.exp2(s_buf[par][hh] - m_buf[par][hh]).astype(bf16)
            v_ext = jnp.concatenate([vT_ref[hd * hh:hd * (hh + 1), pl.ds(w0, A_KEYS)], ones], axis=0)
            o_buf[par][hh] = _dot(v_ext, p)

    def finish(g, par):
        q0, _, _ = offsets(g)
        outs = [_normalise(o_buf[par][hh], hd) for hh in range(2)]
        rows = pl.ds(q0, Q_GROUP)
        o_ref[rows, :] = _gated(jnp.concatenate(outs, axis=0), zs_ref, rows)

    scores(0, 0)
    scores(1, 1)
    values(0, 0)

    def body(i, carry):
        g = 2 * i + 1
        scores(g + 1, 0)
        values(g, 1)
        finish(g - 1, 0)
        scores(g + 2, 1)
        values(g + 1, 0)
        finish(g, 1)
        return carry

    lax.fori_loop(0, (n_groups - 2) // 2, body, 0)
    values(n_groups - 1, 1)
    finish(n_groups - 2, 0)
    finish(n_groups - 1, 1)


def _a_attention(f_rows, k_nat, q_t, v_t, zs):
    b, n_pairs, l, pair = k_nat.shape
    slab = pl.BlockSpec((None, None, l, pair), lambda p, bi: (bi, p, 0, 0))
    scratch = ([pltpu.VMEM((2, A_BIAS_ROWS, Q_GROUP), f32)]
               + [pltpu.VMEM((2, A_KEYS, Q_GROUP), f32)] * 2
               + [pltpu.VMEM((2, 1, Q_GROUP), f32)] * 2
               + [pltpu.VMEM((2, A_HEAD_DIM + ONES_ROWS, Q_GROUP), f32)] * 2)
    return pl.pallas_call(
        functools.partial(_a_attn_body, l),
        grid=(n_pairs, b),
        in_specs=[pl.BlockSpec((2, 1, A_BIAS_SPAN), lambda p, bi: (p, 0, 0)),
                  slab,
                  pl.BlockSpec((None, pair, l), lambda p, bi: (bi, p, 0)),
                  pl.BlockSpec((None, pair, l), lambda p, bi: (bi, p, 0)),
                  slab],
        out_specs=slab,
        out_shape=jax.ShapeDtypeStruct((b, n_pairs, l, pair), bf16),
        scratch_shapes=scratch,
        compiler_params=_params("arbitrary", "arbitrary"),
        name="a_attention",
    )(f_rows, k_nat, q_t, v_t, zs)


def _a_attn_bounded_body(seq, shift_ref, f_ref, k_ref, qT_ref, vT_ref, zs_ref, o_ref, e_ref, p0, p1, o0, o1):
    hd = A_HEAD_DIM
    n_groups = seq // Q_GROUP
    n_pairs = k_ref.shape[0]
    n_items = n_pairs * n_groups
    p_buf, o_buf = (p0, p1), (o0, o1)

    @pl.when(pl.program_id(1) == 0)
    def _():
        for h in range(2 * n_pairs):
            e_ref[h] = _bias_tile(f_ref[h], A_BIAS_ROWS, Q_GROUP) - shift_ref[0]

    zeros = jnp.zeros((hd, Q_GROUP), bf16)
    ones = jnp.ones((ONES_ROWS, A_KEYS), bf16)

    def offsets(n):
        pp = n // n_groups
        g = n - pp * n_groups
        q0 = pl.multiple_of(g * Q_GROUP, Q_GROUP)
        w0 = pl.multiple_of(jnp.maximum(g * Q_GROUP - WINDOW, 0), Q_GROUP)
        e0 = pl.multiple_of(WINDOW - (g * Q_GROUP - w0), Q_GROUP)
        return pp, q0, w0, e0

    def probs(n, par):
        pp, q0, w0, e0 = offsets(n)
        k_win = k_ref[pp, pl.ds(w0, A_KEYS), :]
        for hh in range(2):
            r0 = pl.multiple_of((2 * pp + hh) * hd, hd)
            qh = qT_ref[pl.ds(r0, hd), pl.ds(q0, Q_GROUP)]
            w = jnp.concatenate([qh, zeros] if hh == 0 else [zeros, qh], axis=0)
            bias = e_ref[2 * pp + hh, pl.ds(e0, A_KEYS), :]
            p_buf[par][hh] = jnp.exp2(_dot(k_win, w) + bias).astype(bf16)

    def values(n, par):
        pp, _, w0, _ = offsets(n)
        for hh in range(2):
            r0 = pl.multiple_of((2 * pp + hh) * hd, hd)
            v_ext = jnp.concatenate([vT_ref[pl.ds(r0, hd), pl.ds(w0, A_KEYS)], ones], axis=0)
            o_buf[par][hh] = _dot(v_ext, p_buf[par][hh])

    def finish(n, par):
        pp, q0, _, _ = offsets(n)
        outs = [_normalise(o_buf[par][hh], hd) for hh in range(2)]
        rows = pl.ds(q0, Q_GROUP)
        o_ref[pp, rows, :] = _gated(jnp.concatenate(outs, axis=0), zs_ref.at[pp], rows)

    probs(0, 0)
    probs(1, 1)
    values(0, 0)

    def body(i, carry):
        n = 2 * i + 1
        probs(n + 1, 0)
        values(n, 1)
        finish(n - 1, 0)
        probs(n + 2, 1)
        values(n + 1, 0)
        finish(n, 1)
        return carry

    lax.fori_loop(0, (n_items - 2) // 2, body, 0)
    values(n_items - 1, 1)
    finish(n_items - 2, 0)
    finish(n_items - 1, 1)


def _a_attention_bounded(shift, f_rows, k_nat, q_t, v_t, zs):
    b, n_pairs, l, pair = k_nat.shape
    pps = PAIRS_PER_STEP
    slab = pl.BlockSpec((None, pps, l, pair), lambda p, bi, sh: (bi, p, 0, 0))
    scratch = ([pltpu.VMEM((2 * pps, A_BIAS_ROWS, Q_GROUP), f32)]
               + [pltpu.VMEM((2, A_KEYS, Q_GROUP), bf16)] * 2
               + [pltpu.VMEM((2, A_HEAD_DIM + ONES_ROWS, Q_GROUP), f32)] * 2)
    grid_spec = pltpu.PrefetchScalarGridSpec(
        num_scalar_prefetch=1,
        grid=(n_pairs // pps, b),
        in_specs=[pl.BlockSpec((2 * pps, 1, A_BIAS_SPAN), lambda p, bi, sh: (p, 0, 0)),
                  slab,
                  pl.BlockSpec((None, pps * pair, l), lambda p, bi, sh: (bi, p, 0)),
                  pl.BlockSpec((None, pps * pair, l), lambda p, bi, sh: (bi, p, 0)),
                  slab],
        out_specs=slab,
        scratch_shapes=scratch)
    return pl.pallas_call(
        functools.partial(_a_attn_bounded_body, l),
        grid_spec=grid_spec,
        out_shape=jax.ShapeDtypeStruct((b, n_pairs, l, pair), bf16),
        compiler_params=_params("arbitrary", "arbitrary"),
        name="a_attention_bounded",
    )(shift, f_rows, k_nat, q_t, v_t, zs)


def _a_sattn_body(n_valid, f_ref, kc_ref, kn_ref, vc_ref, vn_ref, qT_ref, zs_ref, o_ref, e_ref):
    hd = A_HEAD_DIM
    n_keys = e_ref.shape[1]
    n_pairs = o_ref.shape[0]

    @pl.when(pl.program_id(1) == 0)
    def _():
        live = lax.broadcasted_iota(jnp.int32, (n_keys, SAMPLE_PAD), 0) < n_valid
        for h in range(2 * n_pairs):
            e_ref[h] = jnp.where(live, _bias_tile(f_ref[h], n_keys, SAMPLE_PAD), NEG_INF)

    zeros = jnp.zeros((hd, SAMPLE_PAD), bf16)
    for pp in range(n_pairs):
        lanes = slice(LANES * pp, LANES * (pp + 1))
        k_all = jnp.concatenate([kc_ref[:, lanes].astype(bf16), kn_ref[pp]], axis=0)
        v_t = jnp.concatenate([vc_ref[:, lanes], vn_ref[:, lanes]], axis=0).T.astype(bf16)
        outs = []
        for hh in range(2):
            qh = qT_ref[LANES * pp + hd * hh:LANES * pp + hd * (hh + 1), :]
            w = jnp.concatenate([qh, zeros] if hh == 0 else [zeros, qh], axis=0)
            p, l = _softmax_cols(_dot(k_all, w) + e_ref[2 * pp + hh])
            outs.append(_dot(v_t[hd * hh:hd * (hh + 1)], p.astype(bf16)) * (1.0 / l))
        o_ref[pp] = _gated(jnp.concatenate(outs, axis=0), zs_ref.at[pp], slice(None))


def _a_sample_attention(f_rows, k_cache, k_new, v_cache, v_new, q_t, zs, n_new):
    b, n_cache, d = k_cache.shape
    pair = 2 * A_HEAD_DIM
    pps = PAIRS_PER_STEP
    n_keys = n_cache + SAMPLE_PAD
    new = pl.BlockSpec((None, SAMPLE_PAD, pps * pair), lambda p, bi: (bi, 0, p))
    old = pl.BlockSpec((None, n_cache, pps * pair), lambda p, bi: (bi, 0, p))
    slab = pl.BlockSpec((None, pps, SAMPLE_PAD, pair), lambda p, bi: (bi, p, 0, 0))
    return pl.pallas_call(
        functools.partial(_a_sattn_body, n_cache + n_new),
        grid=(d // pair // pps, b),
        in_specs=[pl.BlockSpec((2 * pps, 1, A_BIAS_SPAN), lambda p, bi: (p, 0, 0)),
                  old, slab, old, new,
                  pl.BlockSpec((None, pps * pair, SAMPLE_PAD), lambda p, bi: (bi, p, 0)),
                  slab],
        out_specs=slab,
        out_shape=jax.ShapeDtypeStruct((b, d // pair, SAMPLE_PAD, pair), bf16),
        scratch_shapes=[pltpu.VMEM((2 * pps, n_keys, SAMPLE_PAD), f32)],
        compiler_params=_params("arbitrary", "arbitrary"),
        name="a_sample_attention",
    )(f_rows, k_cache, k_new, v_cache, v_new, q_t, zs)


def _out_body(oz_ref, w_ref, x_ref, mod_ref, y_ref):
    y = _dot(_load_lane_tiles(oz_ref), w_ref[...])
    y_ref[...] = x_ref[...] + mod_ref[2:3, :] * y


def _out_project(oz, w_out, x, mod, tm):
    b, l, d = x.shape
    tile = pl.BlockSpec((None, tm, d), lambda bi, i: (bi, i, 0))
    return pl.pallas_call(
        _out_body,
        grid=(b, l // tm),
        in_specs=[_pair_tiles(tm), _weight(w_out.shape), tile,
                  pl.BlockSpec((None, 3, d), lambda bi, i: (bi, 0, 0))],
        out_specs=tile,
        out_shape=jax.ShapeDtypeStruct((b, l, d), f32),
        compiler_params=_params("arbitrary", "arbitrary"),
        name="out_project",
    )(oz, w_out, x, mod)


def _rotate(x, cos, sin):
    half = B_ROPE // 2
    x1, x2 = x[:half], x[half:]
    return jnp.concatenate([x1 * cos - x2 * sin, x2 * cos + x1 * sin], axis=0)


def _b_proj_body(oz_ref, wout_ref, x_ref, mod_prev_ref, mod_ref, g_ref,
                 wcq_ref, wckv_ref, wkr_ref, wz_ref, wuq_ref, wukv_ref,
                 gcq_ref, gckv_ref, gkr_ref, gqn_ref, gqr_ref, gkn_ref, cos_ref, sin_ref,
                 y_ref, qT_ref, ckv_ref, kr_ref, zs_ref, k_ref, vT_ref):
    y = x_ref[...] + mod_prev_ref[2:3, :] * _dot(_load_lane_tiles(oz_ref), wout_ref[...])
    y_ref[...] = y
    hb = _ada_hidden(y, mod_ref, g_ref)
    tm = hb.shape[0]
    cos = cos_ref[...]
    sin = sin_ref[...]
    cq_raw = _nt(wcq_ref[...], hb)
    z = _dot(hb, wz_ref[...])
    ckv_raw = _nt(wckv_ref[...], hb)
    kr_raw = _nt(wkr_ref[...], hb)
    cq = _rms_rows(cq_raw, gcq_ref[...], Q_LORA).astype(bf16)
    qT = _dot(wuq_ref[...], cq)
    ckv = _rms_rows(ckv_raw, gckv_ref[...], KV_LORA)
    kvT = _dot(wukv_ref[...], ckv.astype(bf16))
    _store_lane_tiles(zs_ref, _silu(z).astype(bf16))
    ckv_ref[...] = ckv.T
    kr = _rotate(_rms_rows(kr_raw, gkr_ref[...], B_ROPE), cos, sin)
    kr_pad = jnp.concatenate([kr, jnp.zeros((128 - B_ROPE, tm), f32)], axis=0)
    kr_ref[...] = kr_pad.T[:, :B_ROPE]
    _emit_keys_values(kvT, kr, gkn_ref[...], k_ref, vT_ref)
    gqn = gqn_ref[...]
    gqr = gqr_ref[...]
    for h in range(N_HEADS):
        base = B_QK * h
        nope = _rms_rows(qT[base:base + B_NOPE], gqn, B_NOPE)
        rope = _rotate(_rms_rows(qT[base + B_NOPE:base + B_QK], gqr, B_ROPE), cos, sin)
        qT_ref[base:base + B_QK, :] = jnp.concatenate([nope, rope], axis=0).astype(bf16)


def _b_project(oz_prev, w_out_prev, x, mod_prev, mod, g, w, cos_t, sin_t, tm):
    b, l, d = x.shape
    tile = pl.BlockSpec((None, tm, d), lambda bi, i: (bi, i, 0))
    mod_spec = pl.BlockSpec((None, 3, d), lambda bi, i: (bi, 0, 0))
    rot = pl.BlockSpec((B_ROPE // 2, tm), lambda bi, i: (0, i))
    nq = N_HEADS * B_QK
    return pl.pallas_call(
        _b_proj_body,
        grid=(b, l // tm),
        in_specs=[_pair_tiles(tm), _weight(w_out_prev.shape), tile, mod_spec, mod_spec, _const((1, d)),
                  _weight((Q_LORA, d)), _weight((KV_LORA, d)), _weight((B_ROPE, d)), _weight((d, d)),
                  _weight((nq, Q_LORA)), _weight(w["ukv_t"].shape),
                  _const((Q_LORA, 1)), _const((KV_LORA, 1)), _const((B_ROPE, 1)),
                  _const((B_NOPE, 1)), _const((B_ROPE, 1)), _const((B_NOPE, 1)), rot, rot],
        out_specs=[tile,
                   pl.BlockSpec((None, nq, tm), lambda bi, i: (bi, 0, i)),
                   pl.BlockSpec((None, tm, KV_LORA), lambda bi, i: (bi, i, 0)),
                   pl.BlockSpec((None, tm, B_ROPE), lambda bi, i: (bi, i, 0)),
                   _pair_tiles(tm),
                   pl.BlockSpec((None, N_HEADS, tm, LANES), lambda bi, i: (bi, 0, i, 0)),
                   pl.BlockSpec((None, N_HEADS * B_VDIM, tm), lambda bi, i: (bi, 0, i))],
        out_shape=[jax.ShapeDtypeStruct((b, l, d), f32),
                   jax.ShapeDtypeStruct((b, nq, l), bf16),
                   jax.ShapeDtypeStruct((b, l, KV_LORA), f32),
                   jax.ShapeDtypeStruct((b, l, B_ROPE), f32),
                   jax.ShapeDtypeStruct((b, N_PAIRS, l, LANES), bf16),
                   jax.ShapeDtypeStruct((b, N_HEADS, l, LANES), bf16),
                   jax.ShapeDtypeStruct((b, N_HEADS * B_VDIM, l), bf16)],
        compiler_params=_params("arbitrary", "arbitrary"),
        name="b_project",
    )(oz_prev, w_out_prev, x, mod_prev, mod, g, w["cq_t"], w["ckv_t"], w["kr_t"], w["z"], w["uq_t"], w["ukv_t"],
      w["g_cq"], w["g_ckv"], w["g_kr"], w["g_qn"], w["g_qr"], w["g_kn"], cos_t, sin_t)


def _b_kvup_body(ckv_ref, kr_ref, wukv_ref, gkn_ref, k_ref, vT_ref):
    kvT = _nt(wukv_ref[...], ckv_ref[...].astype(bf16))
    eye = (lax.broadcasted_iota(jnp.int32, (B_ROPE, B_ROPE), 0)
           == lax.broadcasted_iota(jnp.int32, (B_ROPE, B_ROPE), 1)).astype(bf16)
    krT = _nt(eye, kr_ref[...].astype(bf16))
    _emit_keys_values(kvT, krT, gkn_ref[...], k_ref, vT_ref)


def _emit_keys_values(kvT, krT, gkn, k_ref, vT_ref):
    tm = kvT.shape[1]
    key_chunk = (lax.broadcasted_iota(jnp.int32, (128 - B_QK, tm), 1) // CHUNK) % TILE_CHUNKS
    spare_row = lax.broadcasted_iota(jnp.int32, (128 - B_QK, tm), 0)
    chunk_rows = ((spare_row == key_chunk) | (spare_row == TILE_CHUNKS)).astype(f32)
    tail = jnp.concatenate([krT, chunk_rows], axis=0)
    per_head = B_NOPE + B_VDIM
    for h in range(N_HEADS):
        base = per_head * h
        kn = _rms_rows(kvT[base:base + B_NOPE], gkn, B_NOPE)
        k_ref[h] = jnp.concatenate([kn, tail], axis=0).T.astype(bf16)
        vT_ref[B_VDIM * h:B_VDIM * (h + 1), :] = kvT[base + B_NOPE:base + per_head].astype(bf16)


def _b_kv_up(ckv, kr, wukv_t, g_kn, tm):
    b, l, _ = ckv.shape
    return pl.pallas_call(
        _b_kvup_body,
        grid=(b, l // tm),
        in_specs=[pl.BlockSpec((None, tm, KV_LORA), lambda bi, i: (bi, i, 0)),
                  pl.BlockSpec((None, tm, B_ROPE), lambda bi, i: (bi, i, 0)),
                  _const(wukv_t.shape), _const((B_NOPE, 1))],
        out_specs=[pl.BlockSpec((None, N_HEADS, tm, 128), lambda bi, i: (bi, 0, i, 0)),
                   pl.BlockSpec((None, N_HEADS * B_VDIM, tm), lambda bi, i: (bi, 0, i))],
        out_shape=[jax.ShapeDtypeStruct((b, N_HEADS, l, 128), bf16),
                   jax.ShapeDtypeStruct((b, N_HEADS * B_VDIM, l), bf16)],
        compiler_params=_params("arbitrary", "arbitrary"),
        name="b_kv_up",
    )(ckv, kr, wukv_t, g_kn)


def _b_item_table(n_tiles, n_pairs=1):
    rows = [(pp, t, j, int(j == t)) for pp in range(n_pairs) for t in range(n_tiles) for j in range(t, -1, -1)]
    return np.asarray(rows, np.int32).T.copy()


def _b_items_per_step(n_items, odd_steps):
    return max(u for u in range(1, B_ITEMS_PER_STEP + 1)
               if n_items % u == 0 and n_items // u >= 3 and (not odd_steps or (n_items // u) % 2 == 1))


def _b_attn_body(seq, per, tab_ref, k_ref, qT_ref, vT_ref, zs_ref, o_ref,
                 s0, s1, c0, c1, pv0, pv1, a0, a1, m_st, acc_st):
    t_sz = Q_GROUP
    n_tiles = seq // t_sz
    n_steps = n_tiles * (n_tiles + 1) // 2 // per
    s_buf, c_buf, pv_buf, a_buf = (s0, s1), (c0, c1), (pv0, pv1), (a0, a1)
    ones = jnp.ones((ONES_ROWS, t_sz), bf16)
    key_chunk = lax.broadcasted_iota(jnp.int32, (128 - B_QK, t_sz), 0)
    qry_chunk = lax.broadcasted_iota(jnp.int32, (128 - B_QK, t_sz), 1) // CHUNK
    mask_rows = jnp.where((key_chunk < t_sz // CHUNK) & (key_chunk > qry_chunk), NEG_INF, 0.0).astype(bf16)
    zero_rows = jnp.zeros((128 - B_QK, t_sz), bf16)
    m_st[...] = jnp.full(m_st.shape, NEG_INF, f32)
    acc_st[...] = jnp.zeros(acc_st.shape, f32)

    def item(step, u):
        n = step * per + u
        return tab_ref[1, n], tab_ref[2, n], tab_ref[3, n] == 1

    def scores(step, par, u):
        t, j, diag = item(step, u)
        q0 = pl.multiple_of(t * t_sz, t_sz)
        k0 = pl.multiple_of(j * t_sz, t_sz)
        pad = jnp.where(diag, mask_rows, zero_rows)
        for hh in range(2):
            w = jnp.concatenate([qT_ref[B_QK * hh:B_QK * (hh + 1), pl.ds(q0, t_sz)], pad], axis=0)
            s = _dot(k_ref[hh, pl.ds(k0, t_sz), :], w)
            s_buf[par][u, hh] = s
            c_buf[par][u, hh] = jnp.max(s, axis=0, keepdims=True)

    def values(step, par, u):
        t, j, _ = item(step, u)
        k0 = pl.multiple_of(j * t_sz, t_sz)
        for hh in range(2):
            m_old = m_st[t, hh]
            m_new = jnp.maximum(m_old, c_buf[par][u, hh])
            m_st[t, hh] = m_new
            a_buf[par][u, hh] = jnp.exp2(m_old - m_new)
            p = jnp.exp2(s_buf[par][u, hh] - m_new).astype(bf16)
            v_ext = jnp.concatenate([vT_ref[B_VDIM * hh:B_VDIM * (hh + 1), pl.ds(k0, t_sz)], ones], axis=0)
            pv_buf[par][u, hh] = _dot(v_ext, p)

    def accumulate(step, par, u):
        t, _, _ = item(step, u)
        for hh in range(2):
            acc_st[t, hh] = a_buf[par][u, hh] * acc_st[t, hh] + pv_buf[par][u, hh]

    def time_step(tau, par, first=False, drain=0):
        for u in range(per):
            if drain == 0:
                scores(tau + 1, 1 - par, u)
            if drain <= 1:
                values(tau, par, u)
            if not first:
                accumulate(tau - 1, 1 - par, u)

    for u in range(per):
        scores(0, 0, u)
    time_step(0, 0, first=True)

    def body(i, carry):
        tau = 2 * i + 1
        time_step(tau, 1)
        time_step(tau + 1, 0)
        return carry

    lax.fori_loop(0, (n_steps - 3) // 2, body, 0)
    time_step(n_steps - 2, 1)
    time_step(n_steps - 1, 0, drain=1)
    time_step(n_steps, 1, drain=2)
    for t in range(n_tiles):
        outs = [_normalise(acc_st[t, hh], B_VDIM) for hh in range(2)]
        rows = slice(t * t_sz, (t + 1) * t_sz)
        o_ref[rows, :] = _gated(jnp.concatenate(outs, axis=0), zs_ref, rows)


def _b_attention(k_nat, q_t, v_t, zs):
    b, _, l, _ = k_nat.shape
    t_sz = Q_GROUP
    n_tiles = l // t_sz
    table = _b_item_table(n_tiles)
    per = _b_items_per_step(table.shape[1], odd_steps=True)
    ext = B_VDIM + ONES_ROWS
    scratch = ([pltpu.VMEM((per, 2, t_sz, t_sz), f32)] * 2 + [pltpu.VMEM((per, 2, 1, t_sz), f32)] * 2
               + [pltpu.VMEM((per, 2, ext, t_sz), f32)] * 2 + [pltpu.VMEM((per, 2, 1, t_sz), f32)] * 2
               + [pltpu.VMEM((n_tiles, 2, 1, t_sz), f32), pltpu.VMEM((n_tiles, 2, ext, t_sz), f32)])
    grid_spec = pltpu.PrefetchScalarGridSpec(
        num_scalar_prefetch=1,
        grid=(b, N_HEADS // 2),
        in_specs=[pl.BlockSpec((None, 2, l, 128), lambda bi, p, tab: (bi, p, 0, 0)),
                  pl.BlockSpec((None, 2 * B_QK, l), lambda bi, p, tab: (bi, p, 0)),
                  pl.BlockSpec((None, 2 * B_VDIM, l), lambda bi, p, tab: (bi, p, 0)),
                  pl.BlockSpec((None, None, l, LANES), lambda bi, p, tab: (bi, p, 0, 0))],
        out_specs=pl.BlockSpec((None, None, l, LANES), lambda bi, p, tab: (bi, p, 0, 0)),
        scratch_shapes=scratch)
    return pl.pallas_call(
        functools.partial(_b_attn_body, l, per),
        grid_spec=grid_spec,
        out_shape=jax.ShapeDtypeStruct((b, N_PAIRS, l, LANES), bf16),
        compiler_params=_params("arbitrary", "arbitrary"),
        name="b_attention",
    )(jnp.asarray(table), k_nat, q_t, v_t, zs)


def _b_attn_bounded_body(seq, per, tab_ref, shift_ref, k_ref, qT_ref, vT_ref, zs_ref, o_ref, p0, p1, acc_st):
    t_sz = Q_GROUP
    n_tiles = seq // t_sz
    n_pairs = acc_st.shape[0]
    n_steps = n_pairs * n_tiles * (n_tiles + 1) // 2 // per
    p_buf = (p0, p1)
    ones = jnp.ones((ONES_ROWS, t_sz), bf16)
    row = lax.broadcasted_iota(jnp.int32, (128 - B_QK, t_sz), 0)
    qry_chunk = lax.broadcasted_iota(jnp.int32, (128 - B_QK, t_sz), 1) // CHUNK
    shift_rows = jnp.where(row == TILE_CHUNKS, -shift_ref[0], 0.0)
    mask_rows = (jnp.where((row < TILE_CHUNKS) & (row > qry_chunk), NEG_INF, 0.0) + shift_rows).astype(bf16)
    plain_rows = shift_rows.astype(bf16)
    acc_st[...] = jnp.zeros(acc_st.shape, f32)

    def item(step, u):
        n = step * per + u
        return tab_ref[0, n], tab_ref[1, n], tab_ref[2, n], tab_ref[3, n] == 1

    def probs(step, par, u):
        pp, t, j, diag = item(step, u)
        q0 = pl.multiple_of(t * t_sz, t_sz)
        k0 = pl.multiple_of(j * t_sz, t_sz)
        pad = jnp.where(diag, mask_rows, plain_rows)
        for hh in range(2):
            r0 = pl.multiple_of((2 * pp + hh) * B_QK, B_ROPE)
            w = jnp.concatenate([qT_ref[pl.ds(r0, B_QK), pl.ds(q0, t_sz)], pad], axis=0)
            p_buf[par][u, hh] = jnp.exp2(_dot(k_ref[2 * pp + hh, pl.ds(k0, t_sz), :], w)).astype(bf16)

    def values(step, par, u):
        pp, t, j, _ = item(step, u)
        k0 = pl.multiple_of(j * t_sz, t_sz)
        for hh in range(2):
            r0 = pl.multiple_of((2 * pp + hh) * B_VDIM, B_VDIM)
            v_ext = jnp.concatenate([vT_ref[pl.ds(r0, B_VDIM), pl.ds(k0, t_sz)], ones], axis=0)
            acc_st[pp, t, hh] += _dot(v_ext, p_buf[par][u, hh])

    def time_step(tau, last):
        par = tau % 2
        for u in range(per):
            if not last:
                probs(tau + 1, 1 - par, u)
            values(tau, par, u)

    def step_pair(i, carry):
        for par in range(2):
            for u in range(per):
                probs(2 * i + par + 1, 1 - par, u)
                values(2 * i + par, par, u)
        return carry

    for u in range(per):
        probs(0, 0, u)
    n_loop = (n_steps - 1) // 2
    lax.fori_loop(0, n_loop, step_pair, 0)
    for tau in range(2 * n_loop, n_steps):
        time_step(tau, last=tau == n_steps - 1)
    for pp in range(n_pairs):
        for t in range(n_tiles):
            outs = [_normalise(acc_st[pp, t, hh], B_VDIM) for hh in range(2)]
            rows = slice(t * t_sz, (t + 1) * t_sz)
            o_ref[pp, rows, :] = _gated(jnp.concatenate(outs, axis=0), zs_ref.at[pp], rows)


def _b_attention_bounded(shift, k_nat, q_t, v_t, zs):
    b, _, l, _ = k_nat.shape
    t_sz = Q_GROUP
    n_tiles = l // t_sz
    pps = PAIRS_PER_STEP
    table = _b_item_table(n_tiles, pps)
    per = _b_items_per_step(table.shape[1], odd_steps=False)
    scratch = ([pltpu.VMEM((per, 2, t_sz, t_sz), bf16)] * 2
               + [pltpu.VMEM((pps, n_tiles, 2, B_VDIM + ONES_ROWS, t_sz), f32)])
    slab = pl.BlockSpec((None, pps, l, LANES), lambda bi, p, tab, sh: (bi, p, 0, 0))
    grid_spec = pltpu.PrefetchScalarGridSpec(
        num_scalar_prefetch=2,
        grid=(b, N_PAIRS // pps),
        in_specs=[pl.BlockSpec((None, 2 * pps, l, 128), lambda bi, p, tab, sh: (bi, p, 0, 0)),
                  pl.BlockSpec((None, 2 * pps * B_QK, l), lambda bi, p, tab, sh: (bi, p, 0)),
                  pl.BlockSpec((None, 2 * pps * B_VDIM, l), lambda bi, p, tab, sh: (bi, p, 0)),
                  slab],
        out_specs=slab,
        scratch_shapes=scratch)
    return pl.pallas_call(
        functools.partial(_b_attn_bounded_body, l, per),
        grid_spec=grid_spec,
        out_shape=jax.ShapeDtypeStruct((b, N_PAIRS, l, LANES), bf16),
        compiler_params=_params("arbitrary", "arbitrary"),
        name="b_attention_bounded",
    )(jnp.asarray(table), shift, k_nat, q_t, v_t, zs)


def _b_sattn_body(n_new, kc_ref, kn_ref, vc_ref, vn_ref, qT_ref, zs_ref, o_ref):
    t = SAMPLE_PAD
    zpad = jnp.zeros((128 - B_QK, t), bf16)
    new_mask = jnp.where(lax.broadcasted_iota(jnp.int32, (t, t), 0) < n_new, 0.0, NEG_INF).astype(f32)
    for pp in range(o_ref.shape[0]):
        outs = []
        for h in (2 * pp, 2 * pp + 1):
            w = jnp.concatenate([qT_ref[B_QK * h:B_QK * (h + 1), :], zpad], axis=0)
            s_c = _dot(kc_ref[h], w)
            s_n = _dot(kn_ref[h], w) + new_mask
            m = jnp.maximum(jnp.max(s_c, axis=0, keepdims=True), jnp.max(s_n, axis=0, keepdims=True))
            p_c = jnp.exp2(s_c - m)
            p_n = jnp.exp2(s_n - m)
            l = jnp.sum(p_c, axis=0, keepdims=True) + jnp.sum(p_n, axis=0, keepdims=True)
            rows = slice(B_VDIM * h, B_VDIM * (h + 1))
            o_t = _dot(vc_ref[rows, :], p_c.astype(bf16)) + _dot(vn_ref[rows, :], p_n.astype(bf16))
            outs.append(o_t * (1.0 / l))
        o_ref[pp] = _gated(jnp.concatenate(outs, axis=0), zs_ref.at[pp], slice(None))


def _b_sample_attention(k_cache, k_new, v_cache, v_new, q_t, zs, n_new):
    b, _, past, _ = k_cache.shape
    t = SAMPLE_PAD
    pps = PAIRS_PER_STEP
    slab = pl.BlockSpec((None, pps, t, LANES), lambda bi, p: (bi, p, 0, 0))
    return pl.pallas_call(
        functools.partial(_b_sattn_body, n_new),
        grid=(b, N_PAIRS // pps),
        in_specs=[pl.BlockSpec((None, 2 * pps, past, 128), lambda bi, p: (bi, p, 0, 0)),
                  pl.BlockSpec((None, 2 * pps, t, 128), lambda bi, p: (bi, p, 0, 0)),
                  pl.BlockSpec((None, 2 * pps * B_VDIM, past), lambda bi, p: (bi, p, 0)),
                  pl.BlockSpec((None, 2 * pps * B_VDIM, t), lambda bi, p: (bi, p, 0)),
                  pl.BlockSpec((None, 2 * pps * B_QK, t), lambda bi, p: (bi, p, 0)),
                  slab],
        out_specs=slab,
        out_shape=jax.ShapeDtypeStruct((b, N_PAIRS, t, LANES), bf16),
        compiler_params=_params("arbitrary", "arbitrary"),
        name="b_sample_attention",
    )(k_cache, k_new, v_cache, v_new, q_t, zs)


def _bias_rows(table):
    left = A_BIAS_ROWS - 1 - WINDOW - REL_CLIP
    right = A_BIAS_SPAN - left - table.shape[-1]
    return (jnp.pad(table.astype(f32), ((0, 0), (left, right)), mode="edge") * LOG2E)[:, None, :]


def _rope_tables(pos):
    half = B_ROPE // 2
    inv = ROPE_THETA ** (-jnp.arange(half, dtype=f32) / half)
    ang = pos.astype(f32)[:, None] * inv[None, :]
    return jnp.cos(ang).T, jnp.sin(ang).T


def _col(g, scale=1.0):
    return (g.astype(f32) * scale)[:, None]


def kernel(x_prompt, x_sample, cache_a_k, cache_a_v, cache_mla_ckv, cache_mla_krope, c_prompt, c_sample, norm_g, ada_w, ada_b, a_w_in, a_g_q, a_g_k, a_rel_bias, a_w_out, b_w_in, b_g_cq, b_w_uq, b_g_ckv, b_w_ukv, b_g_qn, b_g_qr, b_g_kn, b_g_kr, b_w_out):
    bp, seq, d = x_prompt.shape
    bs, dec, _ = x_sample.shape
    past = cache_mla_ckv.shape[2]
    n_cache_a = cache_a_k.shape[2]
    assert d == N_HEADS * A_HEAD_DIM and seq % (2 * Q_GROUP) == 0 and seq >= A_KEYS
    assert past % CHUNK == 0 and dec <= CHUNK and dec <= SAMPLE_PAD and n_cache_a == WINDOW
    cache_rows = min(WINDOW, seq)
    tm = min(ROW_TILE, seq)

    mod = _modulation(jnp.concatenate([c_prompt, c_sample], axis=0), ada_w, ada_b)
    mod = mod.reshape(mod.shape[0], bp + bs, 3, d)
    xs_pad = jnp.pad(x_sample, ((0, 0), (0, SAMPLE_PAD - dec), (0, 0)))

    w_in = a_w_in[0]
    wq_t, wk_t, wv_t = (w_in[:, d * n:d * (n + 1)].T.astype(bf16) for n in range(3))
    wz = w_in[:, 3 * d:].astype(bf16)
    w_out_a = a_w_out[0].astype(bf16)
    gq = _col(a_g_q[0], A_HEAD_DIM ** -0.5 * LOG2E)
    gk = _col(a_g_k[0])
    g0 = norm_g[0][None, :]
    f_rows = _bias_rows(a_rel_bias[0])
    mod_p0, mod_s0 = mod[0, :bp], mod[0, bp:]

    q_t, k_nat, v_t, zs, k32, v32 = _a_project(x_prompt, mod_p0, g0, wq_t, wk_t, wv_t, wz, gq, gk, tm, cache_rows)
    qk_max = A_HEAD_DIM * jnp.max(jnp.abs(gq)) * jnp.max(jnp.abs(gk)) * BOUND_MARGIN
    upper = qk_max + jnp.max(f_rows)
    oz_p0 = lax.cond(upper - (jnp.min(f_rows) - qk_max) <= MAX_LOGIT_RANGE,
                     lambda: _a_attention_bounded(upper[None], f_rows, k_nat, q_t, v_t, zs),
                     lambda: _a_attention(f_rows, k_nat, q_t, v_t, zs))
    new_a_k_p = k32.reshape(1, bp, cache_rows, N_HEADS, A_HEAD_DIM)
    new_a_v_p = v32.reshape(1, bp, cache_rows, N_HEADS, A_HEAD_DIM)

    q_t, k_nat, v_t, zs, k32, v32 = _a_project(xs_pad, mod_s0, g0, wq_t, wk_t, wv_t, wz, gq, gk,
                                               SAMPLE_PAD, SAMPLE_PAD)
    oz_s0 = _a_sample_attention(f_rows, cache_a_k[0].reshape(bs, n_cache_a, d), k_nat,
                                cache_a_v[0].reshape(bs, n_cache_a, d), v32, q_t, zs, dec)
    new_a_k_s = k32[:, :dec].reshape(1, bs, dec, N_HEADS, A_HEAD_DIM)
    new_a_v_s = v32[:, :dec].reshape(1, bs, dec, N_HEADS, A_HEAD_DIM)

    w_in = b_w_in[0]
    scale = B_QK ** -0.5 * LOG2E
    wb = {
        "cq_t": w_in[:, :Q_LORA].T.astype(bf16),
        "ckv_t": w_in[:, Q_LORA:Q_LORA + KV_LORA].T.astype(bf16),
        "kr_t": w_in[:, Q_LORA + KV_LORA:Q_LORA + KV_LORA + B_ROPE].T.astype(bf16),
        "z": w_in[:, Q_LORA + KV_LORA + B_ROPE:].astype(bf16),
        "uq_t": b_w_uq[0].T.astype(bf16),
        "g_cq": _col(b_g_cq[0]), "g_ckv": _col(b_g_ckv[0]), "g_kr": _col(b_g_kr[0]),
        "g_qn": _col(b_g_qn[0], scale), "g_qr": _col(b_g_qr[0], scale),
        "ukv_t": b_w_ukv[0].T.astype(bf16), "g_kn": _col(b_g_kn[0]),
    }
    wukv_t, g_kn = wb["ukv_t"], wb["g_kn"]
    w_out = b_w_out[0].astype(bf16)
    g1 = norm_g[1][None, :]
    mod_p, mod_s = mod[1, :bp], mod[1, bp:]

    cos_t, sin_t = _rope_tables(jnp.arange(seq, dtype=jnp.int32))
    y_p, q_t, ckv_p, kr_p, zs, k_nat, v_t = _b_project(oz_p0, w_out_a, x_prompt, mod_p0, mod_p, g1, wb,
                                                       cos_t, sin_t, min(B_PROJ_TILE, seq))
    q_sq = B_NOPE * jnp.max(jnp.abs(wb["g_qn"])) ** 2 + B_ROPE * jnp.max(jnp.abs(wb["g_qr"])) ** 2
    k_sq = B_NOPE * jnp.max(jnp.abs(g_kn)) ** 2 + B_ROPE * jnp.max(jnp.abs(wb["g_kr"])) ** 2
    qk_max = jnp.sqrt(q_sq * k_sq) * BOUND_MARGIN
    oz = lax.cond(2.0 * qk_max <= MAX_LOGIT_RANGE,
                  lambda: _b_attention_bounded(qk_max[None], k_nat, q_t, v_t, zs),
                  lambda: _b_attention(k_nat, q_t, v_t, zs))
    y_p = _out_project(oz, w_out, y_p, mod_p, min(OUT_TILE, seq))

    cos_t, sin_t = _rope_tables(past + jnp.arange(SAMPLE_PAD, dtype=jnp.int32))
    ys_pad, q_t, ckv_s, kr_s, zs, k_new, v_new = _b_project(oz_s0, w_out_a, xs_pad, mod_s0, mod_s, g1, wb,
                                                            cos_t, sin_t, SAMPLE_PAD)
    k_old, v_old = _b_kv_up(cache_mla_ckv[0], cache_mla_krope[0], wukv_t, g_kn, tm)
    oz = _b_sample_attention(k_old, k_new, v_old, v_new, q_t, zs, dec)
    ys_pad = _out_project(oz, w_out, ys_pad, mod_s, SAMPLE_PAD)

    return (y_p, ys_pad[:, :dec], new_a_k_p, new_a_v_p, new_a_k_s, new_a_v_s,
            ckv_p[None], kr_p[None], ckv_s[None, :, :dec], kr_s[None, :, :dec])
```

```python
import functools

import numpy as np

import jax
import jax.numpy as jnp
from jax import lax
from jax.experimental import pallas as pl
from jax.experimental.pallas import tpu as pltpu

f32 = jnp.float32
bf16 = jnp.bfloat16

EPS = 1e-6
NEG_INF = -1e30
CHUNK = 64
BAND_CHUNKS = 8
WINDOW = BAND_CHUNKS * CHUNK
REL_CLIP = 128
ROPE_THETA = 10000.0
LOG2E = 1.4426950408889634

LANES = 128
N_HEADS = 16
N_PAIRS = N_HEADS // 2
A_HEAD_DIM = 64
B_NOPE = 64
B_ROPE = 32
B_VDIM = 64
B_QK = B_NOPE + B_ROPE
KV_LORA = 256
Q_LORA = 384

Q_GROUP = 256
TILE_CHUNKS = Q_GROUP // CHUNK
MAX_LOGIT_RANGE = 100.0
BOUND_MARGIN = 1.02
A_KEYS = WINDOW + Q_GROUP
A_BIAS_ROWS = A_KEYS + WINDOW
A_BIAS_SPAN = A_BIAS_ROWS + Q_GROUP
SAMPLE_PAD = 128
ROW_TILE = 512
OUT_TILE = 2048
B_PROJ_TILE = 512
B_ITEMS_PER_STEP = 6
PAIRS_PER_STEP = 4
ONES_ROWS = 8
VMEM_LIMIT = 56 * 1024 * 1024


def _silu(x):
    return x * (1.0 / (1.0 + jnp.exp(-x)))


def _nt(a, b):
    return lax.dot_general(a, b, (((1,), (1,)), ((), ())), preferred_element_type=f32)


def _dot(a, b):
    return jnp.dot(a, b, preferred_element_type=f32)


def _rms_rows(blk, gain_col, n):
    ms = jnp.sum(blk * blk, axis=0, keepdims=True) * (1.0 / n)
    return blk * lax.rsqrt(ms + EPS) * gain_col


def _ada_hidden(x, mod_ref, g_ref):
    ms = jnp.mean(x * x, axis=-1, keepdims=True)
    xn = x * lax.rsqrt(ms + EPS) * g_ref[...]
    return (xn * (1.0 + mod_ref[1:2, :]) + mod_ref[0:1, :]).astype(bf16)


def _softmax_cols(s):
    m = jnp.max(s, axis=0, keepdims=True)
    p = jnp.exp2(s - m)
    return p, jnp.sum(p, axis=0, keepdims=True)


def _store_lane_tiles(ref, val):
    for p in range(ref.shape[0]):
        ref[p] = val[:, LANES * p:LANES * (p + 1)]


def _load_lane_tiles(ref):
    return jnp.concatenate([ref[p] for p in range(ref.shape[0])], axis=1)


def _gated(o_t, zs_ref, rows):
    return (o_t.T * zs_ref[rows, :].astype(f32)).astype(bf16)


def _normalise(o_ext, rows):
    return o_ext[:rows] * (1.0 / o_ext[rows:rows + 1])


def _bias_tile(f_row, n_rows, n_cols):
    blocks = []
    for a in range(pl.cdiv(n_rows, Q_GROUP)):
        lo = A_BIAS_ROWS - Q_GROUP * (a + 1)
        x = jnp.broadcast_to(f_row[:, lo:lo + 2 * Q_GROUP], (Q_GROUP, 2 * Q_GROUP))
        y = pltpu.roll(x, Q_GROUP + 1, 1, stride=1, stride_axis=0)[:, :n_cols]
        r = lax.broadcasted_iota(jnp.int32, (Q_GROUP, n_cols), 0) + Q_GROUP * a
        i = lax.broadcasted_iota(jnp.int32, (Q_GROUP, n_cols), 1)
        back = BAND_CHUNKS + i // CHUNK - r // CHUNK
        blocks.append(jnp.where((back >= 0) & (back <= BAND_CHUNKS), y, NEG_INF))
    return jnp.concatenate(blocks, axis=0)[:n_rows]


def _params(*sem):
    return pltpu.CompilerParams(dimension_semantics=sem, vmem_limit_bytes=VMEM_LIMIT)


def _const(shape):
    nd = len(shape)
    return pl.BlockSpec(shape, lambda *_: (0,) * nd)


def _pair_tiles(tm):
    return pl.BlockSpec((None, N_PAIRS, tm, LANES), lambda bi, i: (bi, 0, i, 0))


def _weight(shape):
    nd = len(shape)
    return pl.BlockSpec(shape, lambda *_: (0,) * nd, pipeline_mode=pl.Buffered(1))


def _mod_body(c_ref, w_ref, b_ref, o_ref):
    o_ref[...] = jnp.dot(_silu(c_ref[...]), w_ref[...], preferred_element_type=f32,
                         precision=lax.Precision.HIGHEST) + b_ref[...]


def _modulation(c_all, ada_w, ada_b):
    depth, d, n3 = ada_w.shape
    rows = c_all.shape[0]
    tn = d
    return pl.pallas_call(
        _mod_body,
        grid=(depth, n3 // tn),
        in_specs=[pl.BlockSpec((rows, d), lambda l, j: (0, 0)),
                  pl.BlockSpec((None, d, tn), lambda l, j: (l, 0, j)),
                  pl.BlockSpec((None, 1, tn), lambda l, j: (l, 0, j))],
        out_specs=pl.BlockSpec((None, rows, tn), lambda l, j: (l, 0, j)),
        out_shape=jax.ShapeDtypeStruct((depth, rows, n3), f32),
        compiler_params=_params("arbitrary", "arbitrary"),
        name="modulation",
    )(c_all, ada_w, ada_b.reshape(depth, 1, n3))


def _a_proj_body(n_skip, x_ref, mod_ref, g_ref, wq_ref, wk_ref, wv_ref, wz_ref, gq_ref, gk_ref,
                 qT_ref, k_ref, vT_ref, zs_ref, k32_ref, v32_ref, kf_ref):
    hb = _ada_hidden(x_ref[...], mod_ref, g_ref)
    z = _dot(hb, wz_ref[...])
    kT = _nt(wk_ref[...], hb)
    _store_lane_tiles(zs_ref, _silu(z).astype(bf16))
    qT = _nt(wq_ref[...], hb)
    gq = gq_ref[...]
    gk = gk_ref[...]
    hd = A_HEAD_DIM
    pairs = [slice(2 * hd * p, 2 * hd * (p + 1)) for p in range(N_HEADS // 2)]
    for p, rows in enumerate(pairs):
        k_pair = [_rms_rows(kT[hd * h:hd * (h + 1)], gk, hd) for h in (2 * p, 2 * p + 1)]
        k_nat = jnp.concatenate(k_pair, axis=0).T
        k_ref[p] = k_nat.astype(bf16)
        kf_ref[:, rows] = k_nat
    vT = _nt(wv_ref[...], hb)
    for p, rows in enumerate(pairs):
        q_pair = [_rms_rows(qT[hd * h:hd * (h + 1)], gq, hd) for h in (2 * p, 2 * p + 1)]
        qT_ref[rows, :] = jnp.concatenate(q_pair, axis=0).astype(bf16)
    vT_ref[...] = vT.astype(bf16)

    @pl.when(pl.program_id(1) >= n_skip)
    def _():
        k32_ref[...] = kf_ref[...]
        v32_ref[...] = vT.T


def _a_project(x, mod, g, wq_t, wk_t, wv_t, wz, gq, gk, tm, cache_rows):
    b, l, d = x.shape
    n_skip = (l - cache_rows) // tm
    tile = pl.BlockSpec((None, tm, d), lambda bi, i: (bi, i, 0))
    tile_t = pl.BlockSpec((None, d, tm), lambda bi, i: (bi, 0, i))
    cache = pl.BlockSpec((None, tm, d), lambda bi, i: (bi, jnp.maximum(i - n_skip, 0), 0))
    return pl.pallas_call(
        functools.partial(_a_proj_body, n_skip),
        grid=(b, l // tm),
        in_specs=[tile, pl.BlockSpec((None, 3, d), lambda bi, i: (bi, 0, 0)), _const((1, d)),
                  _weight((d, d)), _weight((d, d)), _weight((d, d)), _weight((d, d)),
                  _const((A_HEAD_DIM, 1)), _const((A_HEAD_DIM, 1))],
        out_specs=[tile_t, _pair_tiles(tm), tile_t, _pair_tiles(tm), cache, cache],
        out_shape=[jax.ShapeDtypeStruct((b, d, l), bf16), jax.ShapeDtypeStruct((b, N_PAIRS, l, LANES), bf16),
                   jax.ShapeDtypeStruct((b, d, l), bf16), jax.ShapeDtypeStruct((b, N_PAIRS, l, LANES), bf16),
                   jax.ShapeDtypeStruct((b, cache_rows, d), f32),
                   jax.ShapeDtypeStruct((b, cache_rows, d), f32)],
        scratch_shapes=[pltpu.VMEM((tm, d), f32)],
        compiler_params=_params("arbitrary", "arbitrary"),
        name="a_project",
    )(x, mod, g, wq_t, wk_t, wv_t, wz, gq, gk)


def _a_attn_body(seq, f_ref, k_ref, qT_ref, vT_ref, zs_ref, o_ref, e_ref, s0, s1, m0, m1, o0, o1):
    hd = A_HEAD_DIM
    n_groups = seq // Q_GROUP
    s_buf, m_buf, o_buf = (s0, s1), (m0, m1), (o0, o1)

    @pl.when(pl.program_id(1) == 0)
    def _():
        for hh in range(2):
            e_ref[hh] = _bias_tile(f_ref[hh], A_BIAS_ROWS, Q_GROUP)

    zeros = jnp.zeros((hd, Q_GROUP), bf16)
    ones = jnp.ones((ONES_ROWS, A_KEYS), bf16)

    def offsets(g):
        q0 = pl.multiple_of(g * Q_GROUP, Q_GROUP)
        w0 = pl.multiple_of(jnp.maximum(g * Q_GROUP - WINDOW, 0), Q_GROUP)
        e0 = pl.multiple_of(WINDOW - (g * Q_GROUP - w0), Q_GROUP)
        return q0, w0, e0

    def scores(g, par):
        q0, w0, e0 = offsets(g)
        k_win = k_ref[pl.ds(w0, A_KEYS), :]
        for hh in range(2):
            qh = qT_ref[hd * hh:hd * (hh + 1), pl.ds(q0, Q_GROUP)]
            w = jnp.concatenate([qh, zeros] if hh == 0 else [zeros, qh], axis=0)
            s = _dot(k_win, w) + e_ref[hh, pl.ds(e0, A_KEYS), :]
            s_buf[par][hh] = s
            m_buf[par][hh] = jnp.max(s, axis=0, keepdims=True)

    def values(g, par):
        _, w0, _ = offsets(g)
        for hh in range(2):
            p = jnp.exp2(s_buf[par][hh] - m_buf[par][hh]).astype(bf16)
            v_ext = jnp.concatenate([vT_ref[hd * hh:hd * (hh + 1), pl.ds(w0, A_KEYS)], ones], axis=0)
            o_buf[par][hh] = _dot(v_ext, p)

    def finish(g, par):
        q0, _, _ = offsets(g)
        outs = [_normalise(o_buf[par][hh], hd) for hh in range(2)]
        rows = pl.ds(q0, Q_GROUP)
        o_ref[rows, :] = _gated(jnp.concatenate(outs, axis=0), zs_ref, rows)

    scores(0, 0)
    scores(1, 1)
    values(0, 0)

    def body(i, carry):
        g = 2 * i + 1
        scores(g + 1, 0)
        values(g, 1)
        finish(g - 1, 0)
        scores(g + 2, 1)
        values(g + 1, 0)
        finish(g, 1)
        return carry

    lax.fori_loop(0, (n_groups - 2) // 2, body, 0)
    values(n_groups - 1, 1)
    finish(n_groups - 2, 0)
    finish(n_groups - 1, 1)


def _a_attention(f_rows, k_nat, q_t, v_t, zs):
    b, n_pairs, l, pair = k_nat.shape
    slab = pl.BlockSpec((None, None, l, pair), lambda p, bi: (bi, p, 0, 0))
    scratch = ([pltpu.VMEM((2, A_BIAS_ROWS, Q_GROUP), f32)]
               + [pltpu.VMEM((2, A_KEYS, Q_GROUP), f32)] * 2
               + [pltpu.VMEM((2, 1, Q_GROUP), f32)] * 2
               + [pltpu.VMEM((2, A_HEAD_DIM + ONES_ROWS, Q_GROUP), f32)] * 2)
    return pl.pallas_call(
        functools.partial(_a_attn_body, l),
        grid=(n_pairs, b),
        in_specs=[pl.BlockSpec((2, 1, A_BIAS_SPAN), lambda p, bi: (p, 0, 0)),
                  slab,
                  pl.BlockSpec((None, pair, l), lambda p, bi: (bi, p, 0)),
                  pl.BlockSpec((None, pair, l), lambda p, bi: (bi, p, 0)),
                  slab],
        out_specs=slab,
        out_shape=jax.ShapeDtypeStruct((b, n_pairs, l, pair), bf16),
        scratch_shapes=scratch,
        compiler_params=_params("arbitrary", "arbitrary"),
        name="a_attention",
    )(f_rows, k_nat, q_t, v_t, zs)


def _a_attn_bounded_body(seq, shift_ref, f_ref, k_ref, qT_ref, vT_ref, zs_ref, o_ref, e_ref, p0, p1, o0, o1):
    hd = A_HEAD_DIM
    n_groups = seq // Q_GROUP
    n_pairs = k_ref.shape[0]
    n_items = n_pairs * n_groups
    p_buf, o_buf = (p0, p1), (o0, o1)

    @pl.when(pl.program_id(1) == 0)
    def _():
        for h in range(2 * n_pairs):
            e_ref[h] = _bias_tile(f_ref[h], A_BIAS_ROWS, Q_GROUP) - shift_ref[0]

    zeros = jnp.zeros((hd, Q_GROUP), bf16)
    ones = jnp.ones((ONES_ROWS, A_KEYS), bf16)

    def offsets(n):
        pp = n // n_groups
        g = n - pp * n_groups
        q0 = pl.multiple_of(g * Q_GROUP, Q_GROUP)
        w0 = pl.multiple_of(jnp.maximum(g * Q_GROUP - WINDOW, 0), Q_GROUP)
        e0 = pl.multiple_of(WINDOW - (g * Q_GROUP - w0), Q_GROUP)
        return pp, q0, w0, e0

    def probs(n, par):
        pp, q0, w0, e0 = offsets(n)
        k_win = k_ref[pp, pl.ds(w0, A_KEYS), :]
        for hh in range(2):
            r0 = pl.multiple_of((2 * pp + hh) * hd, hd)
            qh = qT_ref[pl.ds(r0, hd), pl.ds(q0, Q_GROUP)]
            w = jnp.concatenate([qh, zeros] if hh == 0 else [zeros, qh], axis=0)
            bias = e_ref[2 * pp + hh, pl.ds(e0, A_KEYS), :]
            p_buf[par][hh] = jnp.exp2(_dot(k_win, w) + bias).astype(bf16)

    def values(n, par):
        pp, _, w0, _ = offsets(n)
        for hh in range(2):
            r0 = pl.multiple_of((2 * pp + hh) * hd, hd)
            v_ext = jnp.concatenate([vT_ref[pl.ds(r0, hd), pl.ds(w0, A_KEYS)], ones], axis=0)
            o_buf[par][hh] = _dot(v_ext, p_buf[par][hh])

    def finish(n, par):
        pp, q0, _, _ = offsets(n)
        outs = [_normalise(o_buf[par][hh], hd) for hh in range(2)]
        rows = pl.ds(q0, Q_GROUP)
        o_ref[pp, rows, :] = _gated(jnp.concatenate(outs, axis=0), zs_ref.at[pp], rows)

    probs(0, 0)
    probs(1, 1)
    values(0, 0)

    def body(i, carry):
        n = 2 * i + 1
        probs(n + 1, 0)
        values(n, 1)
        finish(n - 1, 0)
        probs(n + 2, 1)
        values(n + 1, 0)
        finish(n, 1)
        return carry

    lax.fori_loop(0, (n_items - 2) // 2, body, 0)
    values(n_items - 1, 1)
    finish(n_items - 2, 0)
    finish(n_items - 1, 1)


def _a_attention_bounded(shift, f_rows, k_nat, q_t, v_t, zs):
    b, n_pairs, l, pair = k_nat.shape
    pps = PAIRS_PER_STEP
    slab = pl.BlockSpec((None, pps, l, pair), lambda p, bi, sh: (bi, p, 0, 0))
    scratch = ([pltpu.VMEM((2 * pps, A_BIAS_ROWS, Q_GROUP), f32)]
               + [pltpu.VMEM((2, A_KEYS, Q_GROUP), bf16)] * 2
               + [pltpu.VMEM((2, A_HEAD_DIM + ONES_ROWS, Q_GROUP), f32)] * 2)
    grid_spec = pltpu.PrefetchScalarGridSpec(
        num_scalar_prefetch=1,
        grid=(n_pairs // pps, b),
        in_specs=[pl.BlockSpec((2 * pps, 1, A_BIAS_SPAN), lambda p, bi, sh: (p, 0, 0)),
                  slab,
                  pl.BlockSpec((None, pps * pair, l), lambda p, bi, sh: (bi, p, 0)),
                  pl.BlockSpec((None, pps * pair, l), lambda p, bi, sh: (bi, p, 0)),
                  slab],
        out_specs=slab,
        scratch_shapes=scratch)
    return pl.pallas_call(
        functools.partial(_a_attn_bounded_body, l),
        grid_spec=grid_spec,
        out_shape=jax.ShapeDtypeStruct((b, n_pairs, l, pair), bf16),
        compiler_params=_params("arbitrary", "arbitrary"),
        name="a_attention_bounded",
    )(shift, f_rows, k_nat, q_t, v_t, zs)


def _a_sattn_body(n_valid, f_ref, kc_ref, kn_ref, vc_ref, vn_ref, qT_ref, zs_ref, o_ref, e_ref):
    hd = A_HEAD_DIM
    n_keys = e_ref.shape[1]
    n_pairs = o_ref.shape[0]

    @pl.when(pl.program_id(1) == 0)
    def _():
        live = lax.broadcasted_iota(jnp.int32, (n_keys, SAMPLE_PAD), 0) < n_valid
        for h in range(2 * n_pairs):
            e_ref[h] = jnp.where(live, _bias_tile(f_ref[h], n_keys, SAMPLE_PAD), NEG_INF)

    zeros = jnp.zeros((hd, SAMPLE_PAD), bf16)
    for pp in range(n_pairs):
        lanes = slice(LANES * pp, LANES * (pp + 1))
        k_all = jnp.concatenate([kc_ref[:, lanes].astype(bf16), kn_ref[pp]], axis=0)
        v_t = jnp.concatenate([vc_ref[:, lanes], vn_ref[:, lanes]], axis=0).T.astype(bf16)
        outs = []
        for hh in range(2):
            qh = qT_ref[LANES * pp + hd * hh:LANES * pp + hd * (hh + 1), :]
            w = jnp.concatenate([qh, zeros] if hh == 0 else [zeros, qh], axis=0)
            p, l = _softmax_cols(_dot(k_all, w) + e_ref[2 * pp + hh])
            outs.append(_dot(v_t[hd * hh:hd * (hh + 1)], p.astype(bf16)) * (1.0 / l))
        o_ref[pp] = _gated(jnp.concatenate(outs, axis=0), zs_ref.at[pp], slice(None))


def _a_sample_attention(f_rows, k_cache, k_new, v_cache, v_new, q_t, zs, n_new):
    b, n_cache, d = k_cache.shape
    pair = 2 * A_HEAD_DIM
    pps = PAIRS_PER_STEP
    n_keys = n_cache + SAMPLE_PAD
    new = pl.BlockSpec((None, SAMPLE_PAD, pps * pair), lambda p, bi: (bi, 0, p))
    old = pl.BlockSpec((None, n_cache, pps * pair), lambda p, bi: (bi, 0, p))
    slab = pl.BlockSpec((None, pps, SAMPLE_PAD, pair), lambda p, bi: (bi, p, 0, 0))
    return pl.pallas_call(
        functools.partial(_a_sattn_body, n_cache + n_new),
        grid=(d // pair // pps, b),
        in_specs=[pl.BlockSpec((2 * pps, 1, A_BIAS_SPAN), lambda p, bi: (p, 0, 0)),
                  old, slab, old, new,
                  pl.BlockSpec((None, pps * pair, SAMPLE_PAD), lambda p, bi: (bi, p, 0)),
                  slab],
        out_specs=slab,
        out_shape=jax.ShapeDtypeStruct((b, d // pair, SAMPLE_PAD, pair), bf16),
        scratch_shapes=[pltpu.VMEM((2 * pps, n_keys, SAMPLE_PAD), f32)],
        compiler_params=_params("arbitrary", "arbitrary"),
        name="a_sample_attention",
    )(f_rows, k_cache, k_new, v_cache, v_new, q_t, zs)


def _out_body(oz_ref, w_ref, x_ref, mod_ref, y_ref):
    y = _dot(_load_lane_tiles(oz_ref), w_ref[...])
    y_ref[...] = x_ref[...] + mod_ref[2:3, :] * y


def _out_project(oz, w_out, x, mod, tm):
    b, l, d = x.shape
    tile = pl.BlockSpec((None, tm, d), lambda bi, i: (bi, i, 0))
    return pl.pallas_call(
        _out_body,
        grid=(b, l // tm),
        in_specs=[_pair_tiles(tm), _weight(w_out.shape), tile,
                  pl.BlockSpec((None, 3, d), lambda bi, i: (bi, 0, 0))],
        out_specs=tile,
        out_shape=jax.ShapeDtypeStruct((b, l, d), f32),
        compiler_params=_params("arbitrary", "arbitrary"),
        name="out_project",
    )(oz, w_out, x, mod)


def _rotate(x, cos, sin):
    half = B_ROPE // 2
    x1, x2 = x[:half], x[half:]
    return jnp.concatenate([x1 * cos - x2 * sin, x2 * cos + x1 * sin], axis=0)


def _b_proj_body(oz_ref, wout_ref, x_ref, mod_prev_ref, mod_ref, g_ref,
                 wcq_ref, wckv_ref, wkr_ref, wz_ref, wuq_ref, wukv_ref,
                 gcq_ref, gckv_ref, gkr_ref, gqn_ref, gqr_ref, gkn_ref, cos_ref, sin_ref,
                 y_ref, qT_ref, ckv_ref, kr_ref, zs_ref, k_ref, vT_ref):
    y = x_ref[...] + mod_prev_ref[2:3, :] * _dot(_load_lane_tiles(oz_ref), wout_ref[...])
    y_ref[...] = y
    hb = _ada_hidden(y, mod_ref, g_ref)
    tm = hb.shape[0]
    cos = cos_ref[...]
    sin = sin_ref[...]
    cq_raw = _nt(wcq_ref[...], hb)
    z = _dot(hb, wz_ref[...])
    ckv_raw = _nt(wckv_ref[...], hb)
    kr_raw = _nt(wkr_ref[...], hb)
    cq = _rms_rows(cq_raw, gcq_ref[...], Q_LORA).astype(bf16)
    qT = _dot(wuq_ref[...], cq)
    ckv = _rms_rows(ckv_raw, gckv_ref[...], KV_LORA)
    kvT = _dot(wukv_ref[...], ckv.astype(bf16))
    _store_lane_tiles(zs_ref, _silu(z).astype(bf16))
    ckv_ref[...] = ckv.T
    kr = _rotate(_rms_rows(kr_raw, gkr_ref[...], B_ROPE), cos, sin)
    kr_pad = jnp.concatenate([kr, jnp.zeros((128 - B_ROPE, tm), f32)], axis=0)
    kr_ref[...] = kr_pad.T[:, :B_ROPE]
    _emit_keys_values(kvT, kr, gkn_ref[...], k_ref, vT_ref)
    gqn = gqn_ref[...]
    gqr = gqr_ref[...]
    for h in range(N_HEADS):
        base = B_QK * h
        nope = _rms_rows(qT[base:base + B_NOPE], gqn, B_NOPE)
        rope = _rotate(_rms_rows(qT[base + B_NOPE:base + B_QK], gqr, B_ROPE), cos, sin)
        qT_ref[base:base + B_QK, :] = jnp.concatenate([nope, rope], axis=0).astype(bf16)


def _b_project(oz_prev, w_out_prev, x, mod_prev, mod, g, w, cos_t, sin_t, tm):
    b, l, d = x.shape
    tile = pl.BlockSpec((None, tm, d), lambda bi, i: (bi, i, 0))
    mod_spec = pl.BlockSpec((None, 3, d), lambda bi, i: (bi, 0, 0))
    rot = pl.BlockSpec((B_ROPE // 2, tm), lambda bi, i: (0, i))
    nq = N_HEADS * B_QK
    return pl.pallas_call(
        _b_proj_body,
        grid=(b, l // tm),
        in_specs=[_pair_tiles(tm), _weight(w_out_prev.shape), tile, mod_spec, mod_spec, _const((1, d)),
                  _weight((Q_LORA, d)), _weight((KV_LORA, d)), _weight((B_ROPE, d)), _weight((d, d)),
                  _weight((nq, Q_LORA)), _weight(w["ukv_t"].shape),
                  _const((Q_LORA, 1)), _const((KV_LORA, 1)), _const((B_ROPE, 1)),
                  _const((B_NOPE, 1)), _const((B_ROPE, 1)), _const((B_NOPE, 1)), rot, rot],
        out_specs=[tile,
                   pl.BlockSpec((None, nq, tm), lambda bi, i: (bi, 0, i)),
                   pl.BlockSpec((None, tm, KV_LORA), lambda bi, i: (bi, i, 0)),
                   pl.BlockSpec((None, tm, B_ROPE), lambda bi, i: (bi, i, 0)),
                   _pair_tiles(tm),
                   pl.BlockSpec((None, N_HEADS, tm, LANES), lambda bi, i: (bi, 0, i, 0)),
                   pl.BlockSpec((None, N_HEADS * B_VDIM, tm), lambda bi, i: (bi, 0, i))],
        out_shape=[jax.ShapeDtypeStruct((b, l, d), f32),
                   jax.ShapeDtypeStruct((b, nq, l), bf16),
                   jax.ShapeDtypeStruct((b, l, KV_LORA), f32),
                   jax.ShapeDtypeStruct((b, l, B_ROPE), f32),
                   jax.ShapeDtypeStruct((b, N_PAIRS, l, LANES), bf16),
                   jax.ShapeDtypeStruct((b, N_HEADS, l, LANES), bf16),
                   jax.ShapeDtypeStruct((b, N_HEADS * B_VDIM, l), bf16)],
        compiler_params=_params("arbitrary", "arbitrary"),
        name="b_project",
    )(oz_prev, w_out_prev, x, mod_prev, mod, g, w["cq_t"], w["ckv_t"], w["kr_t"], w["z"], w["uq_t"], w["ukv_t"],
      w["g_cq"], w["g_ckv"], w["g_kr"], w["g_qn"], w["g_qr"], w["g_kn"], cos_t, sin_t)


def _b_kvup_body(ckv_ref, kr_ref, wukv_ref, gkn_ref, k_ref, vT_ref):
    kvT = _nt(wukv_ref[...], ckv_ref[...].astype(bf16))
    eye = (lax.broadcasted_iota(jnp.int32, (B_ROPE, B_ROPE), 0)
           == lax.broadcasted_iota(jnp.int32, (B_ROPE, B_ROPE), 1)).astype(bf16)
    krT = _nt(eye, kr_ref[...].astype(bf16))
    _emit_keys_values(kvT, krT, gkn_ref[...], k_ref, vT_ref)


def _emit_keys_values(kvT, krT, gkn, k_ref, vT_ref):
    tm = kvT.shape[1]
    key_chunk = (lax.broadcasted_iota(jnp.int32, (128 - B_QK, tm), 1) // CHUNK) % TILE_CHUNKS
    spare_row = lax.broadcasted_iota(jnp.int32, (128 - B_QK, tm), 0)
    chunk_rows = ((spare_row == key_chunk) | (spare_row == TILE_CHUNKS)).astype(f32)
    tail = jnp.concatenate([krT, chunk_rows], axis=0)
    per_head = B_NOPE + B_VDIM
    for h in range(N_HEADS):
        base = per_head * h
        kn = _rms_rows(kvT[base:base + B_NOPE], gkn, B_NOPE)
        k_ref[h] = jnp.concatenate([kn, tail], axis=0).T.astype(bf16)
        vT_ref[B_VDIM * h:B_VDIM * (h + 1), :] = kvT[base + B_NOPE:base + per_head].astype(bf16)


def _b_kv_up(ckv, kr, wukv_t, g_kn, tm):
    b, l, _ = ckv.shape
    return pl.pallas_call(
        _b_kvup_body,
        grid=(b, l // tm),
        in_specs=[pl.BlockSpec((None, tm, KV_LORA), lambda bi, i: (bi, i, 0)),
                  pl.BlockSpec((None, tm, B_ROPE), lambda bi, i: (bi, i, 0)),
                  _const(wukv_t.shape), _const((B_NOPE, 1))],
        out_specs=[pl.BlockSpec((None, N_HEADS, tm, 128), lambda bi, i: (bi, 0, i, 0)),
                   pl.BlockSpec((None, N_HEADS * B_VDIM, tm), lambda bi, i: (bi, 0, i))],
        out_shape=[jax.ShapeDtypeStruct((b, N_HEADS, l, 128), bf16),
                   jax.ShapeDtypeStruct((b, N_HEADS * B_VDIM, l), bf16)],
        compiler_params=_params("arbitrary", "arbitrary"),
        name="b_kv_up",
    )(ckv, kr, wukv_t, g_kn)


def _b_item_table(n_tiles, n_pairs=1):
    rows = [(pp, t, j, int(j == t)) for pp in range(n_pairs) for t in range(n_tiles) for j in range(t, -1, -1)]
    return np.asarray(rows, np.int32).T.copy()


def _b_items_per_step(n_items, odd_steps):
    return max(u for u in range(1, B_ITEMS_PER_STEP + 1)
               if n_items % u == 0 and n_items // u >= 3 and (not odd_steps or (n_items // u) % 2 == 1))


def _b_attn_body(seq, per, tab_ref, k_ref, qT_ref, vT_ref, zs_ref, o_ref,
                 s0, s1, c0, c1, pv0, pv1, a0, a1, m_st, acc_st):
    t_sz = Q_GROUP
    n_tiles = seq // t_sz
    n_steps = n_tiles * (n_tiles + 1) // 2 // per
    s_buf, c_buf, pv_buf, a_buf = (s0, s1), (c0, c1), (pv0, pv1), (a0, a1)
    ones = jnp.ones((ONES_ROWS, t_sz), bf16)
    key_chunk = lax.broadcasted_iota(jnp.int32, (128 - B_QK, t_sz), 0)
    qry_chunk = lax.broadcasted_iota(jnp.int32, (128 - B_QK, t_sz), 1) // CHUNK
    mask_rows = jnp.where((key_chunk < t_sz // CHUNK) & (key_chunk > qry_chunk), NEG_INF, 0.0).astype(bf16)
    zero_rows = jnp.zeros((128 - B_QK, t_sz), bf16)
    m_st[...] = jnp.full(m_st.shape, NEG_INF, f32)
    acc_st[...] = jnp.zeros(acc_st.shape, f32)

    def item(step, u):
        n = step * per + u
        return tab_ref[1, n], tab_ref[2, n], tab_ref[3, n] == 1

    def scores(step, par, u):
        t, j, diag = item(step, u)
        q0 = pl.multiple_of(t * t_sz, t_sz)
        k0 = pl.multiple_of(j * t_sz, t_sz)
        pad = jnp.where(diag, mask_rows, zero_rows)
        for hh in range(2):
            w = jnp.concatenate([qT_ref[B_QK * hh:B_QK * (hh + 1), pl.ds(q0, t_sz)], pad], axis=0)
            s = _dot(k_ref[hh, pl.ds(k0, t_sz), :], w)
            s_buf[par][u, hh] = s
            c_buf[par][u, hh] = jnp.max(s, axis=0, keepdims=True)

    def values(step, par, u):
        t, j, _ = item(step, u)
        k0 = pl.multiple_of(j * t_sz, t_sz)
        for hh in range(2):
            m_old = m_st[t, hh]
            m_new = jnp.maximum(m_old, c_buf[par][u, hh])
            m_st[t, hh] = m_new
            a_buf[par][u, hh] = jnp.exp2(m_old - m_new)
            p = jnp.exp2(s_buf[par][u, hh] - m_new).astype(bf16)
            v_ext = jnp.concatenate([vT_ref[B_VDIM * hh:B_VDIM * (hh + 1), pl.ds(k0, t_sz)], ones], axis=0)
            pv_buf[par][u, hh] = _dot(v_ext, p)

    def accumulate(step, par, u):
        t, _, _ = item(step, u)
        for hh in range(2):
            acc_st[t, hh] = a_buf[par][u, hh] * acc_st[t, hh] + pv_buf[par][u, hh]

    def time_step(tau, par, first=False, drain=0):
        for u in range(per):
            if drain == 0:
                scores(tau + 1, 1 - par, u)
            if drain <= 1:
                values(tau, par, u)
            if not first:
                accumulate(tau - 1, 1 - par, u)

    for u in range(per):
        scores(0, 0, u)
    time_step(0, 0, first=True)

    def body(i, carry):
        tau = 2 * i + 1
        time_step(tau, 1)
        time_step(tau + 1, 0)
        return carry

    lax.fori_loop(0, (n_steps - 3) // 2, body, 0)
    time_step(n_steps - 2, 1)
    time_step(n_steps - 1, 0, drain=1)
    time_step(n_steps, 1, drain=2)
    for t in range(n_tiles):
        outs = [_normalise(acc_st[t, hh], B_VDIM) for hh in range(2)]
        rows = slice(t * t_sz, (t + 1) * t_sz)
        o_ref[rows, :] = _gated(jnp.concatenate(outs, axis=0), zs_ref, rows)


def _b_attention(k_nat, q_t, v_t, zs):
    b, _, l, _ = k_nat.shape
    t_sz = Q_GROUP
    n_tiles = l // t_sz
    table = _b_item_table(n_tiles)
    per = _b_items_per_step(table.shape[1], odd_steps=True)
    ext = B_VDIM + ONES_ROWS
    scratch = ([pltpu.VMEM((per, 2, t_sz, t_sz), f32)] * 2 + [pltpu.VMEM((per, 2, 1, t_sz), f32)] * 2
               + [pltpu.VMEM((per, 2, ext, t_sz), f32)] * 2 + [pltpu.VMEM((per, 2, 1, t_sz), f32)] * 2
               + [pltpu.VMEM((n_tiles, 2, 1, t_sz), f32), pltpu.VMEM((n_tiles, 2, ext, t_sz), f32)])
    grid_spec = pltpu.PrefetchScalarGridSpec(
        num_scalar_prefetch=1,
        grid=(b, N_HEADS // 2),
        in_specs=[pl.BlockSpec((None, 2, l, 128), lambda bi, p, tab: (bi, p, 0, 0)),
                  pl.BlockSpec((None, 2 * B_QK, l), lambda bi, p, tab: (bi, p, 0)),
                  pl.BlockSpec((None, 2 * B_VDIM, l), lambda bi, p, tab: (bi, p, 0)),
                  pl.BlockSpec((None, None, l, LANES), lambda bi, p, tab: (bi, p, 0, 0))],
        out_specs=pl.BlockSpec((None, None, l, LANES), lambda bi, p, tab: (bi, p, 0, 0)),
        scratch_shapes=scratch)
    return pl.pallas_call(
        functools.partial(_b_attn_body, l, per),
        grid_spec=grid_spec,
        out_shape=jax.ShapeDtypeStruct((b, N_PAIRS, l, LANES), bf16),
        compiler_params=_params("arbitrary", "arbitrary"),
        name="b_attention",
    )(jnp.asarray(table), k_nat, q_t, v_t, zs)


def _b_attn_bounded_body(seq, per, tab_ref, shift_ref, k_ref, qT_ref, vT_ref, zs_ref, o_ref, p0, p1, acc_st):
    t_sz = Q_GROUP
    n_tiles = seq // t_sz
    n_pairs = acc_st.shape[0]
    n_steps = n_pairs * n_tiles * (n_tiles + 1) // 2 // per
    p_buf = (p0, p1)
    ones = jnp.ones((ONES_ROWS, t_sz), bf16)
    row = lax.broadcasted_iota(jnp.int32, (128 - B_QK, t_sz), 0)
    qry_chunk = lax.broadcasted_iota(jnp.int32, (128 - B_QK, t_sz), 1) // CHUNK
    shift_rows = jnp.where(row == TILE_CHUNKS, -shift_ref[0], 0.0)
    mask_rows = (jnp.where((row < TILE_CHUNKS) & (row > qry_chunk), NEG_INF, 0.0) + shift_rows).astype(bf16)
    plain_rows = shift_rows.astype(bf16)
    acc_st[...] = jnp.zeros(acc_st.shape, f32)

    def item(step, u):
        n = step * per + u
        return tab_ref[0, n], tab_ref[1, n], tab_ref[2, n], tab_ref[3, n] == 1

    def probs(step, par, u):
        pp, t, j, diag = item(step, u)
        q0 = pl.multiple_of(t * t_sz, t_sz)
        k0 = pl.multiple_of(j * t_sz, t_sz)
        pad = jnp.where(diag, mask_rows, plain_rows)
        for hh in range(2):
            r0 = pl.multiple_of((2 * pp + hh) * B_QK, B_ROPE)
            w = jnp.concatenate([qT_ref[pl.ds(r0, B_QK), pl.ds(q0, t_sz)], pad], axis=0)
            p_buf[par][u, hh] = jnp.exp2(_dot(k_ref[2 * pp + hh, pl.ds(k0, t_sz), :], w)).astype(bf16)

    def values(step, par, u):
        pp, t, j, _ = item(step, u)
        k0 = pl.multiple_of(j * t_sz, t_sz)
        for hh in range(2):
            r0 = pl.multiple_of((2 * pp + hh) * B_VDIM, B_VDIM)
            v_ext = jnp.concatenate([vT_ref[pl.ds(r0, B_VDIM), pl.ds(k0, t_sz)], ones], axis=0)
            acc_st[pp, t, hh] += _dot(v_ext, p_buf[par][u, hh])

    def time_step(tau, last):
        par = tau % 2
        for u in range(per):
            if not last:
                probs(tau + 1, 1 - par, u)
            values(tau, par, u)

    def step_pair(i, carry):
        for par in range(2):
            for u in range(per):
                probs(2 * i + par + 1, 1 - par, u)
                values(2 * i + par, par, u)
        return carry

    for u in range(per):
        probs(0, 0, u)
    n_loop = (n_steps - 1) // 2
    lax.fori_loop(0, n_loop, step_pair, 0)
    for tau in range(2 * n_loop, n_steps):
        time_step(tau, last=tau == n_steps - 1)
    for pp in range(n_pairs):
        for t in range(n_tiles):
            outs = [_normalise(acc_st[pp, t, hh], B_VDIM) for hh in range(2)]
            rows = slice(t * t_sz, (t + 1) * t_sz)
            o_ref[pp, rows, :] = _gated(jnp.concatenate(outs, axis=0), zs_ref.at[pp], rows)


def _b_attention_bounded(shift, k_nat, q_t, v_t, zs):
    b, _, l, _ = k_nat.shape
    t_sz = Q_GROUP
    n_tiles = l // t_sz
    pps = PAIRS_PER_STEP
    table = _b_item_table(n_tiles, pps)
    per = _b_items_per_step(table.shape[1], odd_steps=False)
    scratch = ([pltpu.VMEM((per, 2, t_sz, t_sz), bf16)] * 2
               + [pltpu.VMEM((pps, n_tiles, 2, B_VDIM + ONES_ROWS, t_sz), f32)])
    slab = pl.BlockSpec((None, pps, l, LANES), lambda bi, p, tab, sh: (bi, p, 0, 0))
    grid_spec = pltpu.PrefetchScalarGridSpec(
        num_scalar_prefetch=2,
        grid=(b, N_PAIRS // pps),
        in_specs=[pl.BlockSpec((None, 2 * pps, l, 128), lambda bi, p, tab, sh: (bi, p, 0, 0)),
                  pl.BlockSpec((None, 2 * pps * B_QK, l), lambda bi, p, tab, sh: (bi, p, 0)),
                  pl.BlockSpec((None, 2 * pps * B_VDIM, l), lambda bi, p, tab, sh: (bi, p, 0)),
                  slab],
        out_specs=slab,
        scratch_shapes=scratch)
    return pl.pallas_call(
        functools.partial(_b_attn_bounded_body, l, per),
        grid_spec=grid_spec,
        out_shape=jax.ShapeDtypeStruct((b, N_PAIRS, l, LANES), bf16),
        compiler_params=_params("arbitrary", "arbitrary"),
        name="b_attention_bounded",
    )(jnp.asarray(table), shift, k_nat, q_t, v_t, zs)


def _b_sattn_body(n_new, kc_ref, kn_ref, vc_ref, vn_ref, qT_ref, zs_ref, o_ref):
    t = SAMPLE_PAD
    zpad = jnp.zeros((128 - B_QK, t), bf16)
    new_mask = jnp.where(lax.broadcasted_iota(jnp.int32, (t, t), 0) < n_new, 0.0, NEG_INF).astype(f32)
    for pp in range(o_ref.shape[0]):
        outs = []
        for h in (2 * pp, 2 * pp + 1):
            w = jnp.concatenate([qT_ref[B_QK * h:B_QK * (h + 1), :], zpad], axis=0)
            s_c = _dot(kc_ref[h], w)
            s_n = _dot(kn_ref[h], w) + new_mask
            m = jnp.maximum(jnp.max(s_c, axis=0, keepdims=True), jnp.max(s_n, axis=0, keepdims=True))
            p_c = jnp.exp2(s_c - m)
            p_n = jnp.exp2(s_n - m)
            l = jnp.sum(p_c, axis=0, keepdims=True) + jnp.sum(p_n, axis=0, keepdims=True)
            rows = slice(B_VDIM * h, B_VDIM * (h + 1))
            o_t = _dot(vc_ref[rows, :], p_c.astype(bf16)) + _dot(vn_ref[rows, :], p_n.astype(bf16))
            outs.append(o_t * (1.0 / l))
        o_ref[pp] = _gated(jnp.concatenate(outs, axis=0), zs_ref.at[pp], slice(None))


def _b_sample_attention(k_cache, k_new, v_cache, v_new, q_t, zs, n_new):
    b, _, past, _ = k_cache.shape
    t = SAMPLE_PAD
    pps = PAIRS_PER_STEP
    slab = pl.BlockSpec((None, pps, t, LANES), lambda bi, p: (bi, p, 0, 0))
    return pl.pallas_call(
        functools.partial(_b_sattn_body, n_new),
        grid=(b, N_PAIRS // pps),
        in_specs=[pl.BlockSpec((None, 2 * pps, past, 128), lambda bi, p: (bi, p, 0, 0)),
                  pl.BlockSpec((None, 2 * pps, t, 128), lambda bi, p: (bi, p, 0, 0)),
                  pl.BlockSpec((None, 2 * pps * B_VDIM, past), lambda bi, p: (bi, p, 0)),
                  pl.BlockSpec((None, 2 * pps * B_VDIM, t), lambda bi, p: (bi, p, 0)),
                  pl.BlockSpec((None, 2 * pps * B_QK, t), lambda bi, p: (bi, p, 0)),
                  slab],
        out_specs=slab,
        out_shape=jax.ShapeDtypeStruct((b, N_PAIRS, t, LANES), bf16),
        compiler_params=_params("arbitrary", "arbitrary"),
        name="b_sample_attention",
    )(k_cache, k_new, v_cache, v_new, q_t, zs)


def _bias_rows(table):
    left = A_BIAS_ROWS - 1 - WINDOW - REL_CLIP
    right = A_BIAS_SPAN - left - table.shape[-1]
    return (jnp.pad(table.astype(f32), ((0, 0), (left, right)), mode="edge") * LOG2E)[:, None, :]


def _rope_tables(pos):
    half = B_ROPE // 2
    inv = ROPE_THETA ** (-jnp.arange(half, dtype=f32) / half)
    ang = pos.astype(f32)[:, None] * inv[None, :]
    return jnp.cos(ang).T, jnp.sin(ang).T


def _col(g, scale=1.0):
    return (g.astype(f32) * scale)[:, None]


def kernel(x_prompt, x_sample, cache_a_k, cache_a_v, cache_mla_ckv, cache_mla_krope, c_prompt, c_sample, norm_g, ada_w, ada_b, a_w_in, a_g_q, a_g_k, a_rel_bias, a_w_out, b_w_in, b_g_cq, b_w_uq, b_g_ckv, b_w_ukv, b_g_qn, b_g_qr, b_g_kn, b_g_kr, b_w_out):
    bp, seq, d = x_prompt.shape
    bs, dec, _ = x_sample.shape
    past = cache_mla_ckv.shape[2]
    n_cache_a = cache_a_k.shape[2]
    assert d == N_HEADS * A_HEAD_DIM and seq % (2 * Q_GROUP) == 0 and seq >= A_KEYS
    assert past % CHUNK == 0 and dec <= CHUNK and dec <= SAMPLE_PAD and n_cache_a == WINDOW
    cache_rows = min(WINDOW, seq)
    tm = min(ROW_TILE, seq)

    mod = _modulation(jnp.concatenate([c_prompt, c_sample], axis=0), ada_w, ada_b)
    mod = mod.reshape(mod.shape[0], bp + bs, 3, d)
    xs_pad = jnp.pad(x_sample, ((0, 0), (0, SAMPLE_PAD - dec), (0, 0)))

    w_in = a_w_in[0]
    wq_t, wk_t, wv_t = (w_in[:, d * n:d * (n + 1)].T.astype(bf16) for n in range(3))
    wz = w_in[:, 3 * d:].astype(bf16)
    w_out_a = a_w_out[0].astype(bf16)
    gq = _col(a_g_q[0], A_HEAD_DIM ** -0.5 * LOG2E)
    gk = _col(a_g_k[0])
    g0 = norm_g[0][None, :]
    f_rows = _bias_rows(a_rel_bias[0])
    mod_p0, mod_s0 = mod[0, :bp], mod[0, bp:]

    q_t, k_nat, v_t, zs, k32, v32 = _a_project(x_prompt, mod_p0, g0, wq_t, wk_t, wv_t, wz, gq, gk, tm, cache_rows)
    qk_max = A_HEAD_DIM * jnp.max(jnp.abs(gq)) * jnp.max(jnp.abs(gk)) * BOUND_MARGIN
    upper = qk_max + jnp.max(f_rows)
    oz_p0 = lax.cond(upper - (jnp.min(f_rows) - qk_max) <= MAX_LOGIT_RANGE,
                     lambda: _a_attention_bounded(upper[None], f_rows, k_nat, q_t, v_t, zs),
                     lambda: _a_attention(f_rows, k_nat, q_t, v_t, zs))
    new_a_k_p = k32.reshape(1, bp, cache_rows, N_HEADS, A_HEAD_DIM)
    new_a_v_p = v32.reshape(1, bp, cache_rows, N_HEADS, A_HEAD_DIM)

    q_t, k_nat, v_t, zs, k32, v32 = _a_project(xs_pad, mod_s0, g0, wq_t, wk_t, wv_t, wz, gq, gk,
                                               SAMPLE_PAD, SAMPLE_PAD)
    oz_s0 = _a_sample_attention(f_rows, cache_a_k[0].reshape(bs, n_cache_a, d), k_nat,
                                cache_a_v[0].reshape(bs, n_cache_a, d), v32, q_t, zs, dec)
    new_a_k_s = k32[:, :dec].reshape(1, bs, dec, N_HEADS, A_HEAD_DIM)
    new_a_v_s = v32[:, :dec].reshape(1, bs, dec, N_HEADS, A_HEAD_DIM)

    w_in = b_w_in[0]
    scale = B_QK ** -0.5 * LOG2E
    wb = {
        "cq_t": w_in[:, :Q_LORA].T.astype(bf16),
        "ckv_t": w_in[:, Q_LORA:Q_LORA + KV_LORA].T.astype(bf16),
        "kr_t": w_in[:, Q_LORA + KV_LORA:Q_LORA + KV_LORA + B_ROPE].T.astype(bf16),
        "z": w_in[:, Q_LORA + KV_LORA + B_ROPE:].astype(bf16),
        "uq_t": b_w_uq[0].T.astype(bf16),
        "g_cq": _col(b_g_cq[0]), "g_ckv": _col(b_g_ckv[0]), "g_kr": _col(b_g_kr[0]),
        "g_qn": _col(b_g_qn[0], scale), "g_qr": _col(b_g_qr[0], scale),
        "ukv_t": b_w_ukv[0].T.astype(bf16), "g_kn": _col(b_g_kn[0]),
    }
    wukv_t, g_kn = wb["ukv_t"], wb["g_kn"]
    w_out = b_w_out[0].astype(bf16)
    g1 = norm_g[1][None, :]
    mod_p, mod_s = mod[1, :bp], mod[1, bp:]

    cos_t, sin_t = _rope_tables(jnp.arange(seq, dtype=jnp.int32))
    y_p, q_t, ckv_p, kr_p, zs, k_nat, v_t = _b_project(oz_p0, w_out_a, x_prompt, mod_p0, mod_p, g1, wb,
                                                       cos_t, sin_t, min(B_PROJ_TILE, seq))
    q_sq = B_NOPE * jnp.max(jnp.abs(wb["g_qn"])) ** 2 + B_ROPE * jnp.max(jnp.abs(wb["g_qr"])) ** 2
    k_sq = B_NOPE * jnp.max(jnp.abs(g_kn)) ** 2 + B_ROPE * jnp.max(jnp.abs(wb["g_kr"])) ** 2
    qk_max = jnp.sqrt(q_sq * k_sq) * BOUND_MARGIN
    oz = lax.cond(2.0 * qk_max <= MAX_LOGIT_RANGE,
                  lambda: _b_attention_bounded(qk_max[None], k_nat, q_t, v_t, zs),
                  lambda: _b_attention(k_nat, q_t, v_t, zs))
    y_p = _out_project(oz, w_out, y_p, mod_p, min(OUT_TILE, seq))

    cos_t, sin_t = _rope_tables(past + jnp.arange(SAMPLE_PAD, dtype=jnp.int32))
    ys_pad, q_t, ckv_s, kr_s, zs, k_new, v_new = _b_project(oz_s0, w_out_a, xs_pad, mod_s0, mod_s, g1, wb,
                                                            cos_t, sin_t, SAMPLE_PAD)
    k_old, v_old = _b_kv_up(cache_mla_ckv[0], cache_mla_krope[0], wukv_t, g_kn, tm)
    oz = _b_sample_attention(k_old, k_new, v_old, v_new, q_t, zs, dec)
    ys_pad = _out_project(oz, w_out, ys_pad, mod_s, SAMPLE_PAD)

    return (y_p, ys_pad[:, :dec], new_a_k_p, new_a_v_p, new_a_k_s, new_a_v_s,
            ckv_p[None], kr_p[None], ckv_s[None, :, :dec], kr_s[None, :, :dec])
```
